```python
import math, functools
import jax, jax.numpy as jnp
from jax import lax
import numpy as np

D_MODEL = 2048
BATCH = 1
SEQ = 8192
DEPTH = 1
DEC_BATCH = 128
DEC_SEQ = 1
PAST_LEN = 16384
PAGE_SIZE = 128

N_META = 16
HGRN_HEADS = 8
HGRN_K = 128
HGRN_V = 128
HGRN_WIDTH = HGRN_HEADS * HGRN_V
CHUNK = 128
MLA_HEADS = 8
QK_NOPE = 128
QK_ROPE = 64
V_HEAD = 128
Q_RANK = 512
KV_RANK = 256
MLA_WIDTH = MLA_HEADS * V_HEAD
MIX_WIDTH = HGRN_WIDTH + MLA_WIDTH
ROPE_THETA = 10000.0
SOFTMAX_SCALE = (QK_NOPE + QK_ROPE) ** -0.5
Q_BLOCK = 128
D_FF = 5632
CONV_WIDTH = 3
EPS = 1e-6
IN_SPLITS = (HGRN_HEADS * HGRN_K, HGRN_HEADS * HGRN_K, HGRN_WIDTH, HGRN_WIDTH, Q_RANK, KV_RANK, QK_ROPE)
IN_TOTAL = 2 * HGRN_HEADS * HGRN_K + 2 * HGRN_WIDTH + Q_RANK + KV_RANK + QK_ROPE

kernel_name = 'hymba_hgrn2_mla_convffn_step'


def rmsnorm(x, g):
    xf = x.astype(jnp.float32)
    y = xf * lax.rsqrt(jnp.mean(xf * xf, axis=-1, keepdims=True) + EPS)
    return (y * g.astype(jnp.float32)).astype(x.dtype)


def rope(x, pos):
    half = QK_ROPE // 2
    inv = ROPE_THETA ** (-jnp.arange(half, dtype=jnp.float32) / half)
    ang = pos.astype(jnp.float32)[:, None] * inv[None, :]
    cos = jnp.cos(ang)[:, None, :]
    sin = jnp.sin(ang)[:, None, :]
    xf = x.astype(jnp.float32)
    x1, x2 = xf[..., :half], xf[..., half:]
    return jnp.concatenate([x1 * cos - x2 * sin, x1 * sin + x2 * cos], axis=-1).astype(x.dtype)


def gla_chunk(S0, q, k, v, logf):
    C = q.shape[1]
    b = jnp.cumsum(logf, axis=1)
    causal = jnp.tril(jnp.ones((C, C), bool))
    diff = b[:, :, None] - b[:, None, :]
    decay = jnp.exp(jnp.where(causal[None, :, :, None, None], diff, -jnp.inf))
    scores = jnp.einsum('bthk,bshk,btshk->bhts', q, k, decay)
    o = (jnp.einsum('bthk,bhkv->bthv', q * jnp.exp(b), S0)
         + jnp.einsum('bhts,bshv->bthv', scores, v))
    b_last = b[:, -1]
    S = (jnp.exp(b_last)[..., None] * S0
         + jnp.einsum('bshk,bshv->bhkv', k * jnp.exp(b_last[:, None] - b), v))
    return S, o


def hgrn_prompt(q, k, v, logf):
    B = q.shape[0]
    S0 = jnp.zeros((B, HGRN_HEADS, HGRN_K, HGRN_V), jnp.float32)
    S, o_meta = gla_chunk(S0, q[:, :N_META], k[:, :N_META], v[:, :N_META], logf[:, :N_META])

    def to_chunks(a):
        r = a[:, N_META:]
        nc = r.shape[1] // CHUNK
        return r.reshape((B, nc, CHUNK) + a.shape[2:]).swapaxes(0, 1)

    def step(S_c, xs):
        return gla_chunk(S_c, xs[0], xs[1], xs[2], xs[3])

    S, o_real = lax.scan(step, S, (to_chunks(q), to_chunks(k), to_chunks(v), to_chunks(logf)))
    o_real = o_real.swapaxes(0, 1).reshape(B, -1, HGRN_HEADS, HGRN_V)
    return S, jnp.concatenate([o_meta, o_real], axis=1)


def mla_project(q_down, kv_down, k_rope_raw, pos, q_a_norm, kv_a_norm, w_q_up):
    B, T, _ = q_down.shape
    q = (rmsnorm(q_down, q_a_norm) @ w_q_up).reshape(B, T, MLA_HEADS, QK_NOPE + QK_ROPE)
    q_nope = q[..., :QK_NOPE]
    q_pe = rope(q[..., QK_NOPE:], pos)
    c_kv = rmsnorm(kv_down, kv_a_norm)
    k_pe = rope(k_rope_raw[:, :, None, :], pos)[:, :, 0, :]
    return q_nope, q_pe, c_kv, k_pe


def mla_prompt(q_nope, q_pe, c_kv, k_pe, w_uk, w_uv):
    B, L = c_kv.shape[:2]
    k_nope = jnp.einsum('bsr,rhn->bshn', c_kv, w_uk)
    v = jnp.einsum('bsr,rhv->bshv', c_kv, w_uv)
    n_blk = -(-L // Q_BLOCK)
    Lp = n_blk * Q_BLOCK
    pad = ((0, 0), (0, Lp - L), (0, 0), (0, 0))
    qn = jnp.pad(q_nope, pad).reshape(B, n_blk, Q_BLOCK, MLA_HEADS, QK_NOPE).swapaxes(0, 1)
    qp = jnp.pad(q_pe, pad).reshape(B, n_blk, Q_BLOCK, MLA_HEADS, QK_ROPE).swapaxes(0, 1)
    qpos = jnp.arange(Lp).reshape(n_blk, Q_BLOCK)
    kpos = jnp.arange(L)

    def attend(args):
        qn_b, qp_b, qpos_b = args
        s = (jnp.einsum('bqhn,bkhn->bhqk', qn_b, k_nope)
             + jnp.einsum('bqhp,bkp->bhqk', qp_b, k_pe)).astype(jnp.float32) * SOFTMAX_SCALE
        s = jnp.where(kpos[None, :] <= qpos_b[:, None], s, -jnp.inf)
        p = jax.nn.softmax(s, axis=-1).astype(v.dtype)
        return jnp.einsum('bhqk,bkhv->bqhv', p, v)

    o = lax.map(attend, (qn, qp, qpos))
    return o.swapaxes(0, 1).reshape(B, Lp, MLA_WIDTH)[:, :L]


def mla_sample(ckv_past, kpe_past, q_nope, q_pe, c_kv, k_pe, w_uk, w_uv):
    B, T = c_kv.shape[:2]
    P = ckv_past.shape[1]
    q_lat = jnp.einsum('bthn,rhn->bthr', q_nope, w_uk)
    s_past = (jnp.einsum('bthr,bsr->bhts', q_lat, ckv_past)
              + jnp.einsum('bthp,bsp->bhts', q_pe, kpe_past))
    s_new = (jnp.einsum('bthr,bsr->bhts', q_lat, c_kv)
             + jnp.einsum('bthp,bsp->bhts', q_pe, k_pe))
    s = jnp.concatenate([s_past, s_new], axis=-1).astype(jnp.float32) * SOFTMAX_SCALE
    mask = jnp.concatenate([jnp.ones((T, P), bool), jnp.tril(jnp.ones((T, T), bool))], axis=1)
    s = jnp.where(mask, s, -jnp.inf)
    p = jax.nn.softmax(s, axis=-1).astype(c_kv.dtype)
    o_lat = (jnp.einsum('bhts,bsr->bthr', p[..., :P], ckv_past)
             + jnp.einsum('bhts,bsr->bthr', p[..., P:], c_kv))
    return jnp.einsum('bthr,rhv->bthv', o_lat, w_uv).reshape(B, T, MLA_WIDTH)


def conv_ffn(h, buf, w_ffn_up, conv_w, conv_b, w_ffn_down):
    T = h.shape[1]
    u = h @ w_ffn_up
    gate_pre, val = u[..., :D_FF], u[..., D_FF:]
    ext = jnp.concatenate([buf.astype(h.dtype), gate_pre], axis=1)
    y = conv_b
    for j in range(CONV_WIDTH):
        y = y + ext[:, j:j + T] * conv_w[j]
    new_buf = ext[:, ext.shape[1] - (CONV_WIDTH - 1):]
    return (jax.nn.silu(y) * val) @ w_ffn_down, new_buf


def layer(x, pos, hgrn_S0, conv_buf, mla_mix, lb, g_mix_pre, g_mix_post, g_ffn_pre, g_ffn_post,
          w_in, hgrn_o_norm, q_a_norm, kv_a_norm, w_q_up, w_kv_up, w_out,
          w_ffn_up, conv_w, conv_b, w_ffn_down):
    B, T, _ = x.shape
    h = rmsnorm(x, g_mix_pre)
    proj = h @ w_in
    idx = np.cumsum(IN_SPLITS)[:-1].tolist()
    hq, hf, hi, hg, q_down, kv_down, k_rope_raw = jnp.split(proj, idx, axis=-1)

    kshape = (B, T, HGRN_HEADS, HGRN_K)
    f = lb + (1.0 - lb) * jax.nn.sigmoid(hf.astype(jnp.float32))
    q = jax.nn.silu(hq.astype(jnp.float32)).reshape(kshape)
    k = (1.0 - f).reshape(kshape)
    logf = jnp.log(f).reshape(kshape)
    v = hi.astype(jnp.float32).reshape(B, T, HGRN_HEADS, HGRN_V)
    if hgrn_S0 is None:
        S, o = hgrn_prompt(q, k, v, logf)
    else:
        S, o = gla_chunk(hgrn_S0.astype(jnp.float32), q, k, v, logf)
    gate = jax.nn.silu(hg.astype(jnp.float32)).reshape(B, T, HGRN_HEADS, HGRN_V)
    o_h = (rmsnorm(o, hgrn_o_norm) * gate).reshape(B, T, HGRN_WIDTH).astype(x.dtype)

    q_nope, q_pe, c_kv, k_pe = mla_project(q_down, kv_down, k_rope_raw, pos, q_a_norm, kv_a_norm, w_q_up)
    w_kv = w_kv_up.reshape(KV_RANK, MLA_HEADS, QK_NOPE + V_HEAD)
    o_a = mla_mix(q_nope, q_pe, c_kv, k_pe, w_kv[..., :QK_NOPE], w_kv[..., QK_NOPE:])

    mix = jnp.concatenate([o_h, o_a.astype(x.dtype)], axis=-1) @ w_out
    x = x + rmsnorm(mix, g_mix_post)
    f_out, new_buf = conv_ffn(rmsnorm(x, g_ffn_pre), conv_buf, w_ffn_up, conv_w, conv_b, w_ffn_down)
    x = x + rmsnorm(f_out, g_ffn_post)
    return x, S.astype(x.dtype), c_kv, k_pe, new_buf


def setup_inputs(seed: int = 0) -> dict:
    key = jax.random.key(seed)
    ks = jax.random.split(key, 24)
    f32 = jnp.float32
    n_pages = PAST_LEN // PAGE_SIZE
    n_used = DEC_BATCH * n_pages
    n_pool = n_used + max(1, n_used // 4)

    def nrm(k, shape, scale=1.0):
        return scale * jax.random.normal(k, shape, f32)

    def gain(k, shape):
        return 1.0 + 0.01 * jax.random.normal(k, shape, f32)

    page_table = jax.random.permutation(ks[6], n_pool)[:n_used].reshape(DEC_BATCH, n_pages).astype(jnp.int32)
    return {
        'x_prompt': nrm(ks[0], (BATCH, SEQ, D_MODEL)),
        'x_sample': nrm(ks[1], (DEC_BATCH, DEC_SEQ, D_MODEL)),
        'cache_ckv': nrm(ks[2], (DEPTH, n_pool, PAGE_SIZE, KV_RANK)),
        'cache_krope': nrm(ks[3], (DEPTH, n_pool, PAGE_SIZE, QK_ROPE)),
        'state_hgrn': nrm(ks[4], (DEPTH, DEC_BATCH, HGRN_HEADS, HGRN_K, HGRN_V), 0.3),
        'state_conv': nrm(ks[5], (DEPTH, DEC_BATCH, CONV_WIDTH - 1, D_FF)),
        'page_table': page_table,
        'meta_tokens': nrm(ks[7], (N_META, D_MODEL)),
        'lb_raw': nrm(ks[8], (DEPTH + 1, HGRN_HEADS * HGRN_K), 0.5),
        'g_mix_pre': gain(ks[9], (DEPTH, D_MODEL)),
        'g_mix_post': gain(ks[10], (DEPTH, D_MODEL)),
        'g_ffn_pre': gain(ks[11], (DEPTH, D_MODEL)),
        'g_ffn_post': gain(ks[12], (DEPTH, D_MODEL)),
        'w_in': nrm(ks[13], (DEPTH, D_MODEL, IN_TOTAL), D_MODEL ** -0.5),
        'hgrn_o_norm': gain(ks[14], (DEPTH, HGRN_V)),
        'q_a_norm': gain(ks[15], (DEPTH, Q_RANK)),
        'kv_a_norm': gain(ks[16], (DEPTH, KV_RANK)),
        'w_q_up': nrm(ks[17], (DEPTH, Q_RANK, MLA_HEADS * (QK_NOPE + QK_ROPE)), Q_RANK ** -0.5),
        'w_kv_up': nrm(ks[18], (DEPTH, KV_RANK, MLA_HEADS * (QK_NOPE + V_HEAD)), KV_RANK ** -0.5),
        'w_out': nrm(ks[19], (DEPTH, MIX_WIDTH, D_MODEL), MIX_WIDTH ** -0.5),
        'w_ffn_up': nrm(ks[20], (DEPTH, D_MODEL, 2 * D_FF), D_MODEL ** -0.5),
        'conv_w': nrm(ks[21], (DEPTH, CONV_WIDTH, D_FF), CONV_WIDTH ** -0.5),
        'conv_b': nrm(ks[22], (DEPTH, D_FF), 0.01),
        'w_ffn_down': nrm(ks[23], (DEPTH, D_FF, D_MODEL), D_FF ** -0.5),
    }


def reference(x_prompt, x_sample, cache_ckv, cache_krope, state_hgrn, state_conv, page_table,
              meta_tokens, lb_raw, g_mix_pre, g_mix_post, g_ffn_pre, g_ffn_post, w_in, hgrn_o_norm,
              q_a_norm, kv_a_norm, w_q_up, w_kv_up, w_out, w_ffn_up, conv_w, conv_b, w_ffn_down):
    B = x_prompt.shape[0]
    n_seq = page_table.shape[0]
    meta = jnp.broadcast_to(meta_tokens[None].astype(x_prompt.dtype), (B, N_META, D_MODEL))
    xp = jnp.concatenate([meta, x_prompt], axis=1)
    pos_p = jnp.arange(xp.shape[1])
    xs = x_sample
    pos_s = PAST_LEN + jnp.arange(xs.shape[1])
    lbs = jnp.cumsum(jax.nn.softmax(lb_raw.astype(jnp.float32), axis=0), axis=0)
    conv_zero = jnp.zeros((B, CONV_WIDTH - 1, D_FF), x_prompt.dtype)

    ckv_p_l, kpe_p_l, ckv_s_l, kpe_s_l = [], [], [], []
    hg_p_l, hg_s_l, cv_p_l, cv_s_l = [], [], [], []
    for l in range(DEPTH):
        wl = (lbs[l], g_mix_pre[l], g_mix_post[l], g_ffn_pre[l], g_ffn_post[l], w_in[l], hgrn_o_norm[l],
              q_a_norm[l], kv_a_norm[l], w_q_up[l], w_kv_up[l], w_out[l],
              w_ffn_up[l], conv_w[l], conv_b[l], w_ffn_down[l])
        xp, s_p, ckv_p, kpe_p, cb_p = layer(xp, pos_p, None, conv_zero, mla_prompt, *wl)
        ckv_past = cache_ckv[l, page_table].reshape(n_seq, -1, KV_RANK)
        kpe_past = cache_krope[l, page_table].reshape(n_seq, -1, QK_ROPE)
        xs, s_s, ckv_s, kpe_s, cb_s = layer(xs, pos_s, state_hgrn[l], state_conv[l],
                                            functools.partial(mla_sample, ckv_past, kpe_past), *wl)
        ckv_p_l.append(ckv_p); kpe_p_l.append(kpe_p); ckv_s_l.append(ckv_s); kpe_s_l.append(kpe_s)
        hg_p_l.append(s_p); hg_s_l.append(s_s); cv_p_l.append(cb_p); cv_s_l.append(cb_s)

    return (xp[:, N_META:], xs,
            jnp.stack(ckv_p_l), jnp.stack(kpe_p_l), jnp.stack(ckv_s_l), jnp.stack(kpe_s_l),
            jnp.stack(hg_p_l), jnp.stack(hg_s_l), jnp.stack(cv_p_l), jnp.stack(cv_s_l))
```

```python
import functools

import numpy as np
import jax
import jax.numpy as jnp
from jax import lax
from jax.experimental import pallas as pl
from jax.experimental.pallas import tpu as pltpu

D_MODEL = 2048
SEQ = 8192
N_META = 16
DEC_BATCH = 128
PAST_LEN = 16384
PAGE_SIZE = 128
N_PAGES = PAST_LEN // PAGE_SIZE
HGRN_HEADS = 8
HGRN_K = 128
HGRN_V = 128
HGRN_WIDTH = HGRN_HEADS * HGRN_V
MLA_HEADS = 8
QK_NOPE = 128
QK_ROPE = 64
QK_DIM = QK_NOPE + QK_ROPE
V_HEAD = 128
Q_RANK = 512
KV_RANK = 256
MLA_WIDTH = MLA_HEADS * V_HEAD
ROPE_THETA = 10000.0
SOFTMAX_SCALE = QK_DIM ** -0.5
D_FF = 5632
EPS = 1e-6
IN_TOTAL = 4 * HGRN_WIDTH + Q_RANK + KV_RANK + QK_ROPE

CHUNK = 128
PAD = CHUNK - N_META
P_START = PAD + N_META
P_ROWS = P_START + SEQ
R_ALL = P_ROWS + DEC_BATCH
N_CHUNKS = P_ROWS // CHUNK

LANES = 128
IN_PAD = 5120
IN_TN = 1024
TM = 768
TM_OUT = 384
TQ = 640
FF_TN = 512
SAMPLE_LOCAL = P_ROWS - (R_ALL // TM - 1) * TM
TAIL_ROWS = 136
PAGES_PER_STEP = 16
N_PAGE_STEPS = N_PAGES // PAGES_PER_STEP
KEYS_PER_STEP = PAGES_PER_STEP * PAGE_SIZE
NEG_BIG = -1e30
VMEM_LIMIT = 56 * 1024 * 1024

NT_DIMS = (((1,), (1,)), ((), ()))
TN_DIMS = (((0,), (0,)), ((), ()))

f32 = jnp.float32
bf16 = jnp.bfloat16


def _params(sem, vmem=VMEM_LIMIT):
    return pltpu.CompilerParams(dimension_semantics=sem, vmem_limit_bytes=vmem)


def _rms(x, g):
    return x * lax.rsqrt(jnp.mean(x * x, axis=-1, keepdims=True) + EPS) * g


def _sigmoid(x):
    return 1.0 / (1.0 + jnp.exp(-x))


def _silu(x):
    return x * _sigmoid(x)


def _dot(a, b):
    return jnp.dot(a, b, preferred_element_type=f32)


def _dot_nt(a, b):
    return lax.dot_general(a, b, NT_DIMS, preferred_element_type=f32)


def _dot_tn(a, b):
    return lax.dot_general(a, b, TN_DIMS, preferred_element_type=f32)


def _in_proj_kernel(x_ref, g_ref, w_ref, o_ref, h_ref):
    @pl.when(pl.program_id(1) == 0)
    def _():
        h_ref[...] = _rms(x_ref[...], g_ref[...]).astype(bf16)

    o_ref[...] = _dot(h_ref[...], w_ref[...])


def _in_proj(x_all, g, w_in_p):
    return pl.pallas_call(
        _in_proj_kernel,
        grid=(R_ALL // TM, IN_PAD // IN_TN),
        in_specs=[
            pl.BlockSpec((TM, D_MODEL), lambda i, j: (i, 0)),
            pl.BlockSpec((1, D_MODEL), lambda i, j: (0, 0)),
            pl.BlockSpec((D_MODEL, IN_TN), lambda i, j: (0, j)),
        ],
        out_specs=pl.BlockSpec((TM, IN_TN), lambda i, j: (i, j)),
        out_shape=jax.ShapeDtypeStruct((R_ALL, IN_PAD), f32),
        scratch_shapes=[pltpu.VMEM((TM, D_MODEL), bf16)],
        compiler_params=_params(("arbitrary", "arbitrary")),
        name="in_proj",
    )(x_all, g, w_in_p)


def _rope_tables(row0, n_rows):
    lane = lax.broadcasted_iota(jnp.int32, (1, LANES), 1)
    half = QK_ROPE // 2
    fidx = (lane & (half - 1)).astype(f32)
    inv = jnp.power(jnp.float32(ROPE_THETA), -fidx / half)
    rows = row0 + lax.broadcasted_iota(jnp.int32, (n_rows, 1), 0)
    pos = jnp.where(rows < P_ROWS, rows - PAD, PAST_LEN).astype(f32)
    ang = pos * inv
    sin = jnp.sin(ang)
    return jnp.cos(ang), jnp.where((lane & half) == 0, -sin, sin)


def _rotate_half(x, cos, sin_signed):
    n = x.shape[1]
    half = QK_ROPE // 2
    first_half = (lax.broadcasted_iota(jnp.int32, (1, n), 1) & half) == 0
    partner = jnp.where(first_half, pltpu.roll(x, n - half, 1), pltpu.roll(x, half, 1))
    return x * cos + partner * sin_signed


def _mla_proj_kernel(qd_ref, kvd_ref, kr_ref, qn_ref, kvn_ref, wq_ref, wkv_ref,
                     qh_ref, kh_ref, vh_ref, ckv_ref, kpe_ref):
    tm = qd_ref.shape[0]
    cos, sin_s = _rope_tables(pl.program_id(0) * tm, tm)

    q = _dot(_rms(qd_ref[...], qn_ref[...]).astype(bf16), wq_ref[...])
    reps = MLA_HEADS * QK_ROPE // LANES
    q_pe = _rotate_half(q[:, MLA_HEADS * QK_NOPE:],
                        jnp.concatenate([cos] * reps, axis=1),
                        jnp.concatenate([sin_s] * reps, axis=1))

    k_pe = _rotate_half(kr_ref[...], cos, sin_s)[:, :QK_ROPE]
    kpe_ref[...] = k_pe

    c_kv = _rms(kvd_ref[...], kvn_ref[...])
    ckv_ref[...] = c_kv
    kv = _dot(c_kv.astype(bf16), wkv_ref[...])

    k_pe_b = k_pe.astype(bf16)
    for h in range(MLA_HEADS):
        qh_ref[h, :, :QK_NOPE] = q[:, h * QK_NOPE:(h + 1) * QK_NOPE].astype(bf16)
        qh_ref[h, :, QK_NOPE:] = q_pe[:, h * QK_ROPE:(h + 1) * QK_ROPE].astype(bf16)
        base = h * (QK_NOPE + V_HEAD)
        kh_ref[h, :, :QK_NOPE] = kv[:, base:base + QK_NOPE].astype(bf16)
        kh_ref[h, :, QK_NOPE:] = k_pe_b
        vh_ref[h] = kv[:, base + QK_NOPE:base + QK_NOPE + V_HEAD].astype(bf16)


def _mla_proj(proj, q_a_norm, kv_a_norm, wq_p, wkv):
    col_q = (4 * HGRN_WIDTH) // Q_RANK
    col_kv = (4 * HGRN_WIDTH + Q_RANK) // KV_RANK
    col_kr = (4 * HGRN_WIDTH + Q_RANK + KV_RANK) // LANES
    head_spec = lambda d: pl.BlockSpec((MLA_HEADS, TM, d), lambda i: (0, i, 0))
    return pl.pallas_call(
        _mla_proj_kernel,
        grid=(R_ALL // TM,),
        in_specs=[
            pl.BlockSpec((TM, Q_RANK), lambda i: (i, col_q)),
            pl.BlockSpec((TM, KV_RANK), lambda i: (i, col_kv)),
            pl.BlockSpec((TM, LANES), lambda i: (i, col_kr)),
            pl.BlockSpec((1, Q_RANK), lambda i: (0, 0)),
            pl.BlockSpec((1, KV_RANK), lambda i: (0, 0)),
            pl.BlockSpec(wq_p.shape, lambda i: (0, 0)),
            pl.BlockSpec(wkv.shape, lambda i: (0, 0)),
        ],
        out_specs=[
            head_spec(QK_DIM), head_spec(QK_DIM), head_spec(V_HEAD),
            pl.BlockSpec((TM, KV_RANK), lambda i: (i, 0)),
            pl.BlockSpec((TM, QK_ROPE), lambda i: (i, 0)),
        ],
        out_shape=[
            jax.ShapeDtypeStruct((MLA_HEADS, R_ALL, QK_DIM), bf16),
            jax.ShapeDtypeStruct((MLA_HEADS, R_ALL, QK_DIM), bf16),
            jax.ShapeDtypeStruct((MLA_HEADS, R_ALL, V_HEAD), bf16),
            jax.ShapeDtypeStruct((R_ALL, KV_RANK), f32),
            jax.ShapeDtypeStruct((R_ALL, QK_ROPE), f32),
        ],
        compiler_params=_params(("arbitrary",)),
        name="mla_proj",
    )(proj, proj, proj, q_a_norm, kv_a_norm, wq_p, wkv)


def _attn_kernel(q_ref, k_ref, v_ref, o_ref, m_ref, l_ref, acc_ref):
    i = pl.program_id(1)
    q = q_ref[0]
    m_ref[...] = jnp.full(m_ref.shape, NEG_BIG, f32)
    l_ref[...] = jnp.zeros(l_ref.shape, f32)
    acc_ref[...] = jnp.zeros(acc_ref.shape, f32)
    q_row = i * TQ + lax.broadcasted_iota(jnp.int32, (TQ, TQ), 0)
    k_col = lax.broadcasted_iota(jnp.int32, (TQ, TQ), 1)
    reps = TQ // LANES

    def body(j, carry):
        start = pl.multiple_of(j * TQ, TQ)
        k = k_ref[0, pl.ds(start, TQ), :]
        v = v_ref[0, pl.ds(start, TQ), :]
        s = _dot_nt(q, k) * SOFTMAX_SCALE
        k_row = k_col + j * TQ
        s = jnp.where(k_row <= q_row, jnp.where(k_row >= PAD, s, NEG_BIG), NEG_BIG)
        m_prev = m_ref[...]
        m_next = jnp.maximum(m_prev, jnp.max(s, axis=1, keepdims=True))
        p = jnp.exp(s - jnp.concatenate([m_next] * reps, axis=1))
        alpha = jnp.exp(m_prev - m_next)
        l_ref[...] = alpha * l_ref[...] + jnp.sum(p, axis=1, keepdims=True)
        acc_ref[...] = alpha * acc_ref[...] + _dot(p.astype(bf16), v)
        m_ref[...] = m_next
        return carry

    lax.fori_loop(0, i + 1, body, 0)
    o_ref[...] = (acc_ref[...] / l_ref[...]).astype(bf16)


def _attention(qh, kh, vh):
    return pl.pallas_call(
        _attn_kernel,
        grid=(MLA_HEADS, P_ROWS // TQ),
        in_specs=[
            pl.BlockSpec((1, TQ, QK_DIM), lambda h, i: (h, i, 0)),
            pl.BlockSpec((1, R_ALL, QK_DIM), lambda h, i: (h, 0, 0)),
            pl.BlockSpec((1, R_ALL, V_HEAD), lambda h, i: (h, 0, 0)),
        ],
        out_specs=pl.BlockSpec((TQ, V_HEAD), lambda h, i: (i, h)),
        out_shape=jax.ShapeDtypeStruct((P_ROWS, MLA_WIDTH), bf16),
        scratch_shapes=[pltpu.VMEM((TQ, LANES), f32), pltpu.VMEM((TQ, LANES), f32),
                        pltpu.VMEM((TQ, V_HEAD), f32)],
        compiler_params=_params(("arbitrary", "arbitrary")),
        name="prompt_attn",
    )(qh, kh, vh)


N_LEVELS = 7


def _hgrn_constants():
    t = np.arange(CHUNK)
    u = np.arange(CHUNK)
    blocks = []
    for lvl in range(N_LEVELS):
        m = 1 << lvl
        blk = t // m
        is_q = (blk % 2) == 1
        start = blk * m
        end = (blk + 1) * m
        sel_q = (u[None, :] >= start[:, None]) & (u[None, :] <= t[:, None])
        sel_k = (u[None, :] > t[:, None]) & (u[None, :] < end[:, None])
        blocks.append(np.where(is_q[:, None], sel_q, sel_k))
    blocks.append(u[None, :] <= t[:, None])
    blocks.append(u[None, :] > t[:, None])
    sums = np.concatenate(blocks, axis=0).astype(np.float32)
    x = t[:, None] ^ t[None, :]
    lev = np.where(x > 0, np.floor(np.log2(np.maximum(x, 1))).astype(np.int32), N_LEVELS)
    lev = np.where(t[None, :] > t[:, None], N_LEVELS + 1, lev).astype(np.int32)
    return sums, lev


def _split3(x):
    a = x.astype(bf16)
    r = x - a.astype(f32)
    b = r.astype(bf16)
    c = (r - b.astype(f32)).astype(bf16)
    return a, b, c


def _lower_bound(lb_raw_ref):
    a0 = lb_raw_ref[0:1, :]
    a1 = lb_raw_ref[1:2, :]
    m = jnp.maximum(a0, a1)
    e0 = jnp.exp(a0 - m)
    return e0 / (e0 + jnp.exp(a1 - m))


def _hgrn_prompt_kernel(hq_ref, hf_ref, hi_ref, hg_ref, lb_ref, on_ref, sums_ref, lev_ref,
                        o_ref, s_ref, st_ref):
    c = pl.program_id(1)

    @pl.when(c == 0)
    def _():
        st_ref[...] = jnp.zeros(st_ref.shape, f32)

    lb = _lower_bound(lb_ref)
    f = lb + (1.0 - lb) * _sigmoid(hf_ref[...])
    q = _silu(hq_ref[...])
    k = 1.0 - f
    v = hi_ref[...].astype(bf16)
    l1, l2, l3 = _split3(jnp.log(f))
    sums = sums_ref[...]
    e_all = _dot(sums, l1) + _dot(sums, l2) + _dot(sums, l3)

    row = lax.broadcasted_iota(jnp.int32, (CHUNK, 1), 0)
    lev = lev_ref[...]
    a = jnp.where(lev == N_LEVELS, _dot_nt(q.astype(bf16), k.astype(bf16)), 0.0)
    for lvl in range(N_LEVELS):
        is_q = ((row >> lvl) & 1) == 1
        w = (jnp.where(is_q, q, k) * jnp.exp(e_all[lvl * CHUNK:(lvl + 1) * CHUNK])).astype(bf16)
        a = jnp.where(lev == lvl, _dot_nt(w, w), a)

    b = e_all[N_LEVELS * CHUNK:(N_LEVELS + 1) * CHUNK]
    b_suf = e_all[(N_LEVELS + 1) * CHUNK:]
    st = st_ref[...]
    o = _dot(a.astype(bf16), v) + _dot_nt((q * jnp.exp(b)).astype(bf16), st.astype(bf16))
    st_new = st * jnp.exp(b[CHUNK - 1:CHUNK, :]) + _dot_tn(v, (k * jnp.exp(b_suf)).astype(bf16))
    st_ref[...] = st_new

    o_ref[...] = (_rms(o, on_ref[...]) * _silu(hg_ref[...])).astype(bf16)

    @pl.when(c == pl.num_programs(1) - 1)
    def _():
        s_ref[0] = st_new.T


def _hgrn_prompt(proj, lb_raw, o_norm):
    sums, lev = _hgrn_constants()
    col = lambda part: (lambda h, c: (c, part * HGRN_HEADS + h))
    blk = lambda part: pl.BlockSpec((CHUNK, HGRN_K), col(part))
    return pl.pallas_call(
        _hgrn_prompt_kernel,
        grid=(HGRN_HEADS, N_CHUNKS),
        in_specs=[
            blk(0), blk(1), blk(2), blk(3),
            pl.BlockSpec((2, HGRN_K), lambda h, c: (0, h)),
            pl.BlockSpec((1, HGRN_V), lambda h, c: (0, 0)),
            pl.BlockSpec(sums.shape, lambda h, c: (0, 0)),
            pl.BlockSpec(lev.shape, lambda h, c: (0, 0)),
        ],
        out_specs=[
            pl.BlockSpec((CHUNK, HGRN_V), lambda h, c: (c, h)),
            pl.BlockSpec((1, HGRN_K, HGRN_V), lambda h, c: (h, 0, 0)),
        ],
        out_shape=[
            jax.ShapeDtypeStruct((P_ROWS, HGRN_WIDTH), bf16),
            jax.ShapeDtypeStruct((HGRN_HEADS, HGRN_K, HGRN_V), f32),
        ],
        scratch_shapes=[pltpu.VMEM((HGRN_V, HGRN_K), f32)],
        compiler_params=_params(("arbitrary", "arbitrary")),
        name="hgrn_prompt",
    )(proj, proj, proj, proj, lb_raw, o_norm, jnp.asarray(sums, bf16), jnp.asarray(lev))


def _hgrn_sample_kernel(hq_ref, hf_ref, hi_ref, hg_ref, lb_ref, on_ref, s0_ref, o_ref, s_ref):
    b = pl.program_id(0)
    row = pl.ds(b, 1)
    lb = _lower_bound(lb_ref)
    f = lb + (1.0 - lb) * _sigmoid(hf_ref[row, :])
    q = _silu(hq_ref[row, :])
    k = 1.0 - f
    v = hi_ref[row, :]
    gate = _silu(hg_ref[row, :])
    eye = (lax.broadcasted_iota(jnp.int32, (HGRN_K, HGRN_K), 0)
           == lax.broadcasted_iota(jnp.int32, (HGRN_K, HGRN_K), 1))

    def column(r):
        return jnp.sum(jnp.where(eye, r, 0.0), axis=1, keepdims=True)

    outs = []
    for h in range(HGRN_HEADS):
        sl = slice(h * HGRN_K, (h + 1) * HGRN_K)
        s_new = column(f[:, sl]) * s0_ref[0, h] + column(k[:, sl]) * v[:, sl]
        s_ref[0, h] = s_new
        o = jnp.sum(column(q[:, sl]) * s_new, axis=0, keepdims=True)
        outs.append(_rms(o, on_ref[...]) * gate[:, sl])
    o_ref[row, :] = jnp.concatenate(outs, axis=1)


def _hgrn_sample(proj, lb_raw, o_norm, state):
    row_blk = P_ROWS // DEC_BATCH
    blk = lambda part: pl.BlockSpec((DEC_BATCH, HGRN_WIDTH), lambda b: (row_blk, part))
    state_spec = pl.BlockSpec((1, HGRN_HEADS, HGRN_K, HGRN_V), lambda b: (b, 0, 0, 0))
    return pl.pallas_call(
        _hgrn_sample_kernel,
        grid=(DEC_BATCH,),
        in_specs=[
            blk(0), blk(1), blk(2), blk(3),
            pl.BlockSpec((2, HGRN_WIDTH), lambda b: (0, 0)),
            pl.BlockSpec((1, HGRN_V), lambda b: (0, 0)),
            state_spec,
        ],
        out_specs=[pl.BlockSpec((DEC_BATCH, HGRN_WIDTH), lambda b: (0, 0)), state_spec],
        out_shape=[
            jax.ShapeDtypeStruct((DEC_BATCH, HGRN_WIDTH), f32),
            jax.ShapeDtypeStruct(state.shape, f32),
        ],
        compiler_params=_params(("arbitrary",)),
        name="hgrn_sample",
    )(proj, proj, proj, proj, lb_raw, o_norm, state)


def _q_latent_kernel(q_ref, wuk_ref, o_ref):
    o_ref[0] = _dot_nt(q_ref[0][:, :QK_NOPE], wuk_ref[...]).astype(bf16)


def _q_latent(qh, wkv):
    row_blk = P_ROWS // DEC_BATCH
    return pl.pallas_call(
        _q_latent_kernel,
        grid=(MLA_HEADS,),
        in_specs=[
            pl.BlockSpec((1, DEC_BATCH, QK_DIM), lambda h: (h, row_blk, 0)),
            pl.BlockSpec((KV_RANK, QK_NOPE), lambda h: (0, 2 * h)),
        ],
        out_specs=pl.BlockSpec((1, DEC_BATCH, KV_RANK), lambda h: (h, 0, 0)),
        out_shape=jax.ShapeDtypeStruct((MLA_HEADS, DEC_BATCH, KV_RANK), bf16),
        compiler_params=_params(("arbitrary",)),
        name="q_latent",
    )(qh, wkv)


def _page_copies(pt_ref, ckv_hbm, kpe_hbm, ckv_buf, kpe_buf, sems, step, slot):
    seq = step // N_PAGE_STEPS
    first = (step % N_PAGE_STEPS) * PAGES_PER_STEP
    copies = []
    for p in range(PAGES_PER_STEP):
        page = pt_ref[seq, first + p]
        copies.append(pltpu.make_async_copy(ckv_hbm.at[0, page], ckv_buf.at[slot, p], sems.at[0, slot]))
        copies.append(pltpu.make_async_copy(kpe_hbm.at[0, page], kpe_buf.at[slot, p], sems.at[1, slot]))
    return copies


def _paged_attn_kernel(pt_ref, ql_ref, qp_ref, cn_ref, kn_ref, ckv_hbm, kpe_hbm, o_ref,
                       ckv_buf, kpe_buf, sems, m_ref, l_ref, acc_ref):
    b = pl.program_id(0)
    c = pl.program_id(1)
    step = b * N_PAGE_STEPS + c
    slot = step % 2
    n_steps = pl.num_programs(0) * N_PAGE_STEPS
    copies = functools.partial(_page_copies, pt_ref, ckv_hbm, kpe_hbm, ckv_buf, kpe_buf, sems)

    @pl.when(step == 0)
    def _():
        for cp in copies(step, slot):
            cp.start()

    @pl.when(step + 1 < n_steps)
    def _():
        for cp in copies(step + 1, 1 - slot):
            cp.start()

    ql = ql_ref[0]
    qp = qp_ref[0]

    @pl.when(c == 0)
    def _():
        s_new = (jnp.sum(ql.astype(f32) * cn_ref[0], axis=1, keepdims=True)
                 + jnp.sum(qp.astype(f32) * kn_ref[0], axis=1, keepdims=True)) * SOFTMAX_SCALE
        m_ref[...] = jnp.broadcast_to(s_new, m_ref.shape)
        l_ref[...] = jnp.ones(l_ref.shape, f32)
        acc_ref[...] = jnp.broadcast_to(cn_ref[0], acc_ref.shape)

    for cp in copies(step, slot):
        cp.wait()

    ck = ckv_buf[slot].reshape(KEYS_PER_STEP, KV_RANK).astype(bf16)
    kp = kpe_buf[slot].reshape(KEYS_PER_STEP, QK_ROPE).astype(bf16)
    s = (_dot_nt(ql, ck) + _dot_nt(qp, kp)) * SOFTMAX_SCALE
    m_prev = m_ref[...]
    m_next = jnp.maximum(m_prev, jnp.max(s, axis=1, keepdims=True))
    p = jnp.exp(s - jnp.concatenate([m_next] * (KEYS_PER_STEP // LANES), axis=1))
    alpha = jnp.exp(m_prev - m_next)
    l_ref[...] = alpha * l_ref[...] + jnp.sum(p, axis=1, keepdims=True)
    acc_ref[...] = (jnp.concatenate([alpha] * (KV_RANK // LANES), axis=1) * acc_ref[...]
                    + _dot(p.astype(bf16), ck))
    m_ref[...] = m_next

    @pl.when(c == N_PAGE_STEPS - 1)
    def _():
        o_ref[0] = acc_ref[...] / jnp.concatenate([l_ref[...]] * (KV_RANK // LANES), axis=1)


def _paged_attention(page_table, q_lat, q_pe, ckv_new, kpe_new, cache_ckv, cache_krope):
    seq_spec = lambda r, d: pl.BlockSpec((1, r, d), lambda b, c, pt: (b, 0, 0))
    grid_spec = pltpu.PrefetchScalarGridSpec(
        num_scalar_prefetch=1,
        grid=(DEC_BATCH, N_PAGE_STEPS),
        in_specs=[
            seq_spec(MLA_HEADS, KV_RANK), seq_spec(MLA_HEADS, QK_ROPE),
            seq_spec(1, KV_RANK), seq_spec(1, QK_ROPE),
            pl.BlockSpec(memory_space=pl.ANY), pl.BlockSpec(memory_space=pl.ANY),
        ],
        out_specs=seq_spec(MLA_HEADS, KV_RANK),
        scratch_shapes=[
            pltpu.VMEM((2, PAGES_PER_STEP, PAGE_SIZE, KV_RANK), f32),
            pltpu.VMEM((2, PAGES_PER_STEP, PAGE_SIZE, QK_ROPE), f32),
            pltpu.SemaphoreType.DMA((2, 2)),
            pltpu.VMEM((MLA_HEADS, LANES), f32), pltpu.VMEM((MLA_HEADS, LANES), f32),
            pltpu.VMEM((MLA_HEADS, KV_RANK), f32),
        ],
    )
    return pl.pallas_call(
        _paged_attn_kernel,
        grid_spec=grid_spec,
        out_shape=jax.ShapeDtypeStruct((DEC_BATCH, MLA_HEADS, KV_RANK), f32),
        compiler_params=_params(("arbitrary", "arbitrary")),
        name="paged_attn",
    )(page_table, q_lat, q_pe, ckv_new, kpe_new, cache_ckv, cache_krope)


def _v_up_kernel(o_ref, wuv_ref, out_ref):
    out_ref[...] = _dot(o_ref[0].astype(bf16), wuv_ref[...]).astype(bf16)


def _v_up(o_lat_h, wkv):
    return pl.pallas_call(
        _v_up_kernel,
        grid=(MLA_HEADS,),
        in_specs=[
            pl.BlockSpec((1, DEC_BATCH, KV_RANK), lambda h: (h, 0, 0)),
            pl.BlockSpec((KV_RANK, V_HEAD), lambda h: (0, 2 * h + 1)),
        ],
        out_specs=pl.BlockSpec((DEC_BATCH, V_HEAD), lambda h: (0, h)),
        out_shape=jax.ShapeDtypeStruct((DEC_BATCH, MLA_WIDTH), bf16),
        compiler_params=_params(("arbitrary",)),
        name="v_up",
    )(o_lat_h, wkv)


def _out_proj_kernel(oh_ref, oa_ref, x_ref, wh_ref, wa_ref, g_ref, o_ref):
    mix = _dot(oh_ref[...], wh_ref[...]) + _dot(oa_ref[...], wa_ref[...])
    o_ref[...] = x_ref[...] + _rms(mix, g_ref[...])


def _out_proj(o_h, o_a, x_all, w_out_b, g):
    return pl.pallas_call(
        _out_proj_kernel,
        grid=(R_ALL // TM_OUT,),
        in_specs=[
            pl.BlockSpec((TM_OUT, HGRN_WIDTH), lambda i: (i, 0)),
            pl.BlockSpec((TM_OUT, MLA_WIDTH), lambda i: (i, 0)),
            pl.BlockSpec((TM_OUT, D_MODEL), lambda i: (i, 0)),
            pl.BlockSpec((HGRN_WIDTH, D_MODEL), lambda i: (0, 0)),
            pl.BlockSpec((MLA_WIDTH, D_MODEL), lambda i: (1, 0)),
            pl.BlockSpec((1, D_MODEL), lambda i: (0, 0)),
        ],
        out_specs=pl.BlockSpec((TM_OUT, D_MODEL), lambda i: (i, 0)),
        out_shape=jax.ShapeDtypeStruct((R_ALL, D_MODEL), f32),
        compiler_params=_params(("arbitrary",)),
        name="out_proj",
    )(o_h, o_a, x_all, w_out_b, w_out_b, g)


def _ffn_kernel(x_ref, gpre_ref, gpost_ref, wg_ref, wv_ref, cw_ref, cb_ref, wd_ref, buf0_ref, buf1_ref,
                o_ref, tail_ref, h_ref, acc_ref, act_ref, carry_ref):
    i = pl.program_id(0)
    j = pl.program_id(1)
    col = pl.ds(pl.multiple_of(j * FF_TN, FF_TN), FF_TN)

    @pl.when(j == 0)
    def _():
        h_ref[...] = _rms(x_ref[...], gpre_ref[...]).astype(bf16)
        acc_ref[...] = jnp.zeros(acc_ref.shape, f32)

    @pl.when(i == 0)
    def _():
        carry_ref[:, col] = jnp.zeros((8, FF_TN), f32)

    h = h_ref[...]
    row = lax.broadcasted_iota(jnp.int32, (TM, 1), 0)
    g = jnp.where(row + i * TM >= PAD, _dot(h, wg_ref[...]), 0.0)
    val = _dot(h, wv_ref[...])
    prev = carry_ref[:, col]
    g1 = jnp.where(row == 0, prev[7:8], pltpu.roll(g, 1, 0))
    g2 = jnp.where(row == 0, prev[6:7], jnp.where(row == 1, prev[7:8], pltpu.roll(g, 2, 0)))
    w0, w1, w2 = cw_ref[0:1, :], cw_ref[1:2, :], cw_ref[2:3, :]
    y = cb_ref[...] + w0 * g2 + w1 * g1 + w2 * g
    act_ref[...] = (_silu(y) * val).astype(bf16)
    carry_ref[:, col] = g[TM - 8:, :]
    tail_ref[0] = g[TM - TAIL_ROWS:, :]

    @pl.when(i == pl.num_programs(0) - 1)
    def _():
        ys = cb_ref[...] + w0 * buf0_ref[...] + w1 * buf1_ref[...] + w2 * g[SAMPLE_LOCAL:, :]
        act_ref[SAMPLE_LOCAL:, :] = (_silu(ys) * val[SAMPLE_LOCAL:, :]).astype(bf16)

    acc_ref[...] += _dot(act_ref[...], wd_ref[...])

    @pl.when(j == pl.num_programs(1) - 1)
    def _():
        o_ref[...] = x_ref[...] + _rms(acc_ref[...], gpost_ref[...])


def _ffn(x1, g_pre, g_post, w_up_b, conv_w, conv_b, w_down_b, buf0, buf1):
    n_ff = D_FF // FF_TN
    return pl.pallas_call(
        _ffn_kernel,
        grid=(R_ALL // TM, n_ff),
        in_specs=[
            pl.BlockSpec((TM, D_MODEL), lambda i, j: (i, 0), pipeline_mode=pl.Buffered(1)),
            pl.BlockSpec((1, D_MODEL), lambda i, j: (0, 0)),
            pl.BlockSpec((1, D_MODEL), lambda i, j: (0, 0)),
            pl.BlockSpec((D_MODEL, FF_TN), lambda i, j: (0, j)),
            pl.BlockSpec((D_MODEL, FF_TN), lambda i, j: (0, j + n_ff)),
            pl.BlockSpec((3, FF_TN), lambda i, j: (0, j)),
            pl.BlockSpec((1, FF_TN), lambda i, j: (0, j)),
            pl.BlockSpec((FF_TN, D_MODEL), lambda i, j: (j, 0)),
            pl.BlockSpec((DEC_BATCH, FF_TN), lambda i, j: (0, j)),
            pl.BlockSpec((DEC_BATCH, FF_TN), lambda i, j: (0, j)),
        ],
        out_specs=[
            pl.BlockSpec((TM, D_MODEL), lambda i, j: (i, 0)),
            pl.BlockSpec((1, TAIL_ROWS, FF_TN), lambda i, j: (i, 0, j)),
        ],
        out_shape=[
            jax.ShapeDtypeStruct((R_ALL, D_MODEL), f32),
            jax.ShapeDtypeStruct((R_ALL // TM, TAIL_ROWS, D_FF), f32),
        ],
        scratch_shapes=[
            pltpu.VMEM((TM, D_MODEL), bf16),
            pltpu.VMEM((TM, D_MODEL), f32),
            pltpu.VMEM((TM, FF_TN), bf16),
            pltpu.VMEM((8, D_FF), f32),
        ],
        compiler_params=_params(("arbitrary", "arbitrary")),
        name="conv_ffn",
    )(x1, g_pre, g_post, w_up_b, w_up_b, conv_w, conv_b, w_down_b, buf0, buf1)


def kernel(x_prompt, x_sample, cache_ckv, cache_krope, state_hgrn, state_conv, page_table, meta_tokens,
           lb_raw, g_mix_pre, g_mix_post, g_ffn_pre, g_ffn_post, w_in, hgrn_o_norm, q_a_norm, kv_a_norm,
           w_q_up, w_kv_up, w_out, w_ffn_up, conv_w, conv_b, w_ffn_down):
    x_all = jnp.concatenate([jnp.zeros((PAD, D_MODEL), f32), meta_tokens.astype(f32), x_prompt[0],
                             x_sample[:, 0]], axis=0)

    w_in_p = jnp.pad(w_in[0], ((0, 0), (0, IN_PAD - IN_TOTAL))).astype(bf16)
    wq = w_q_up[0].reshape(Q_RANK, MLA_HEADS, QK_DIM)
    wq_p = jnp.concatenate([wq[:, :, :QK_NOPE].reshape(Q_RANK, -1), wq[:, :, QK_NOPE:].reshape(Q_RANK, -1)],
                           axis=1).astype(bf16)
    wkv = w_kv_up[0].astype(bf16)
    w_out_b = w_out[0].astype(bf16)
    w_up_b = w_ffn_up[0].astype(bf16)
    w_down_b = w_ffn_down[0].astype(bf16)

    proj = _in_proj(x_all, g_mix_pre, w_in_p)
    qh, kh, vh, ckv, kpe = _mla_proj(proj, q_a_norm, kv_a_norm, wq_p, wkv)

    o_a_p = _attention(qh, kh, vh)
    o_h_p, s_p = _hgrn_prompt(proj, lb_raw, hgrn_o_norm)

    o_h_s, s_s = _hgrn_sample(proj, lb_raw, hgrn_o_norm, state_hgrn[0])
    q_lat = _q_latent(qh, wkv).transpose(1, 0, 2)
    q_pe_s = qh[:, P_ROWS:, QK_NOPE:].transpose(1, 0, 2)
    ckv_s = ckv[P_ROWS:]
    kpe_s = kpe[P_ROWS:]
    o_lat = _paged_attention(page_table, q_lat, q_pe_s, ckv_s[:, None, :], kpe_s[:, None, :],
                             cache_ckv, cache_krope)
    o_a_s = _v_up(o_lat.transpose(1, 0, 2), wkv)

    o_h = jnp.concatenate([o_h_p, o_h_s.astype(bf16)], axis=0)
    o_a = jnp.concatenate([o_a_p, o_a_s], axis=0)
    x1 = _out_proj(o_h, o_a, x_all, w_out_b, g_mix_post)
    x2, tails = _ffn(x1, g_ffn_pre, g_ffn_post, w_up_b, conv_w[0], conv_b, w_down_b,
                     state_conv[0, :, 0], state_conv[0, :, 1])
    tail = tails[-1]

    n_tail = TAIL_ROWS - DEC_BATCH
    conv_p = tail[n_tail - 2:n_tail][None, None]
    conv_s = jnp.stack([state_conv[0, :, 1], tail[n_tail:]], axis=1)[None]
    return (x2[P_START:P_ROWS][None], x2[P_ROWS:][:, None],
            ckv[PAD:P_ROWS][None, None], kpe[PAD:P_ROWS][None, None],
            ckv_s[None, :, None], kpe_s[None, :, None],
            s_p[None, None], s_s[None], conv_p, conv_s)
```

```python
import functools

import numpy as np
import jax
import jax.numpy as jnp
from jax import lax
from jax.experimental import pallas as pl
from jax.experimental.pallas import tpu as pltpu

D_MODEL = 2048
SEQ = 8192
N_META = 16
DEC_BATCH = 128
PAST_LEN = 16384
PAGE_SIZE = 128
N_PAGES = PAST_LEN // PAGE_SIZE
HGRN_HEADS = 8
HGRN_K = 128
HGRN_V = 128
HGRN_WIDTH = HGRN_HEADS * HGRN_V
MLA_HEADS = 8
QK_NOPE = 128
QK_ROPE = 64
QK_DIM = QK_NOPE + QK_ROPE
V_HEAD = 128
Q_RANK = 512
KV_RANK = 256
MLA_WIDTH = MLA_HEADS * V_HEAD
ROPE_THETA = 10000.0
SOFTMAX_SCALE = QK_DIM ** -0.5
LOG2_E = 1.4426950408889634
D_FF = 5632
EPS = 1e-6
IN_TOTAL = 4 * HGRN_WIDTH + Q_RANK + KV_RANK + QK_ROPE

CHUNK = 128
PAD = CHUNK - N_META
P_START = PAD + N_META
P_ROWS = P_START + SEQ
R_ALL = P_ROWS + DEC_BATCH
N_CHUNKS = P_ROWS // CHUNK

LANES = 128
IN_PAD = 5120
IN_TN = 1024
TM = 768
TM_OUT = 384
TQ = 640
TQ_SUB = 320
FF_TN = 512
SAMPLE_LOCAL = P_ROWS - (R_ALL // TM - 1) * TM
TAIL_ROWS = 136
PAGES_PER_STEP = 16
N_PAGE_STEPS = N_PAGES // PAGES_PER_STEP
KEYS_PER_STEP = PAGES_PER_STEP * PAGE_SIZE
NEG_BIG = -1e30
VMEM_LIMIT = 56 * 1024 * 1024

NT_DIMS = (((1,), (1,)), ((), ()))
TN_DIMS = (((0,), (0,)), ((), ()))

f32 = jnp.float32
bf16 = jnp.bfloat16


def _params(sem, vmem=VMEM_LIMIT):
    return pltpu.CompilerParams(dimension_semantics=sem, vmem_limit_bytes=vmem)


def _rms(x, g):
    return x * lax.rsqrt(jnp.mean(x * x, axis=-1, keepdims=True) + EPS) * g


def _sigmoid(x):
    return 1.0 / (1.0 + jnp.exp(-x))


def _silu(x):
    return x * _sigmoid(x)


def _dot(a, b):
    return jnp.dot(a, b, preferred_element_type=f32)


def _dot_nt(a, b):
    return lax.dot_general(a, b, NT_DIMS, preferred_element_type=f32)


def _dot_tn(a, b):
    return lax.dot_general(a, b, TN_DIMS, preferred_element_type=f32)


def _in_proj_kernel(x_ref, g_ref, w_ref, o_ref, h_ref):
    @pl.when(pl.program_id(1) == 0)
    def _():
        h_ref[...] = _rms(x_ref[...], g_ref[...]).astype(bf16)

    o_ref[...] = _dot(h_ref[...], w_ref[...])


def _in_proj(x_all, g, w_in_p):
    return pl.pallas_call(
        _in_proj_kernel,
        grid=(R_ALL // TM, IN_PAD // IN_TN),
        in_specs=[
            pl.BlockSpec((TM, D_MODEL), lambda i, j: (i, 0)),
            pl.BlockSpec((1, D_MODEL), lambda i, j: (0, 0)),
            pl.BlockSpec((D_MODEL, IN_TN), lambda i, j: (0, j)),
        ],
        out_specs=pl.BlockSpec((TM, IN_TN), lambda i, j: (i, j)),
        out_shape=jax.ShapeDtypeStruct((R_ALL, IN_PAD), f32),
        scratch_shapes=[pltpu.VMEM((TM, D_MODEL), bf16)],
        compiler_params=_params(("arbitrary", "arbitrary")),
        name="in_proj",
    )(x_all, g, w_in_p)


def _rope_tables(row0, n_rows):
    lane = lax.broadcasted_iota(jnp.int32, (1, LANES), 1)
    half = QK_ROPE // 2
    fidx = (lane & (half - 1)).astype(f32)
    inv = jnp.power(jnp.float32(ROPE_THETA), -fidx / half)
    rows = row0 + lax.broadcasted_iota(jnp.int32, (n_rows, 1), 0)
    pos = jnp.where(rows < P_ROWS, rows - PAD, PAST_LEN).astype(f32)
    ang = pos * inv
    sin = jnp.sin(ang)
    return jnp.cos(ang), jnp.where((lane & half) == 0, -sin, sin)


def _rotate_half(x, cos, sin_signed):
    n = x.shape[1]
    half = QK_ROPE // 2
    first_half = (lax.broadcasted_iota(jnp.int32, (1, n), 1) & half) == 0
    partner = jnp.where(first_half, pltpu.roll(x, n - half, 1), pltpu.roll(x, half, 1))
    return x * cos + partner * sin_signed


def _mla_proj_kernel(qd_ref, kvd_ref, kr_ref, qn_ref, kvn_ref, wq_ref, wkv_ref,
                     qh_ref, kh_ref, vh_ref, ckv_ref, kpe_ref):
    tm = qd_ref.shape[0]
    cos, sin_s = _rope_tables(pl.program_id(0) * tm, tm)

    q = _dot(_rms(qd_ref[...], qn_ref[...]).astype(bf16), wq_ref[...])
    reps = MLA_HEADS * QK_ROPE // LANES
    q_pe = _rotate_half(q[:, MLA_HEADS * QK_NOPE:],
                        jnp.concatenate([cos] * reps, axis=1),
                        jnp.concatenate([sin_s] * reps, axis=1))

    k_pe = _rotate_half(kr_ref[...], cos, sin_s)[:, :QK_ROPE]
    kpe_ref[...] = k_pe

    c_kv = _rms(kvd_ref[...], kvn_ref[...])
    ckv_ref[...] = c_kv
    kv = _dot(c_kv.astype(bf16), wkv_ref[...])

    k_pe_b = k_pe.astype(bf16)
    for h in range(MLA_HEADS):
        qh_ref[h, :, :QK_NOPE] = q[:, h * QK_NOPE:(h + 1) * QK_NOPE].astype(bf16)
        qh_ref[h, :, QK_NOPE:] = q_pe[:, h * QK_ROPE:(h + 1) * QK_ROPE].astype(bf16)
        base = h * (QK_NOPE + V_HEAD)
        kh_ref[h, :, :QK_NOPE] = kv[:, base:base + QK_NOPE].astype(bf16)
        kh_ref[h, :, QK_NOPE:] = k_pe_b
        vh_ref[h] = kv[:, base + QK_NOPE:base + QK_NOPE + V_HEAD].astype(bf16)


def _mla_proj(proj, q_a_norm, kv_a_norm, wq_p, wkv):
    col_q = (4 * HGRN_WIDTH) // Q_RANK
    col_kv = (4 * HGRN_WIDTH + Q_RANK) // KV_RANK
    col_kr = (4 * HGRN_WIDTH + Q_RANK + KV_RANK) // LANES
    head_spec = lambda d: pl.BlockSpec((MLA_HEADS, TM, d), lambda i: (0, i, 0))
    return pl.pallas_call(
        _mla_proj_kernel,
        grid=(R_ALL // TM,),
        in_specs=[
            pl.BlockSpec((TM, Q_RANK), lambda i: (i, col_q)),
            pl.BlockSpec((TM, KV_RANK), lambda i: (i, col_kv)),
            pl.BlockSpec((TM, LANES), lambda i: (i, col_kr)),
            pl.BlockSpec((1, Q_RANK), lambda i: (0, 0)),
            pl.BlockSpec((1, KV_RANK), lambda i: (0, 0)),
            pl.BlockSpec(wq_p.shape, lambda i: (0, 0)),
            pl.BlockSpec(wkv.shape, lambda i: (0, 0)),
        ],
        out_specs=[
            head_spec(QK_DIM), head_spec(QK_DIM), head_spec(V_HEAD),
            pl.BlockSpec((TM, KV_RANK), lambda i: (i, 0)),
            pl.BlockSpec((TM, QK_ROPE), lambda i: (i, 0)),
        ],
        out_shape=[
            jax.ShapeDtypeStruct((MLA_HEADS, R_ALL, QK_DIM), bf16),
            jax.ShapeDtypeStruct((MLA_HEADS, R_ALL, QK_DIM), bf16),
            jax.ShapeDtypeStruct((MLA_HEADS, R_ALL, V_HEAD), bf16),
            jax.ShapeDtypeStruct((R_ALL, KV_RANK), f32),
            jax.ShapeDtypeStruct((R_ALL, QK_ROPE), f32),
        ],
        compiler_params=_params(("arbitrary",)),
        name="mla_proj",
    )(proj, proj, proj, q_a_norm, kv_a_norm, wq_p, wkv)


def _attn_kernel(q_ref, k_ref, v_ref, o_ref, m_ref, l_ref, acc_ref):
    i = pl.program_id(1)
    q = q_ref[0]
    m_ref[...] = jnp.full(m_ref.shape, NEG_BIG, f32)
    l_ref[...] = jnp.zeros(l_ref.shape, f32)
    acc_ref[...] = jnp.zeros(acc_ref.shape, f32)
    reps = TQ // LANES

    def block(j, masked):
        start = pl.multiple_of(j * TQ, TQ)
        k = k_ref[0, pl.ds(start, TQ), :]
        v = v_ref[0, pl.ds(start, TQ), :]
        groups = range(0, TQ, TQ_SUB)
        logits = [_dot_nt(q[r0:r0 + TQ_SUB], k) for r0 in groups]
        for r0, t in zip(groups, logits):
            rows = slice(r0, r0 + TQ_SUB)
            t = t * (SOFTMAX_SCALE * LOG2_E)
            if masked:
                q_row = i * TQ + r0 + lax.broadcasted_iota(jnp.int32, (TQ_SUB, TQ), 0)
                k_row = j * TQ + lax.broadcasted_iota(jnp.int32, (TQ_SUB, TQ), 1)
                t = jnp.where(k_row <= q_row, jnp.where(k_row >= PAD, t, NEG_BIG), NEG_BIG)
            m_prev = m_ref[rows]
            m_next = jnp.maximum(m_prev, jnp.max(t, axis=1, keepdims=True))
            p = jnp.exp2(t - jnp.concatenate([m_next] * reps, axis=1))
            alpha = jnp.exp2(m_prev - m_next)
            l_ref[rows] = alpha * l_ref[rows] + jnp.sum(p, axis=1, keepdims=True)
            acc_ref[rows] = alpha * acc_ref[rows] + _dot(p.astype(bf16), v)
            m_ref[rows] = m_next

    block(0, True)

    def body(j, carry):
        block(j, False)
        return carry

    lax.fori_loop(1, i, body, 0)

    @pl.when(i > 0)
    def _():
        block(i, True)

    o_ref[...] = (acc_ref[...] / l_ref[...]).astype(bf16)


def _attention(qh, kh, vh):
    return pl.pallas_call(
        _attn_kernel,
        grid=(MLA_HEADS, P_ROWS // TQ),
        in_specs=[
            pl.BlockSpec((1, TQ, QK_DIM), lambda h, i: (h, i, 0)),
            pl.BlockSpec((1, R_ALL, QK_DIM), lambda h, i: (h, 0, 0)),
            pl.BlockSpec((1, R_ALL, V_HEAD), lambda h, i: (h, 0, 0)),
        ],
        out_specs=pl.BlockSpec((TQ, V_HEAD), lambda h, i: (i, h)),
        out_shape=jax.ShapeDtypeStruct((P_ROWS, MLA_WIDTH), bf16),
        scratch_shapes=[pltpu.VMEM((TQ, LANES), f32), pltpu.VMEM((TQ, LANES), f32),
                        pltpu.VMEM((TQ, V_HEAD), f32)],
        compiler_params=_params(("arbitrary", "arbitrary")),
        name="prompt_attn",
    )(qh, kh, vh)


N_LEVELS = 7


def _hgrn_constants():
    t = np.arange(CHUNK)
    u = np.arange(CHUNK)
    blocks = []
    for lvl in range(N_LEVELS):
        m = 1 << lvl
        blk = t // m
        is_q = (blk % 2) == 1
        start = blk * m
        end = (blk + 1) * m
        sel_q = (u[None, :] >= start[:, None]) & (u[None, :] <= t[:, None])
        sel_k = (u[None, :] > t[:, None]) & (u[None, :] < end[:, None])
        blocks.append(np.where(is_q[:, None], sel_q, sel_k))
    blocks.append(u[None, :] <= t[:, None])
    blocks.append(u[None, :] > t[:, None])
    sums = np.concatenate(blocks, axis=0).astype(np.float32)
    x = t[:, None] ^ t[None, :]
    lev = np.where(x > 0, np.floor(np.log2(np.maximum(x, 1))).astype(np.int32), N_LEVELS)
    lev = np.where(t[None, :] > t[:, None], N_LEVELS + 1, lev).astype(np.int32)
    return sums, lev


def _split3(x):
    a = x.astype(bf16)
    r = x - a.astype(f32)
    b = r.astype(bf16)
    c = (r - b.astype(f32)).astype(bf16)
    return a, b, c


def _lower_bound(lb_raw_ref):
    a0 = lb_raw_ref[0:1, :]
    a1 = lb_raw_ref[1:2, :]
    m = jnp.maximum(a0, a1)
    e0 = jnp.exp(a0 - m)
    return e0 / (e0 + jnp.exp(a1 - m))


def _hgrn_prompt_kernel(hq_ref, hf_ref, hi_ref, hg_ref, lb_ref, on_ref, sums_ref, lev_ref,
                        o_ref, s_ref, st_ref):
    c = pl.program_id(0)

    @pl.when(c == 0)
    def _():
        st_ref[...] = jnp.zeros(st_ref.shape, f32)

    lb = _lower_bound(lb_ref)
    f_all = lb + (1.0 - lb) * _sigmoid(hf_ref[...])
    l1, l2, l3 = _split3(jnp.log(f_all))
    sums = sums_ref[...]
    e_all = _dot(sums, l1) + _dot(sums, l2) + _dot(sums, l3)

    row = lax.broadcasted_iota(jnp.int32, (CHUNK, 1), 0)
    lev = lev_ref[...]
    for h in range(HGRN_HEADS):
        sl = slice(h * HGRN_K, (h + 1) * HGRN_K)
        q = _silu(hq_ref[:, sl])
        k = 1.0 - f_all[:, sl]
        v = hi_ref[:, sl].astype(bf16)
        a = jnp.where(lev == N_LEVELS, _dot_nt(q.astype(bf16), k.astype(bf16)), 0.0)
        for lvl in range(N_LEVELS):
            is_q = ((row >> lvl) & 1) == 1
            w = (jnp.where(is_q, q, k) * jnp.exp(e_all[lvl * CHUNK:(lvl + 1) * CHUNK, sl])).astype(bf16)
            a = jnp.where(lev == lvl, _dot_nt(w, w), a)

        b = e_all[N_LEVELS * CHUNK:(N_LEVELS + 1) * CHUNK, sl]
        b_suf = e_all[(N_LEVELS + 1) * CHUNK:, sl]
        st = st_ref[h]
        o = _dot(a.astype(bf16), v) + _dot_nt((q * jnp.exp(b)).astype(bf16), st.astype(bf16))
        st_new = st * jnp.exp(b[CHUNK - 1:CHUNK, :]) + _dot_tn(v, (k * jnp.exp(b_suf)).astype(bf16))
        st_ref[h] = st_new
        o_ref[:, sl] = (_rms(o, on_ref[...]) * _silu(hg_ref[:, sl])).astype(bf16)

    @pl.when(c == pl.num_programs(0) - 1)
    def _():
        for h in range(HGRN_HEADS):
            s_ref[h] = st_ref[h].T


def _hgrn_prompt(proj, lb_raw, o_norm):
    sums, lev = _hgrn_constants()
    blk = lambda part: pl.BlockSpec((CHUNK, HGRN_WIDTH), lambda c: (c, part))
    return pl.pallas_call(
        _hgrn_prompt_kernel,
        grid=(N_CHUNKS,),
        in_specs=[
            blk(0), blk(1), blk(2), blk(3),
            pl.BlockSpec((2, HGRN_WIDTH), lambda c: (0, 0)),
            pl.BlockSpec((1, HGRN_V), lambda c: (0, 0)),
            pl.BlockSpec(sums.shape, lambda c: (0, 0)),
            pl.BlockSpec(lev.shape, lambda c: (0, 0)),
        ],
        out_specs=[
            pl.BlockSpec((CHUNK, HGRN_WIDTH), lambda c: (c, 0)),
            pl.BlockSpec((HGRN_HEADS, HGRN_K, HGRN_V), lambda c: (0, 0, 0)),
        ],
        out_shape=[
            jax.ShapeDtypeStruct((P_ROWS, HGRN_WIDTH), bf16),
            jax.ShapeDtypeStruct((HGRN_HEADS, HGRN_K, HGRN_V), f32),
        ],
        scratch_shapes=[pltpu.VMEM((HGRN_HEADS, HGRN_V, HGRN_K), f32)],
        compiler_params=_params(("arbitrary",)),
        name="hgrn_prompt",
    )(proj, proj, proj, proj, lb_raw, o_norm, jnp.asarray(sums, bf16), jnp.asarray(lev))


def _hgrn_sample_kernel(hq_ref, hf_ref, hi_ref, hg_ref, lb_ref, on_ref, s0_ref, o_ref, s_ref):
    b = pl.program_id(0)
    row = pl.ds(b, 1)
    lb = _lower_bound(lb_ref)
    f = lb + (1.0 - lb) * _sigmoid(hf_ref[row, :])
    q = _silu(hq_ref[row, :])
    k = 1.0 - f
    v = hi_ref[row, :]
    gate = _silu(hg_ref[row, :])
    eye = (lax.broadcasted_iota(jnp.int32, (HGRN_K, HGRN_K), 0)
           == lax.broadcasted_iota(jnp.int32, (HGRN_K, HGRN_K), 1))

    def column(r):
        return jnp.sum(jnp.where(eye, r, 0.0), axis=1, keepdims=True)

    outs = []
    for h in range(HGRN_HEADS):
        sl = slice(h * HGRN_K, (h + 1) * HGRN_K)
        s_new = column(f[:, sl]) * s0_ref[0, h] + column(k[:, sl]) * v[:, sl]
        s_ref[0, h] = s_new
        o = jnp.sum(column(q[:, sl]) * s_new, axis=0, keepdims=True)
        outs.append(_rms(o, on_ref[...]) * gate[:, sl])
    o_ref[row, :] = jnp.concatenate(outs, axis=1)


def _hgrn_sample(proj, lb_raw, o_norm, state):
    row_blk = P_ROWS // DEC_BATCH
    blk = lambda part: pl.BlockSpec((DEC_BATCH, HGRN_WIDTH), lambda b: (row_blk, part))
    state_spec = pl.BlockSpec((1, HGRN_HEADS, HGRN_K, HGRN_V), lambda b: (b, 0, 0, 0))
    return pl.pallas_call(
        _hgrn_sample_kernel,
        grid=(DEC_BATCH,),
        in_specs=[
            blk(0), blk(1), blk(2), blk(3),
            pl.BlockSpec((2, HGRN_WIDTH), lambda b: (0, 0)),
            pl.BlockSpec((1, HGRN_V), lambda b: (0, 0)),
            state_spec,
        ],
        out_specs=[pl.BlockSpec((DEC_BATCH, HGRN_WIDTH), lambda b: (0, 0)), state_spec],
        out_shape=[
            jax.ShapeDtypeStruct((DEC_BATCH, HGRN_WIDTH), f32),
            jax.ShapeDtypeStruct(state.shape, f32),
        ],
        compiler_params=_params(("arbitrary",)),
        name="hgrn_sample",
    )(proj, proj, proj, proj, lb_raw, o_norm, state)


def _q_latent_kernel(q_ref, wuk_ref, o_ref):
    o_ref[0] = _dot_nt(q_ref[0][:, :QK_NOPE], wuk_ref[...]).astype(bf16)


def _q_latent(qh, wkv):
    row_blk = P_ROWS // DEC_BATCH
    return pl.pallas_call(
        _q_latent_kernel,
        grid=(MLA_HEADS,),
        in_specs=[
            pl.BlockSpec((1, DEC_BATCH, QK_DIM), lambda h: (h, row_blk, 0)),
            pl.BlockSpec((KV_RANK, QK_NOPE), lambda h: (0, 2 * h)),
        ],
        out_specs=pl.BlockSpec((1, DEC_BATCH, KV_RANK), lambda h: (h, 0, 0)),
        out_shape=jax.ShapeDtypeStruct((MLA_HEADS, DEC_BATCH, KV_RANK), bf16),
        compiler_params=_params(("arbitrary",)),
        name="q_latent",
    )(qh, wkv)


def _page_copies(pt_ref, ckv_hbm, kpe_hbm, ckv_buf, kpe_buf, sems, seq, c):
    copies = []
    for p in range(PAGES_PER_STEP):
        page = pt_ref[seq, c * PAGES_PER_STEP + p]
        copies.append(pltpu.make_async_copy(ckv_hbm.at[0, page], ckv_buf.at[c, p], sems.at[0, c]))
        copies.append(pltpu.make_async_copy(kpe_hbm.at[0, page], kpe_buf.at[c, p], sems.at[1, c]))
    return copies


def _paged_attn_kernel(pt_ref, ql_ref, qp_ref, cn_ref, kn_ref, ckv_hbm, kpe_hbm, o_ref,
                       ckv_buf, kpe_buf, sems, ckb_ref, kpb_ref):
    b = pl.program_id(0)
    copies = functools.partial(_page_copies, pt_ref, ckv_hbm, kpe_hbm, ckv_buf, kpe_buf, sems)
    to_log2 = SOFTMAX_SCALE * LOG2_E

    @pl.when(b == 0)
    def _():
        for c in range(N_PAGE_STEPS):
            for cp in copies(b, c):
                cp.start()

    ql = ql_ref[0]
    qp = qp_ref[0]

    def scores(c):
        for cp in copies(b, c):
            cp.wait()
        half = c % 2
        ckb_ref[half] = ckv_buf[c].reshape(KEYS_PER_STEP, KV_RANK).astype(bf16)
        kpb_ref[half] = jnp.concatenate([kpe_buf[c, p] for p in range(PAGES_PER_STEP)], axis=1).astype(bf16)
        return (_dot_nt(ql, ckb_ref[half]) + _dot(qp, kpb_ref[half])) * to_log2

    t_new = (jnp.sum(ql.astype(f32) * cn_ref[0], axis=1, keepdims=True)
             + jnp.sum(qp.astype(f32) * kn_ref[0], axis=1, keepdims=True)) * to_log2
    m = jnp.broadcast_to(t_new, (MLA_HEADS, LANES))
    l = jnp.ones((MLA_HEADS, LANES), f32)
    acc = jnp.broadcast_to(cn_ref[0], (MLA_HEADS, KV_RANK))

    t = scores(0)
    for c in range(N_PAGE_STEPS):
        t_next = scores(c + 1) if c + 1 < N_PAGE_STEPS else None
        m_next = jnp.maximum(m, jnp.max(t, axis=1, keepdims=True))
        p = jnp.exp2(t - jnp.concatenate([m_next] * (KEYS_PER_STEP // LANES), axis=1))
        alpha = jnp.exp2(m - m_next)
        l = alpha * l + jnp.sum(p, axis=1, keepdims=True)
        acc = jnp.concatenate([alpha] * (KV_RANK // LANES), axis=1) * acc + _dot(p.astype(bf16), ckb_ref[c % 2])
        m = m_next

        @pl.when(b + 1 < pl.num_programs(0))
        def _():
            for cp in copies(b + 1, c):
                cp.start()

        t = t_next

    o_ref[0] = acc / jnp.concatenate([l] * (KV_RANK // LANES), axis=1)


def _paged_attention(page_table, q_lat, q_pe, ckv_new, kpe_new, cache_ckv, cache_krope_t):
    seq_spec = lambda r, d: pl.BlockSpec((1, r, d), lambda b, pt: (b, 0, 0))
    grid_spec = pltpu.PrefetchScalarGridSpec(
        num_scalar_prefetch=1,
        grid=(DEC_BATCH,),
        in_specs=[
            seq_spec(MLA_HEADS, KV_RANK), seq_spec(MLA_HEADS, QK_ROPE),
            seq_spec(1, KV_RANK), seq_spec(1, QK_ROPE),
            pl.BlockSpec(memory_space=pl.ANY), pl.BlockSpec(memory_space=pl.ANY),
        ],
        out_specs=seq_spec(MLA_HEADS, KV_RANK),
        scratch_shapes=[
            pltpu.VMEM((N_PAGE_STEPS, PAGES_PER_STEP, PAGE_SIZE, KV_RANK), f32),
            pltpu.VMEM((N_PAGE_STEPS, PAGES_PER_STEP, QK_ROPE, PAGE_SIZE), f32),
            pltpu.SemaphoreType.DMA((2, N_PAGE_STEPS)),
            pltpu.VMEM((2, KEYS_PER_STEP, KV_RANK), bf16),
            pltpu.VMEM((2, QK_ROPE, KEYS_PER_STEP), bf16),
        ],
    )
    return pl.pallas_call(
        _paged_attn_kernel,
        grid_spec=grid_spec,
        out_shape=jax.ShapeDtypeStruct((DEC_BATCH, MLA_HEADS, KV_RANK), f32),
        compiler_params=_params(("arbitrary",)),
        name="paged_attn",
    )(page_table, q_lat, q_pe, ckv_new, kpe_new, cache_ckv, cache_krope_t)


def _v_up_kernel(o_ref, wuv_ref, out_ref):
    out_ref[...] = _dot(o_ref[0].astype(bf16), wuv_ref[...]).astype(bf16)


def _v_up(o_lat_h, wkv):
    return pl.pallas_call(
        _v_up_kernel,
        grid=(MLA_HEADS,),
        in_specs=[
            pl.BlockSpec((1, DEC_BATCH, KV_RANK), lambda h: (h, 0, 0)),
            pl.BlockSpec((KV_RANK, V_HEAD), lambda h: (0, 2 * h + 1)),
        ],
        out_specs=pl.BlockSpec((DEC_BATCH, V_HEAD), lambda h: (0, h)),
        out_shape=jax.ShapeDtypeStruct((DEC_BATCH, MLA_WIDTH), bf16),
        compiler_params=_params(("arbitrary",)),
        name="v_up",
    )(o_lat_h, wkv)


def _out_proj_kernel(oh_ref, oa_ref, x_ref, wh_ref, wa_ref, g_ref, o_ref):
    mix = _dot(oh_ref[...], wh_ref[...]) + _dot(oa_ref[...], wa_ref[...])
    o_ref[...] = x_ref[...] + _rms(mix, g_ref[...])


def _out_proj(o_h, o_a, x_all, w_out_b, g):
    return pl.pallas_call(
        _out_proj_kernel,
        grid=(R_ALL // TM_OUT,),
        in_specs=[
            pl.BlockSpec((TM_OUT, HGRN_WIDTH), lambda i: (i, 0)),
            pl.BlockSpec((TM_OUT, MLA_WIDTH), lambda i: (i, 0)),
            pl.BlockSpec((TM_OUT, D_MODEL), lambda i: (i, 0)),
            pl.BlockSpec((HGRN_WIDTH, D_MODEL), lambda i: (0, 0)),
            pl.BlockSpec((MLA_WIDTH, D_MODEL), lambda i: (1, 0)),
            pl.BlockSpec((1, D_MODEL), lambda i: (0, 0)),
        ],
        out_specs=pl.BlockSpec((TM_OUT, D_MODEL), lambda i: (i, 0)),
        out_shape=jax.ShapeDtypeStruct((R_ALL, D_MODEL), f32),
        compiler_params=_params(("arbitrary",)),
        name="out_proj",
    )(o_h, o_a, x_all, w_out_b, w_out_b, g)


def _ffn_kernel(x_ref, gpre_ref, gpost_ref, wg_ref, wv_ref, cw_ref, cb_ref, wd_ref, buf0_ref, buf1_ref,
                o_ref, tail_ref, h_ref, acc_ref, act_ref, carry_ref):
    i = pl.program_id(0)
    j = pl.program_id(1)
    col = pl.ds(pl.multiple_of(j * FF_TN, FF_TN), FF_TN)

    @pl.when(j == 0)
    def _():
        h_ref[...] = _rms(x_ref[...], gpre_ref[...]).astype(bf16)
        acc_ref[...] = jnp.zeros(acc_ref.shape, f32)

    @pl.when(i == 0)
    def _():
        carry_ref[:, col] = jnp.zeros((8, FF_TN), f32)

    h = h_ref[...]
    row = lax.broadcasted_iota(jnp.int32, (TM, 1), 0)
    g = jnp.where(row + i * TM >= PAD, _dot(h, wg_ref[...]), 0.0)
    val = _dot(h, wv_ref[...])
    prev = carry_ref[:, col]
    g1 = jnp.where(row == 0, prev[7:8], pltpu.roll(g, 1, 0))
    g2 = jnp.where(row == 0, prev[6:7], jnp.where(row == 1, prev[7:8], pltpu.roll(g, 2, 0)))
    w0, w1, w2 = cw_ref[0:1, :], cw_ref[1:2, :], cw_ref[2:3, :]
    y = cb_ref[...] + w0 * g2 + w1 * g1 + w2 * g
    act_ref[...] = (_silu(y) * val).astype(bf16)
    carry_ref[:, col] = g[TM - 8:, :]
    tail_ref[0] = g[TM - TAIL_ROWS:, :]

    @pl.when(i == pl.num_programs(0) - 1)
    def _():
        ys = cb_ref[...] + w0 * buf0_ref[...] + w1 * buf1_ref[...] + w2 * g[SAMPLE_LOCAL:, :]
        act_ref[SAMPLE_LOCAL:, :] = (_silu(ys) * val[SAMPLE_LOCAL:, :]).astype(bf16)

    acc_ref[...] += _dot(act_ref[...], wd_ref[...])

    @pl.when(j == pl.num_programs(1) - 1)
    def _():
        o_ref[...] = x_ref[...] + _rms(acc_ref[...], gpost_ref[...])


def _ffn(x1, g_pre, g_post, w_up_b, conv_w, conv_b, w_down_b, buf0, buf1):
    n_ff = D_FF // FF_TN
    return pl.pallas_call(
        _ffn_kernel,
        grid=(R_ALL // TM, n_ff),
        in_specs=[
            pl.BlockSpec((TM, D_MODEL), lambda i, j: (i, 0), pipeline_mode=pl.Buffered(1)),
            pl.BlockSpec((1, D_MODEL), lambda i, j: (0, 0)),
            pl.BlockSpec((1, D_MODEL), lambda i, j: (0, 0)),
            pl.BlockSpec((D_MODEL, FF_TN), lambda i, j: (0, j)),
            pl.BlockSpec((D_MODEL, FF_TN), lambda i, j: (0, j + n_ff)),
            pl.BlockSpec((3, FF_TN), lambda i, j: (0, j)),
            pl.BlockSpec((1, FF_TN), lambda i, j: (0, j)),
            pl.BlockSpec((FF_TN, D_MODEL), lambda i, j: (j, 0)),
            pl.BlockSpec((DEC_BATCH, FF_TN), lambda i, j: (0, j)),
            pl.BlockSpec((DEC_BATCH, FF_TN), lambda i, j: (0, j)),
        ],
        out_specs=[
            pl.BlockSpec((TM, D_MODEL), lambda i, j: (i, 0)),
            pl.BlockSpec((1, TAIL_ROWS, FF_TN), lambda i, j: (i, 0, j)),
        ],
        out_shape=[
            jax.ShapeDtypeStruct((R_ALL, D_MODEL), f32),
            jax.ShapeDtypeStruct((R_ALL // TM, TAIL_ROWS, D_FF), f32),
        ],
        scratch_shapes=[
            pltpu.VMEM((TM, D_MODEL), bf16),
            pltpu.VMEM((TM, D_MODEL), f32),
            pltpu.VMEM((TM, FF_TN), bf16),
            pltpu.VMEM((8, D_FF), f32),
        ],
        compiler_params=_params(("arbitrary", "arbitrary")),
        name="conv_ffn",
    )(x1, g_pre, g_post, w_up_b, w_up_b, conv_w, conv_b, w_down_b, buf0, buf1)


def kernel(x_prompt, x_sample, cache_ckv, cache_krope, state_hgrn, state_conv, page_table, meta_tokens,
           lb_raw, g_mix_pre, g_mix_post, g_ffn_pre, g_ffn_post, w_in, hgrn_o_norm, q_a_norm, kv_a_norm,
           w_q_up, w_kv_up, w_out, w_ffn_up, conv_w, conv_b, w_ffn_down):
    x_all = jnp.concatenate([jnp.zeros((PAD, D_MODEL), f32), meta_tokens.astype(f32), x_prompt[0],
                             x_sample[:, 0]], axis=0)

    w_in_p = jnp.pad(w_in[0], ((0, 0), (0, IN_PAD - IN_TOTAL))).astype(bf16)
    wq = w_q_up[0].reshape(Q_RANK, MLA_HEADS, QK_DIM)
    wq_p = jnp.concatenate([wq[:, :, :QK_NOPE].reshape(Q_RANK, -1), wq[:, :, QK_NOPE:].reshape(Q_RANK, -1)],
                           axis=1).astype(bf16)
    wkv = w_kv_up[0].astype(bf16)
    w_out_b = w_out[0].astype(bf16)
    w_up_b = w_ffn_up[0].astype(bf16)
    w_down_b = w_ffn_down[0].astype(bf16)

    proj = _in_proj(x_all, g_mix_pre, w_in_p)
    qh, kh, vh, ckv, kpe = _mla_proj(proj, q_a_norm, kv_a_norm, wq_p, wkv)

    o_a_p = _attention(qh, kh, vh)
    o_h_p, s_p = _hgrn_prompt(proj, lb_raw, hgrn_o_norm)

    o_h_s, s_s = _hgrn_sample(proj, lb_raw, hgrn_o_norm, state_hgrn[0])
    q_lat = _q_latent(qh, wkv).transpose(1, 0, 2)
    q_pe_s = qh[:, P_ROWS:, QK_NOPE:].transpose(1, 0, 2)
    ckv_s = ckv[P_ROWS:]
    kpe_s = kpe[P_ROWS:]
    o_lat = _paged_attention(page_table, q_lat, q_pe_s, ckv_s[:, None, :], kpe_s[:, None, :],
                             cache_ckv, cache_krope.transpose(0, 1, 3, 2))
    o_a_s = _v_up(o_lat.transpose(1, 0, 2), wkv)

    o_h = jnp.concatenate([o_h_p, o_h_s.astype(bf16)], axis=0)
    o_a = jnp.concatenate([o_a_p, o_a_s], axis=0)
    x1 = _out_proj(o_h, o_a, x_all, w_out_b, g_mix_post)
    x2, tails = _ffn(x1, g_ffn_pre, g_ffn_post, w_up_b, conv_w[0], conv_b, w_down_b,
                     state_conv[0, :, 0], state_conv[0, :, 1])
    tail = tails[-1]

    n_tail = TAIL_ROWS - DEC_BATCH
    conv_p = tail[n_tail - 2:n_tail][None, None]
    conv_s = jnp.stack([state_conv[0, :, 1], tail[n_tail:]], axis=1)[None]
    return (x2[P_START:P_ROWS][None], x2[P_ROWS:][:, None],
            ckv[PAD:P_ROWS][None, None], kpe[PAD:P_ROWS][None, None],
            ckv_s[None, :, None], kpe_s[None, :, None],
            s_p[None, None], s_s[None], conv_p, conv_s)
```

```python
import functools

import numpy as np
import jax
import jax.numpy as jnp
from jax import lax
from jax.experimental import pallas as pl
from jax.experimental.pallas import tpu as pltpu

D_MODEL = 2048
SEQ = 8192
N_META = 16
DEC_BATCH = 128
PAST_LEN = 16384
PAGE_SIZE = 128
N_PAGES = PAST_LEN // PAGE_SIZE
HGRN_HEADS = 8
HGRN_K = 128
HGRN_V = 128
HGRN_WIDTH = HGRN_HEADS * HGRN_V
MLA_HEADS = 8
QK_NOPE = 128
QK_ROPE = 64
QK_DIM = QK_NOPE + QK_ROPE
V_HEAD = 128
Q_RANK = 512
KV_RANK = 256
MLA_WIDTH = MLA_HEADS * V_HEAD
ROPE_THETA = 10000.0
SOFTMAX_SCALE = QK_DIM ** -0.5
LOG2_E = 1.4426950408889634
D_FF = 5632
EPS = 1e-6
IN_TOTAL = 4 * HGRN_WIDTH + Q_RANK + KV_RANK + QK_ROPE

CHUNK = 128
PAD = CHUNK - N_META
P_START = PAD + N_META
P_ROWS = P_START + SEQ
R_ALL = P_ROWS + DEC_BATCH
N_CHUNKS = P_ROWS // CHUNK

LANES = 128
SUBLANES = 8
IN_PAD = 5120
IN_TN = 1024
TM = 768
TM_OUT = 384
TQ = 640
TQ_SUB = 320
FF_TN = 512
HGRN_SEQS = 16
HGRN_STAGE_SEQS = 8
SAMPLE_LOCAL = P_ROWS - (R_ALL // TM - 1) * TM
TAIL_ROWS = 136
PAGES_PER_STEP = 16
N_PAGE_STEPS = N_PAGES // PAGES_PER_STEP
KEYS_PER_STEP = PAGES_PER_STEP * PAGE_SIZE
NEG_BIG = -1e30
VMEM_LIMIT = 56 * 1024 * 1024

NT_DIMS = (((1,), (1,)), ((), ()))
TN_DIMS = (((0,), (0,)), ((), ()))

f32 = jnp.float32
bf16 = jnp.bfloat16


def _params(sem, vmem=VMEM_LIMIT):
    return pltpu.CompilerParams(dimension_semantics=sem, vmem_limit_bytes=vmem)


def _rms(x, g):
    return x * lax.rsqrt(jnp.mean(x * x, axis=-1, keepdims=True) + EPS) * g


def _sigmoid(x):
    return 1.0 / (1.0 + jnp.exp(-x))


def _silu(x):
    return x * _sigmoid(x)


def _dot(a, b):
    return jnp.dot(a, b, preferred_element_type=f32)


def _dot_nt(a, b):
    return lax.dot_general(a, b, NT_DIMS, preferred_element_type=f32)


def _dot_tn(a, b):
    return lax.dot_general(a, b, TN_DIMS, preferred_element_type=f32)


def _in_proj_kernel(x_ref, g_ref, w_ref, o_ref, h_ref):
    @pl.when(pl.program_id(1) == 0)
    def _():
        h_ref[...] = _rms(x_ref[...], g_ref[...]).astype(bf16)

    o_ref[...] = _dot(h_ref[...], w_ref[...])


def _in_proj(x_all, g, w_in_p):
    return pl.pallas_call(
        _in_proj_kernel,
        grid=(R_ALL // TM, IN_PAD // IN_TN),
        in_specs=[
            pl.BlockSpec((TM, D_MODEL), lambda i, j: (i, 0)),
            pl.BlockSpec((1, D_MODEL), lambda i, j: (0, 0)),
            pl.BlockSpec((D_MODEL, IN_TN), lambda i, j: (0, j)),
        ],
        out_specs=pl.BlockSpec((TM, IN_TN), lambda i, j: (i, j)),
        out_shape=jax.ShapeDtypeStruct((R_ALL, IN_PAD), f32),
        scratch_shapes=[pltpu.VMEM((TM, D_MODEL), bf16)],
        compiler_params=_params(("arbitrary", "arbitrary")),
        name="in_proj",
    )(x_all, g, w_in_p)


def _rope_tables(row0, n_rows):
    lane = lax.broadcasted_iota(jnp.int32, (1, LANES), 1)
    half = QK_ROPE // 2
    fidx = (lane & (half - 1)).astype(f32)
    inv = jnp.power(jnp.float32(ROPE_THETA), -fidx / half)
    rows = row0 + lax.broadcasted_iota(jnp.int32, (n_rows, 1), 0)
    pos = jnp.where(rows < P_ROWS, rows - PAD, PAST_LEN).astype(f32)
    ang = pos * inv
    sin = jnp.sin(ang)
    return jnp.cos(ang), jnp.where((lane & half) == 0, -sin, sin)


def _rotate_half(x, cos, sin_signed):
    n = x.shape[1]
    half = QK_ROPE // 2
    first_half = (lax.broadcasted_iota(jnp.int32, (1, n), 1) & half) == 0
    partner = jnp.where(first_half, pltpu.roll(x, n - half, 1), pltpu.roll(x, half, 1))
    return x * cos + partner * sin_signed


def _mla_proj_kernel(qd_ref, kvd_ref, kr_ref, qn_ref, kvn_ref, wq_ref, wkv_ref,
                     qh_ref, kh_ref, vh_ref, ckv_ref, kpe_ref):
    tm = qd_ref.shape[0]
    cos, sin_s = _rope_tables(pl.program_id(0) * tm, tm)

    q = _dot(_rms(qd_ref[...], qn_ref[...]).astype(bf16), wq_ref[...])
    reps = MLA_HEADS * QK_ROPE // LANES
    q_pe = _rotate_half(q[:, MLA_HEADS * QK_NOPE:],
                        jnp.concatenate([cos] * reps, axis=1),
                        jnp.concatenate([sin_s] * reps, axis=1))

    k_pe = _rotate_half(kr_ref[...], cos, sin_s)[:, :QK_ROPE]
    kpe_ref[...] = k_pe

    c_kv = _rms(kvd_ref[...], kvn_ref[...])
    ckv_ref[...] = c_kv
    kv = _dot(c_kv.astype(bf16), wkv_ref[...])

    k_pe_b = k_pe.astype(bf16)
    for h in range(MLA_HEADS):
        qh_ref[h, :, :QK_NOPE] = q[:, h * QK_NOPE:(h + 1) * QK_NOPE].astype(bf16)
        qh_ref[h, :, QK_NOPE:] = q_pe[:, h * QK_ROPE:(h + 1) * QK_ROPE].astype(bf16)
        base = h * (QK_NOPE + V_HEAD)
        kh_ref[h, :, :QK_NOPE] = kv[:, base:base + QK_NOPE].astype(bf16)
        kh_ref[h, :, QK_NOPE:] = k_pe_b
        vh_ref[h] = kv[:, base + QK_NOPE:base + QK_NOPE + V_HEAD].astype(bf16)


def _mla_proj(proj, q_a_norm, kv_a_norm, wq_p, wkv):
    col_q = (4 * HGRN_WIDTH) // Q_RANK
    col_kv = (4 * HGRN_WIDTH + Q_RANK) // KV_RANK
    col_kr = (4 * HGRN_WIDTH + Q_RANK + KV_RANK) // LANES
    head_spec = lambda d: pl.BlockSpec((MLA_HEADS, TM, d), lambda i: (0, i, 0))
    return pl.pallas_call(
        _mla_proj_kernel,
        grid=(R_ALL // TM,),
        in_specs=[
            pl.BlockSpec((TM, Q_RANK), lambda i: (i, col_q)),
            pl.BlockSpec((TM, KV_RANK), lambda i: (i, col_kv)),
            pl.BlockSpec((TM, LANES), lambda i: (i, col_kr)),
            pl.BlockSpec((1, Q_RANK), lambda i: (0, 0)),
            pl.BlockSpec((1, KV_RANK), lambda i: (0, 0)),
            pl.BlockSpec(wq_p.shape, lambda i: (0, 0)),
            pl.BlockSpec(wkv.shape, lambda i: (0, 0)),
        ],
        out_specs=[
            head_spec(QK_DIM), head_spec(QK_DIM), head_spec(V_HEAD),
            pl.BlockSpec((TM, KV_RANK), lambda i: (i, 0)),
            pl.BlockSpec((TM, QK_ROPE), lambda i: (i, 0)),
        ],
        out_shape=[
            jax.ShapeDtypeStruct((MLA_HEADS, R_ALL, QK_DIM), bf16),
            jax.ShapeDtypeStruct((MLA_HEADS, R_ALL, QK_DIM), bf16),
            jax.ShapeDtypeStruct((MLA_HEADS, R_ALL, V_HEAD), bf16),
            jax.ShapeDtypeStruct((R_ALL, KV_RANK), f32),
            jax.ShapeDtypeStruct((R_ALL, QK_ROPE), f32),
        ],
        compiler_params=_params(("arbitrary",)),
        name="mla_proj",
    )(proj, proj, proj, q_a_norm, kv_a_norm, wq_p, wkv)


def _attn_kernel(q_ref, k_ref, v_ref, o_ref, m_ref, l_ref, acc_ref, ta_ref, tb_ref):
    i = pl.program_id(1)
    q = q_ref[0]
    m_ref[...] = jnp.full(m_ref.shape, NEG_BIG, f32)
    l_ref[...] = jnp.zeros(l_ref.shape, f32)
    acc_ref[...] = jnp.zeros(acc_ref.shape, f32)
    reps = TQ // LANES
    groups = range(0, TQ, TQ_SUB)

    def logits(t_ref, j):
        k = k_ref[0, pl.ds(pl.multiple_of(j * TQ, TQ), TQ), :]
        for r0 in groups:
            t_ref[r0:r0 + TQ_SUB] = _dot_nt(q[r0:r0 + TQ_SUB], k)

    def block(t_ref, j, masked):
        v = v_ref[0, pl.ds(pl.multiple_of(j * TQ, TQ), TQ), :]
        for r0 in groups:
            rows = slice(r0, r0 + TQ_SUB)
            t = t_ref[rows] * (SOFTMAX_SCALE * LOG2_E)
            if masked:
                q_row = i * TQ + r0 + lax.broadcasted_iota(jnp.int32, (TQ_SUB, TQ), 0)
                k_row = j * TQ + lax.broadcasted_iota(jnp.int32, (TQ_SUB, TQ), 1)
                t = jnp.where(k_row <= q_row, jnp.where(k_row >= PAD, t, NEG_BIG), NEG_BIG)
            m_prev = m_ref[rows]
            m_next = jnp.maximum(m_prev, jnp.max(t, axis=1, keepdims=True))
            p = jnp.exp2(t - jnp.concatenate([m_next] * reps, axis=1))
            alpha = jnp.exp2(m_prev - m_next)
            l_ref[rows] = alpha * l_ref[rows] + jnp.sum(p, axis=1, keepdims=True)
            acc_ref[rows] = alpha * acc_ref[rows] + _dot(p.astype(bf16), v)
            m_ref[rows] = m_next

    logits(ta_ref, 0)

    @pl.when(i == 0)
    def _():
        block(ta_ref, 0, True)

    @pl.when(i > 0)
    def _():
        logits(tb_ref, 1)
        block(ta_ref, 0, True)
        n_inner = i - 1

        def pair(p, carry):
            j = 1 + 2 * p
            logits(ta_ref, j + 1)
            block(tb_ref, j, False)
            logits(tb_ref, j + 2)
            block(ta_ref, j + 1, False)
            return carry

        lax.fori_loop(0, n_inner // 2, pair, 0)
        j = 1 + 2 * (n_inner // 2)

        @pl.when(n_inner % 2 == 1)
        def _():
            logits(ta_ref, j + 1)
            block(tb_ref, j, False)
            block(ta_ref, j + 1, True)

        @pl.when(n_inner % 2 == 0)
        def _():
            block(tb_ref, j, True)

    o_ref[...] = (acc_ref[...] / l_ref[...]).astype(bf16)


def _attention(qh, kh, vh):
    return pl.pallas_call(
        _attn_kernel,
        grid=(MLA_HEADS, P_ROWS // TQ),
        in_specs=[
            pl.BlockSpec((1, TQ, QK_DIM), lambda h, i: (h, i, 0)),
            pl.BlockSpec((1, R_ALL, QK_DIM), lambda h, i: (h, 0, 0)),
            pl.BlockSpec((1, R_ALL, V_HEAD), lambda h, i: (h, 0, 0)),
        ],
        out_specs=pl.BlockSpec((TQ, V_HEAD), lambda h, i: (i, h)),
        out_shape=jax.ShapeDtypeStruct((P_ROWS, MLA_WIDTH), bf16),
        scratch_shapes=[pltpu.VMEM((TQ, LANES), f32), pltpu.VMEM((TQ, LANES), f32),
                        pltpu.VMEM((TQ, V_HEAD), f32),
                        pltpu.VMEM((TQ, TQ), f32), pltpu.VMEM((TQ, TQ), f32)],
        compiler_params=_params(("arbitrary", "arbitrary")),
        name="prompt_attn",
    )(qh, kh, vh)


N_LEVELS = 7


def _hgrn_constants():
    t = np.arange(CHUNK)
    sums = (t[None, :] <= t[:, None]).astype(np.float32)
    x = t[:, None] ^ t[None, :]
    lev = np.where(x > 0, np.floor(np.log2(np.maximum(x, 1))).astype(np.int32), N_LEVELS)
    lev = np.where(t[None, :] > t[:, None], N_LEVELS + 1, lev).astype(np.int32)
    return sums, lev


def _split3(x):
    a = x.astype(bf16)
    r = x - a.astype(f32)
    b = r.astype(bf16)
    c = (r - b.astype(f32)).astype(bf16)
    return a, b, c


def _lower_bound(lb_raw_ref):
    a0 = lb_raw_ref[0:1, :]
    a1 = lb_raw_ref[1:2, :]
    m = jnp.maximum(a0, a1)
    e0 = jnp.exp(a0 - m)
    return e0 / (e0 + jnp.exp(a1 - m))


def _pair_block_reference(b, lvl):
    m = 1 << lvl
    n, w = b.shape
    if m >= SUBLANES:
        g = m // SUBLANES
        b4 = b.reshape(n // (2 * m), 2 * g, SUBLANES, w)
        return jnp.broadcast_to(b4[:, g - 1:g, SUBLANES - 1:, :], b4.shape).reshape(n, w)
    b3 = b.reshape(n // SUBLANES, SUBLANES, w)
    sub = lax.broadcasted_iota(jnp.int32, (1, SUBLANES, 1), 1)
    ref = b3[:, m - 1:m, :]
    for start in range(2 * m, SUBLANES, 2 * m):
        ref = jnp.where(sub >= start, b3[:, start + m - 1:start + m, :], ref)
    return jnp.broadcast_to(ref, b3.shape).reshape(n, w)


def _hgrn_prompt_kernel(hq_ref, hf_ref, hi_ref, hg_ref, lb_ref, on_ref, sums_ref, lev_ref,
                        o_ref, s_ref, st_ref):
    c = pl.program_id(0)

    @pl.when(c == 0)
    def _():
        st_ref[...] = jnp.zeros(st_ref.shape, f32)

    @pl.when(c == N_CHUNKS)
    def _():
        o_ref[...] = jnp.zeros(o_ref.shape, bf16)

    @pl.when(c < N_CHUNKS)
    def _():
        lb = _lower_bound(lb_ref)
        f_all = lb + (1.0 - lb) * _sigmoid(hf_ref[...])
        l1, l2, l3 = _split3(jnp.log(f_all))
        sums = sums_ref[...]
        b_all = _dot(sums, l1) + _dot(sums, l2) + _dot(sums, l3)

        row = lax.broadcasted_iota(jnp.int32, (CHUNK, 1), 0)
        lev = lev_ref[...]
        refs = [_pair_block_reference(b_all, lvl) for lvl in range(N_LEVELS)]
        for h in range(HGRN_HEADS):
            sl = slice(h * HGRN_K, (h + 1) * HGRN_K)
            q = _silu(hq_ref[:, sl])
            k = 1.0 - f_all[:, sl]
            v = hi_ref[:, sl].astype(bf16)
            b = b_all[:, sl]
            a = jnp.where(lev == N_LEVELS, _dot_nt(q.astype(bf16), k.astype(bf16)), 0.0)
            for lvl in range(N_LEVELS):
                is_q = ((row >> lvl) & 1) == 1
                d = b - refs[lvl][:, sl]
                w = (jnp.where(is_q, q, k) * jnp.exp(jnp.where(is_q, d, -d))).astype(bf16)
                a = jnp.where(lev == lvl, _dot_nt(w, w), a)

            b_suf = b[CHUNK - 1:CHUNK, :] - b
            st = st_ref[h]
            o = _dot(a.astype(bf16), v) + _dot_nt((q * jnp.exp(b)).astype(bf16), st.astype(bf16))
            st_new = st * jnp.exp(b[CHUNK - 1:CHUNK, :]) + _dot_tn(v, (k * jnp.exp(b_suf)).astype(bf16))
            st_ref[h] = st_new
            o_ref[:, sl] = (_rms(o, on_ref[...]) * _silu(hg_ref[:, sl])).astype(bf16)

    @pl.when(c == N_CHUNKS - 1)
    def _():
        for h in range(HGRN_HEADS):
            s_ref[h] = st_ref[h].T


def _hgrn_prompt(proj, lb_raw, o_norm):
    sums, lev = _hgrn_constants()
    blk = lambda part: pl.BlockSpec((CHUNK, HGRN_WIDTH), lambda c: (c, part))
    return pl.pallas_call(
        _hgrn_prompt_kernel,
        grid=(N_CHUNKS + DEC_BATCH // CHUNK,),
        in_specs=[
            blk(0), blk(1), blk(2), blk(3),
            pl.BlockSpec((2, HGRN_WIDTH), lambda c: (0, 0)),
            pl.BlockSpec((1, HGRN_V), lambda c: (0, 0)),
            pl.BlockSpec(sums.shape, lambda c: (0, 0)),
            pl.BlockSpec(lev.shape, lambda c: (0, 0)),
        ],
        out_specs=[
            pl.BlockSpec((CHUNK, HGRN_WIDTH), lambda c: (c, 0)),
            pl.BlockSpec((HGRN_HEADS, HGRN_K, HGRN_V), lambda c: (0, 0, 0)),
        ],
        out_shape=[
            jax.ShapeDtypeStruct((R_ALL, HGRN_WIDTH), bf16),
            jax.ShapeDtypeStruct((HGRN_HEADS, HGRN_K, HGRN_V), f32),
        ],
        scratch_shapes=[pltpu.VMEM((HGRN_HEADS, HGRN_V, HGRN_K), f32)],
        compiler_params=_params(("arbitrary",)),
        name="hgrn_prompt",
    )(proj, proj, proj, proj, lb_raw, o_norm, jnp.asarray(sums, bf16), jnp.asarray(lev))


def _hgrn_sample_kernel(hq_ref, hf_ref, hi_ref, hg_ref, lb_ref, on_ref, s0_ref, oh_hbm, o_ref, s_ref):
    del oh_hbm
    lb = _lower_bound(lb_ref)
    f = lb + (1.0 - lb) * _sigmoid(hf_ref[...])
    q = _silu(hq_ref[...])
    k = (1.0 - f).astype(bf16).astype(f32)
    v = hi_ref[...].astype(bf16).astype(f32)
    gate = _silu(hg_ref[...])
    f1 = f.astype(bf16).astype(f32)
    f2 = (f - f1).astype(bf16).astype(f32)
    f3 = ((f - f1) - f2).astype(bf16).astype(f32)

    n_rows = 16
    rid = lax.broadcasted_iota(jnp.int32, (n_rows, HGRN_K), 0)
    rid2 = lax.broadcasted_iota(jnp.int32, (n_rows, 2 * HGRN_V), 0)
    left = lax.broadcasted_iota(jnp.int32, (n_rows, 2 * HGRN_V), 1) < HGRN_V
    zero_v = jnp.zeros((1, HGRN_V), f32)

    at = lambda x, s, h: x[s:s + 1, h * HGRN_K:(h + 1) * HGRN_K]

    def mix(s, h):
        lhs = jnp.where(rid == 0, at(f1, s, h), jnp.where(rid == 1, at(f2, s, h), jnp.where(
            rid == 2, at(f3, s, h), jnp.where(rid == 3, at(k, s, h), 0.0))))
        rhs = jnp.where(rid2 < 3, jnp.where(left, 1.0, 0.0),
                        jnp.where(rid2 == 3, jnp.concatenate([zero_v, at(v, s, h)], axis=1), 0.0))
        return _dot_tn(lhs.astype(bf16), rhs.astype(bf16))

    outs = []
    for s0 in range(0, HGRN_SEQS, HGRN_STAGE_SEQS):
        pairs = [(s, h) for s in range(s0, s0 + HGRN_STAGE_SEQS) for h in range(HGRN_HEADS)]
        mixes = [mix(s, h) for s, h in pairs]
        for (s, h), mx in zip(pairs, mixes):
            s_ref[s, h] = mx[:, :HGRN_V] * s0_ref[s, h] + mx[:, HGRN_V:]
        o_raw = [_dot(jnp.where(rid == 0, at(q, s, h), 0.0).astype(bf16), s_ref[s, h].astype(bf16))[0:1]
                 for s, h in pairs]
        outs += [_rms(o, on_ref[...]) * at(gate, s, h) for (s, h), o in zip(pairs, o_raw)]
    o_ref[...] = jnp.concatenate(
        [jnp.concatenate(outs[s * HGRN_HEADS:(s + 1) * HGRN_HEADS], axis=1) for s in range(HGRN_SEQS)],
        axis=0).astype(bf16)


def _hgrn_sample(proj, lb_raw, o_norm, state, o_h):
    row_blk = P_ROWS // HGRN_SEQS
    blk = lambda part: pl.BlockSpec((HGRN_SEQS, HGRN_WIDTH), lambda b: (row_blk + b, part))
    state_spec = pl.BlockSpec((HGRN_SEQS, HGRN_HEADS, HGRN_K, HGRN_V), lambda b: (b, 0, 0, 0))
    return pl.pallas_call(
        _hgrn_sample_kernel,
        grid=(DEC_BATCH // HGRN_SEQS,),
        in_specs=[
            blk(0), blk(1), blk(2), blk(3),
            pl.BlockSpec((2, HGRN_WIDTH), lambda b: (0, 0)),
            pl.BlockSpec((1, HGRN_V), lambda b: (0, 0)),
            state_spec,
            pl.BlockSpec(memory_space=pl.ANY),
        ],
        out_specs=[blk(0), state_spec],
        out_shape=[
            jax.ShapeDtypeStruct(o_h.shape, o_h.dtype),
            jax.ShapeDtypeStruct(state.shape, f32),
        ],
        input_output_aliases={7: 0},
        compiler_params=_params(("arbitrary",)),
        name="hgrn_sample",
    )(proj, proj, proj, proj, lb_raw, o_norm, state, o_h)


def _q_latent_kernel(q_ref, wuk_ref, o_ref):
    o_ref[0] = _dot_nt(q_ref[0][:, :QK_NOPE], wuk_ref[...]).astype(bf16)


def _q_latent(qh, wkv):
    row_blk = P_ROWS // DEC_BATCH
    return pl.pallas_call(
        _q_latent_kernel,
        grid=(MLA_HEADS,),
        in_specs=[
            pl.BlockSpec((1, DEC_BATCH, QK_DIM), lambda h: (h, row_blk, 0)),
            pl.BlockSpec((KV_RANK, QK_NOPE), lambda h: (0, 2 * h)),
        ],
        out_specs=pl.BlockSpec((1, DEC_BATCH, KV_RANK), lambda h: (h, 0, 0)),
        out_shape=jax.ShapeDtypeStruct((MLA_HEADS, DEC_BATCH, KV_RANK), bf16),
        compiler_params=_params(("arbitrary",)),
        name="q_latent",
    )(qh, wkv)


def _page_copies(pt_ref, ckv_hbm, kpe_hbm, ckv_buf, kpe_buf, sems, seq, c):
    copies = []
    for p in range(PAGES_PER_STEP):
        page = pt_ref[seq, c * PAGES_PER_STEP + p]
        copies.append(pltpu.make_async_copy(ckv_hbm.at[0, page], ckv_buf.at[c, p], sems.at[0, c]))
        copies.append(pltpu.make_async_copy(kpe_hbm.at[0, page], kpe_buf.at[c, p], sems.at[1, c]))
    return copies


def _paged_attn_kernel(pt_ref, ql_ref, qp_ref, cn_ref, kn_ref, ckv_hbm, kpe_hbm, o_ref,
                       ckv_buf, kpe_buf, sems, ckb_ref, kpb_ref):
    b = pl.program_id(0)
    copies = functools.partial(_page_copies, pt_ref, ckv_hbm, kpe_hbm, ckv_buf, kpe_buf, sems)
    to_log2 = SOFTMAX_SCALE * LOG2_E

    @pl.when(b == 0)
    def _():
        for c in range(N_PAGE_STEPS):
            for cp in copies(b, c):
                cp.start()

    ql = ql_ref[0]
    qp = qp_ref[0]

    def scores(c):
        for cp in copies(b, c):
            cp.wait()
        half = c % 2
        ckb_ref[half] = ckv_buf[c].reshape(KEYS_PER_STEP, KV_RANK).astype(bf16)
        kpb_ref[half] = jnp.concatenate([kpe_buf[c, p] for p in range(PAGES_PER_STEP)], axis=1).astype(bf16)
        return (_dot_nt(ql, ckb_ref[half]) + _dot(qp, kpb_ref[half])) * to_log2

    t_new = (jnp.sum(ql.astype(f32) * cn_ref[0], axis=1, keepdims=True)
             + jnp.sum(qp.astype(f32) * kn_ref[0], axis=1, keepdims=True)) * to_log2
    m = jnp.broadcast_to(t_new, (MLA_HEADS, LANES))
    l = jnp.ones((MLA_HEADS, LANES), f32)
    acc = jnp.broadcast_to(cn_ref[0], (MLA_HEADS, KV_RANK))

    t = scores(0)
    for c in range(N_PAGE_STEPS):
        t_next = scores(c + 1) if c + 1 < N_PAGE_STEPS else None
        m_next = jnp.maximum(m, jnp.max(t, axis=1, keepdims=True))
        p = jnp.exp2(t - jnp.concatenate([m_next] * (KEYS_PER_STEP // LANES), axis=1))
        alpha = jnp.exp2(m - m_next)
        l = alpha * l + jnp.sum(p, axis=1, keepdims=True)
        acc = jnp.concatenate([alpha] * (KV_RANK // LANES), axis=1) * acc + _dot(p.astype(bf16), ckb_ref[c % 2])
        m = m_next

        @pl.when(b + 1 < pl.num_programs(0))
        def _():
            for cp in copies(b + 1, c):
                cp.start()

        t = t_next

    o_ref[0] = acc / jnp.concatenate([l] * (KV_RANK // LANES), axis=1)


def _paged_attention(page_table, q_lat, q_pe, ckv_new, kpe_new, cache_ckv, cache_krope_t):
    seq_spec = lambda r, d: pl.BlockSpec((1, r, d), lambda b, pt: (b, 0, 0))
    grid_spec = pltpu.PrefetchScalarGridSpec(
        num_scalar_prefetch=1,
        grid=(DEC_BATCH,),
        in_specs=[
            seq_spec(MLA_HEADS, KV_RANK), seq_spec(MLA_HEADS, QK_ROPE),
            seq_spec(1, KV_RANK), seq_spec(1, QK_ROPE),
            pl.BlockSpec(memory_space=pl.ANY), pl.BlockSpec(memory_space=pl.ANY),
        ],
        out_specs=seq_spec(MLA_HEADS, KV_RANK),
        scratch_shapes=[
            pltpu.VMEM((N_PAGE_STEPS, PAGES_PER_STEP, PAGE_SIZE, KV_RANK), f32),
            pltpu.VMEM((N_PAGE_STEPS, PAGES_PER_STEP, QK_ROPE, PAGE_SIZE), f32),
            pltpu.SemaphoreType.DMA((2, N_PAGE_STEPS)),
            pltpu.VMEM((2, KEYS_PER_STEP, KV_RANK), bf16),
            pltpu.VMEM((2, QK_ROPE, KEYS_PER_STEP), bf16),
        ],
    )
    return pl.pallas_call(
        _paged_attn_kernel,
        grid_spec=grid_spec,
        out_shape=jax.ShapeDtypeStruct((DEC_BATCH, MLA_HEADS, KV_RANK), f32),
        compiler_params=_params(("arbitrary",)),
        name="paged_attn",
    )(page_table, q_lat, q_pe, ckv_new, kpe_new, cache_ckv, cache_krope_t)


def _v_up_kernel(o_ref, wuv_ref, out_ref):
    out_ref[...] = _dot(o_ref[0].astype(bf16), wuv_ref[...]).astype(bf16)


def _v_up(o_lat_h, wkv):
    return pl.pallas_call(
        _v_up_kernel,
        grid=(MLA_HEADS,),
        in_specs=[
            pl.BlockSpec((1, DEC_BATCH, KV_RANK), lambda h: (h, 0, 0)),
            pl.BlockSpec((KV_RANK, V_HEAD), lambda h: (0, 2 * h + 1)),
        ],
        out_specs=pl.BlockSpec((DEC_BATCH, V_HEAD), lambda h: (0, h)),
        out_shape=jax.ShapeDtypeStruct((DEC_BATCH, MLA_WIDTH), bf16),
        compiler_params=_params(("arbitrary",)),
        name="v_up",
    )(o_lat_h, wkv)


def _out_proj_kernel(oh_ref, oa_ref, x_ref, wh_ref, wa_ref, g_ref, o_ref):
    mix = _dot(oh_ref[...], wh_ref[...]) + _dot(oa_ref[...], wa_ref[...])
    o_ref[...] = x_ref[...] + _rms(mix, g_ref[...])


def _out_proj(o_h, o_a, x_all, w_out_b, g):
    return pl.pallas_call(
        _out_proj_kernel,
        grid=(R_ALL // TM_OUT,),
        in_specs=[
            pl.BlockSpec((TM_OUT, HGRN_WIDTH), lambda i: (i, 0)),
            pl.BlockSpec((TM_OUT, MLA_WIDTH), lambda i: (i, 0)),
            pl.BlockSpec((TM_OUT, D_MODEL), lambda i: (i, 0)),
            pl.BlockSpec((HGRN_WIDTH, D_MODEL), lambda i: (0, 0)),
            pl.BlockSpec((MLA_WIDTH, D_MODEL), lambda i: (1, 0)),
            pl.BlockSpec((1, D_MODEL), lambda i: (0, 0)),
        ],
        out_specs=pl.BlockSpec((TM_OUT, D_MODEL), lambda i: (i, 0)),
        out_shape=jax.ShapeDtypeStruct((R_ALL, D_MODEL), f32),
        compiler_params=_params(("arbitrary",)),
        name="out_proj",
    )(o_h, o_a, x_all, w_out_b, w_out_b, g)


def _ffn_kernel(x_ref, gpre_ref, gpost_ref, wg_ref, wv_ref, cw_ref, cb_ref, wd_ref, buf0_ref, buf1_ref,
                o_ref, tail_ref, h_ref, act_ref, carry_ref):
    i = pl.program_id(0)
    j = pl.program_id(1)
    col = pl.ds(pl.multiple_of(j * FF_TN, FF_TN), FF_TN)

    @pl.when(j == 0)
    def _():
        h_ref[...] = _rms(x_ref[...], gpre_ref[...]).astype(bf16)
        o_ref[...] = jnp.zeros(o_ref.shape, f32)

    @pl.when(i == 0)
    def _():
        carry_ref[:, col] = jnp.zeros((8, FF_TN), f32)

    h = h_ref[...]
    row = lax.broadcasted_iota(jnp.int32, (TM, 1), 0)
    g = jnp.where(row + i * TM >= PAD, _dot(h, wg_ref[...]), 0.0)
    val = _dot(h, wv_ref[...])
    prev = carry_ref[:, col]
    g1 = jnp.where(row == 0, prev[7:8], pltpu.roll(g, 1, 0))
    g2 = jnp.where(row == 0, prev[6:7], jnp.where(row == 1, prev[7:8], pltpu.roll(g, 2, 0)))
    w0, w1, w2 = cw_ref[0:1, :], cw_ref[1:2, :], cw_ref[2:3, :]
    y = cb_ref[...] + w0 * g2 + w1 * g1 + w2 * g
    act_ref[...] = (_silu(y) * val).astype(bf16)
    carry_ref[:, col] = g[TM - 8:, :]
    tail_ref[0] = g[TM - TAIL_ROWS:, :]

    @pl.when(i == pl.num_programs(0) - 1)
    def _():
        ys = cb_ref[...] + w0 * buf0_ref[...] + w1 * buf1_ref[...] + w2 * g[SAMPLE_LOCAL:, :]
        act_ref[SAMPLE_LOCAL:, :] = (_silu(ys) * val[SAMPLE_LOCAL:, :]).astype(bf16)

    o_ref[...] += _dot(act_ref[...], wd_ref[...])

    @pl.when(j == pl.num_programs(1) - 1)
    def _():
        o_ref[...] = x_ref[...] + _rms(o_ref[...], gpost_ref[...])


def _ffn(x1, g_pre, g_post, w_up_b, conv_w, conv_b, w_down_b, buf0, buf1):
    n_ff = D_FF // FF_TN
    return pl.pallas_call(
        _ffn_kernel,
        grid=(R_ALL // TM, n_ff),
        in_specs=[
            pl.BlockSpec((TM, D_MODEL), lambda i, j: (i, 0)),
            pl.BlockSpec((1, D_MODEL), lambda i, j: (0, 0)),
            pl.BlockSpec((1, D_MODEL), lambda i, j: (0, 0)),
            pl.BlockSpec((D_MODEL, FF_TN), lambda i, j: (0, j)),
            pl.BlockSpec((D_MODEL, FF_TN), lambda i, j: (0, j + n_ff)),
            pl.BlockSpec((3, FF_TN), lambda i, j: (0, j)),
            pl.BlockSpec((1, FF_TN), lambda i, j: (0, j)),
            pl.BlockSpec((FF_TN, D_MODEL), lambda i, j: (j, 0)),
            pl.BlockSpec((DEC_BATCH, FF_TN), lambda i, j: (0, j)),
            pl.BlockSpec((DEC_BATCH, FF_TN), lambda i, j: (0, j)),
        ],
        out_specs=[
            pl.BlockSpec((TM, D_MODEL), lambda i, j: (i, 0)),
            pl.BlockSpec((1, TAIL_ROWS, FF_TN), lambda i, j: (i, 0, j)),
        ],
        out_shape=[
            jax.ShapeDtypeStruct((R_ALL, D_MODEL), f32),
            jax.ShapeDtypeStruct((R_ALL // TM, TAIL_ROWS, D_FF), f32),
        ],
        scratch_shapes=[
            pltpu.VMEM((TM, D_MODEL), bf16),
            pltpu.VMEM((TM, FF_TN), bf16),
            pltpu.VMEM((8, D_FF), f32),
        ],
        compiler_params=_params(("arbitrary", "arbitrary")),
        name="conv_ffn",
    )(x1, g_pre, g_post, w_up_b, w_up_b, conv_w, conv_b, w_down_b, buf0, buf1)


def kernel(x_prompt, x_sample, cache_ckv, cache_krope, state_hgrn, state_conv, page_table, meta_tokens,
           lb_raw, g_mix_pre, g_mix_post, g_ffn_pre, g_ffn_post, w_in, hgrn_o_norm, q_a_norm, kv_a_norm,
           w_q_up, w_kv_up, w_out, w_ffn_up, conv_w, conv_b, w_ffn_down):
    x_all = jnp.concatenate([jnp.zeros((PAD, D_MODEL), f32), meta_tokens.astype(f32), x_prompt[0],
                             x_sample[:, 0]], axis=0)

    w_in_p = jnp.pad(w_in[0], ((0, 0), (0, IN_PAD - IN_TOTAL))).astype(bf16)
    wq = w_q_up[0].reshape(Q_RANK, MLA_HEADS, QK_DIM)
    wq_p = jnp.concatenate([wq[:, :, :QK_NOPE].reshape(Q_RANK, -1), wq[:, :, QK_NOPE:].reshape(Q_RANK, -1)],
                           axis=1).astype(bf16)
    wkv = w_kv_up[0].astype(bf16)
    w_out_b = w_out[0].astype(bf16)
    w_up_b = w_ffn_up[0].astype(bf16)
    w_down_b = w_ffn_down[0].astype(bf16)

    proj = _in_proj(x_all, g_mix_pre, w_in_p)
    qh, kh, vh, ckv, kpe = _mla_proj(proj, q_a_norm, kv_a_norm, wq_p, wkv)

    o_a = _attention(qh, kh, vh)
    o_h, s_p = _hgrn_prompt(proj, lb_raw, hgrn_o_norm)

    o_h, s_s = _hgrn_sample(proj, lb_raw, hgrn_o_norm, state_hgrn[0], o_h)
    q_lat = _q_latent(qh, wkv).transpose(1, 0, 2)
    q_pe_s = qh[:, P_ROWS:, QK_NOPE:].transpose(1, 0, 2)
    ckv_s = ckv[P_ROWS:]
    kpe_s = kpe[P_ROWS:]
    o_lat = _paged_attention(page_table, q_lat, q_pe_s, ckv_s[:, None, :], kpe_s[:, None, :],
                             cache_ckv, cache_krope.transpose(0, 1, 3, 2))
    o_a = jnp.concatenate([o_a, _v_up(o_lat.transpose(1, 0, 2), wkv)], axis=0)

    x1 = _out_proj(o_h, o_a, x_all, w_out_b, g_mix_post)
    x2, tails = _ffn(x1, g_ffn_pre, g_ffn_post, w_up_b, conv_w[0], conv_b, w_down_b,
                     state_conv[0, :, 0], state_conv[0, :, 1])
    tail = tails[-1]

    n_tail = TAIL_ROWS - DEC_BATCH
    conv_p = tail[n_tail - 2:n_tail][None, None]
    conv_s = jnp.stack([state_conv[0, :, 1], tail[n_tail:]], axis=1)[None]
    return (x2[P_START:P_ROWS][None], x2[P_ROWS:][:, None],
            ckv[PAD:P_ROWS][None, None], kpe[PAD:P_ROWS][None, None],
            ckv_s[None, :, None], kpe_s[None, :, None],
            s_p[None, None], s_s[None], conv_p, conv_s)
```

```python
import functools

import numpy as np
import jax
import jax.numpy as jnp
from jax import lax
from jax.experimental import pallas as pl
from jax.experimental.pallas import tpu as pltpu

D_MODEL = 2048
SEQ = 8192
N_META = 16
DEC_BATCH = 128
PAST_LEN = 16384
PAGE_SIZE = 128
N_PAGES = PAST_LEN // PAGE_SIZE
HGRN_HEADS = 8
HGRN_K = 128
HGRN_V = 128
HGRN_WIDTH = HGRN_HEADS * HGRN_V
MLA_HEADS = 8
QK_NOPE = 128
QK_ROPE = 64
QK_DIM = QK_NOPE + QK_ROPE
V_HEAD = 128
Q_RANK = 512
KV_RANK = 256
MLA_WIDTH = MLA_HEADS * V_HEAD
ROPE_THETA = 10000.0
SOFTMAX_SCALE = QK_DIM ** -0.5
LOG2_E = 1.4426950408889634
D_FF = 5632
EPS = 1e-6
IN_TOTAL = 4 * HGRN_WIDTH + Q_RANK + KV_RANK + QK_ROPE

CHUNK = 128
PAD = CHUNK - N_META
P_START = PAD + N_META
P_ROWS = P_START + SEQ
R_ALL = P_ROWS + DEC_BATCH
N_CHUNKS = P_ROWS // CHUNK

LANES = 128
SUBLANES = 8
IN_PAD = 5120
IN_TN = 1024
TM = 768
TM_OUT = 384
TQ = 640
TQ_SUB = 320
FF_TN = 512
HGRN_SEQS = 16
HGRN_STAGE_SEQS = 8
SAMPLE_LOCAL = P_ROWS - (R_ALL // TM - 1) * TM
TAIL_ROWS = 136
PAGES_PER_STEP = 16
N_PAGE_STEPS = N_PAGES // PAGES_PER_STEP
KEYS_PER_STEP = PAGES_PER_STEP * PAGE_SIZE
NEG_BIG = -1e30
VMEM_LIMIT = 56 * 1024 * 1024

NT_DIMS = (((1,), (1,)), ((), ()))
TN_DIMS = (((0,), (0,)), ((), ()))

f32 = jnp.float32
bf16 = jnp.bfloat16


def _params(sem, vmem=VMEM_LIMIT):
    return pltpu.CompilerParams(dimension_semantics=sem, vmem_limit_bytes=vmem)


def _rms(x, g):
    return x * lax.rsqrt(jnp.mean(x * x, axis=-1, keepdims=True) + EPS) * g


def _sigmoid(x):
    return 1.0 / (1.0 + jnp.exp(-x))


def _silu(x):
    return x * _sigmoid(x)


def _dot(a, b):
    return jnp.dot(a, b, preferred_element_type=f32)


def _dot_nt(a, b):
    return lax.dot_general(a, b, NT_DIMS, preferred_element_type=f32)


def _dot_tn(a, b):
    return lax.dot_general(a, b, TN_DIMS, preferred_element_type=f32)


def _in_proj_kernel(x_ref, g_ref, w_ref, o_ref, h_ref):
    @pl.when(pl.program_id(1) == 0)
    def _():
        h_ref[...] = _rms(x_ref[...], g_ref[...]).astype(bf16)

    o_ref[...] = _dot(h_ref[...], w_ref[...])


def _in_proj(x_all, g, w_in_p):
    return pl.pallas_call(
        _in_proj_kernel,
        grid=(R_ALL // TM, IN_PAD // IN_TN),
        in_specs=[
            pl.BlockSpec((TM, D_MODEL), lambda i, j: (i, 0)),
            pl.BlockSpec((1, D_MODEL), lambda i, j: (0, 0)),
            pl.BlockSpec((D_MODEL, IN_TN), lambda i, j: (0, j)),
        ],
        out_specs=pl.BlockSpec((TM, IN_TN), lambda i, j: (i, j)),
        out_shape=jax.ShapeDtypeStruct((R_ALL, IN_PAD), f32),
        scratch_shapes=[pltpu.VMEM((TM, D_MODEL), bf16)],
        compiler_params=_params(("arbitrary", "arbitrary")),
        name="in_proj",
    )(x_all, g, w_in_p)


def _rope_tables(row0, n_rows):
    lane = lax.broadcasted_iota(jnp.int32, (1, LANES), 1)
    half = QK_ROPE // 2
    fidx = (lane & (half - 1)).astype(f32)
    inv = jnp.power(jnp.float32(ROPE_THETA), -fidx / half)
    rows = row0 + lax.broadcasted_iota(jnp.int32, (n_rows, 1), 0)
    pos = jnp.where(rows < P_ROWS, rows - PAD, PAST_LEN).astype(f32)
    ang = pos * inv
    sin = jnp.sin(ang)
    return jnp.cos(ang), jnp.where((lane & half) == 0, -sin, sin)


def _rotate_half(x, cos, sin_signed):
    n = x.shape[1]
    half = QK_ROPE // 2
    first_half = (lax.broadcasted_iota(jnp.int32, (1, n), 1) & half) == 0
    partner = jnp.where(first_half, pltpu.roll(x, n - half, 1), pltpu.roll(x, half, 1))
    return x * cos + partner * sin_signed


def _mla_proj_kernel(qd_ref, kvd_ref, kr_ref, qn_ref, kvn_ref, wq_ref, wkv_ref,
                     qh_ref, kh_ref, vh_ref, ckv_ref, kpe_ref):
    tm = qd_ref.shape[0]
    cos, sin_s = _rope_tables(pl.program_id(0) * tm, tm)

    q = _dot(_rms(qd_ref[...], qn_ref[...]).astype(bf16), wq_ref[...])
    reps = MLA_HEADS * QK_ROPE // LANES
    q_pe = _rotate_half(q[:, MLA_HEADS * QK_NOPE:],
                        jnp.concatenate([cos] * reps, axis=1),
                        jnp.concatenate([sin_s] * reps, axis=1))

    k_pe = _rotate_half(kr_ref[...], cos, sin_s)[:, :QK_ROPE]
    kpe_ref[...] = k_pe

    c_kv = _rms(kvd_ref[...], kvn_ref[...])
    ckv_ref[...] = c_kv
    kv = _dot(c_kv.astype(bf16), wkv_ref[...])

    k_pe_b = k_pe.astype(bf16)
    for h in range(MLA_HEADS):
        qh_ref[h, :, :QK_NOPE] = q[:, h * QK_NOPE:(h + 1) * QK_NOPE].astype(bf16)
        qh_ref[h, :, QK_NOPE:] = q_pe[:, h * QK_ROPE:(h + 1) * QK_ROPE].astype(bf16)
        base = h * (QK_NOPE + V_HEAD)
        kh_ref[h, :, :QK_NOPE] = kv[:, base:base + QK_NOPE].astype(bf16)
        kh_ref[h, :, QK_NOPE:] = k_pe_b
        vh_ref[h] = kv[:, base + QK_NOPE:base + QK_NOPE + V_HEAD].astype(bf16)


def _mla_proj(proj, q_a_norm, kv_a_norm, wq_p, wkv):
    col_q = (4 * HGRN_WIDTH) // Q_RANK
    col_kv = (4 * HGRN_WIDTH + Q_RANK) // KV_RANK
    col_kr = (4 * HGRN_WIDTH + Q_RANK + KV_RANK) // LANES
    head_spec = lambda d: pl.BlockSpec((MLA_HEADS, TM, d), lambda i: (0, i, 0))
    return pl.pallas_call(
        _mla_proj_kernel,
        grid=(R_ALL // TM,),
        in_specs=[
            pl.BlockSpec((TM, Q_RANK), lambda i: (i, col_q)),
            pl.BlockSpec((TM, KV_RANK), lambda i: (i, col_kv)),
            pl.BlockSpec((TM, LANES), lambda i: (i, col_kr)),
            pl.BlockSpec((1, Q_RANK), lambda i: (0, 0)),
            pl.BlockSpec((1, KV_RANK), lambda i: (0, 0)),
            pl.BlockSpec(wq_p.shape, lambda i: (0, 0)),
            pl.BlockSpec(wkv.shape, lambda i: (0, 0)),
        ],
        out_specs=[
            head_spec(QK_DIM), head_spec(QK_DIM), head_spec(V_HEAD),
            pl.BlockSpec((TM, KV_RANK), lambda i: (i, 0)),
            pl.BlockSpec((TM, QK_ROPE), lambda i: (i, 0)),
        ],
        out_shape=[
            jax.ShapeDtypeStruct((MLA_HEADS, R_ALL, QK_DIM), bf16),
            jax.ShapeDtypeStruct((MLA_HEADS, R_ALL, QK_DIM), bf16),
            jax.ShapeDtypeStruct((MLA_HEADS, R_ALL, V_HEAD), bf16),
            jax.ShapeDtypeStruct((R_ALL, KV_RANK), f32),
            jax.ShapeDtypeStruct((R_ALL, QK_ROPE), f32),
        ],
        compiler_params=_params(("arbitrary",)),
        name="mla_proj",
    )(proj, proj, proj, q_a_norm, kv_a_norm, wq_p, wkv)


N_UNITS = DEC_BATCH * N_PAGE_STEPS
Q_TILES = P_ROWS // TQ
UNITS_FIRST_BLOCK = 3


def _unit_schedule():
    fused = [(i + 1) + UNITS_FIRST_BLOCK - 1 for _ in range(MLA_HEADS) for i in range(Q_TILES)]
    left = N_UNITS - sum(fused)
    assert 0 <= left <= len(fused)
    extra = [1 if s < left else 0 for s in range(len(fused))]
    first = np.cumsum([0] + [n + e for n, e in zip(fused, extra)])[:-1]
    return np.asarray(first, np.int32), np.asarray(extra, np.int32)


def _page_copies(pt_ref, ckv_hbm, kpe_hbm, ckv_buf, kpe_buf, sems, seq, c):
    copies = []
    for p in range(PAGES_PER_STEP):
        page = pt_ref[seq, c * PAGES_PER_STEP + p]
        copies.append(pltpu.make_async_copy(ckv_hbm.at[0, page], ckv_buf.at[c, p], sems.at[0, c]))
        copies.append(pltpu.make_async_copy(kpe_hbm.at[0, page], kpe_buf.at[c, p], sems.at[1, c]))
    return copies


def _attn_kernel(pt_ref, first_ref, extra_ref,
                 q_ref, k_ref, v_ref, ql_ref, qp_ref, cn_ref, kn_ref, ckv_hbm, kpe_hbm,
                 o_ref, olat_ref,
                 m_ref, l_ref, acc_ref, ta_ref, tb_ref,
                 ckv_buf, kpe_buf, sems, ckb_ref, kpb_ref, pm_ref, pl_ref, pacc_ref):
    i = pl.program_id(1)
    step = pl.program_id(0) * Q_TILES + i
    u0 = first_ref[step]
    to_log2 = SOFTMAX_SCALE * LOG2_E
    copies = functools.partial(_page_copies, pt_ref, ckv_hbm, kpe_hbm, ckv_buf, kpe_buf, sems)

    @pl.when(step == 0)
    def _():
        pm_ref[...] = jnp.zeros(pm_ref.shape, f32)
        pl_ref[...] = jnp.zeros(pl_ref.shape, f32)
        pacc_ref[...] = jnp.zeros(pacc_ref.shape, f32)
        for c in range(N_PAGE_STEPS):
            for cp in copies(0, c):
                cp.start()

    def unit_wait(u):
        for cp in copies(u // N_PAGE_STEPS, u % N_PAGE_STEPS):
            cp.wait()

    def unit_scores(u):
        seq = u // N_PAGE_STEPS
        c = u % N_PAGE_STEPS
        ckb_ref[...] = ckv_buf[c].reshape(KEYS_PER_STEP, KV_RANK).astype(bf16)
        kpb_ref[...] = jnp.concatenate([kpe_buf[c, p] for p in range(PAGES_PER_STEP)], axis=1).astype(bf16)
        ql = ql_ref[seq]
        qp = qp_ref[seq]
        t = (_dot_nt(ql, ckb_ref[...]) + _dot(qp, kpb_ref[...])) * to_log2
        return seq, c, ql, qp, t

    def unit_softmax(seq, c, ql, qp, t):
        c_new = cn_ref[seq]
        t_new = (jnp.sum(ql.astype(f32) * c_new, axis=1, keepdims=True)
                 + jnp.sum(qp.astype(f32) * kn_ref[seq], axis=1, keepdims=True)) * to_log2
        first = (jnp.zeros((MLA_HEADS, LANES), jnp.int32) + c) == 0
        first_w = (jnp.zeros((MLA_HEADS, KV_RANK), jnp.int32) + c) == 0
        m_prev = jnp.where(first, jnp.broadcast_to(t_new, (MLA_HEADS, LANES)), pm_ref[...])
        l_prev = jnp.where(first, 1.0, pl_ref[...])
        acc_prev = jnp.where(first_w, jnp.broadcast_to(c_new, (MLA_HEADS, KV_RANK)), pacc_ref[...])
        m_next = jnp.maximum(m_prev, jnp.max(t, axis=1, keepdims=True))
        p = jnp.exp2(t - jnp.concatenate([m_next] * (KEYS_PER_STEP // LANES), axis=1))
        alpha = jnp.exp2(m_prev - m_next)
        l_next = alpha * l_prev + jnp.sum(p, axis=1, keepdims=True)
        acc = (jnp.concatenate([alpha] * (KV_RANK // LANES), axis=1) * acc_prev
               + _dot(p.astype(bf16), ckb_ref[...]))
        pm_ref[...] = m_next
        pl_ref[...] = l_next
        pacc_ref[...] = acc
        olat_ref[seq] = acc / jnp.concatenate([l_next] * (KV_RANK // LANES), axis=1)

    def unit_refill(u):
        for cp in copies((u // N_PAGE_STEPS + 1) % DEC_BATCH, u % N_PAGE_STEPS):
            cp.start()

    def unit(u):
        unit_wait(u)
        unit_softmax(*unit_scores(u))
        unit_refill(u)

    q = q_ref[0]
    m_ref[...] = jnp.full(m_ref.shape, NEG_BIG, f32)
    l_ref[...] = jnp.zeros(l_ref.shape, f32)
    acc_ref[...] = jnp.zeros(acc_ref.shape, f32)
    reps = TQ // LANES
    groups = range(0, TQ, TQ_SUB)

    def logits(t_ref, j):
        k = k_ref[0, pl.ds(pl.multiple_of(j * TQ, TQ), TQ), :]
        for r0 in groups:
            t_ref[r0:r0 + TQ_SUB] = _dot_nt(q[r0:r0 + TQ_SUB], k)

    def block(t_ref, j, masked, units, nxt=None):
        v = v_ref[0, pl.ds(pl.multiple_of(j * TQ, TQ), TQ), :]

        def row_group(r0):
            rows = slice(r0, r0 + TQ_SUB)
            t = t_ref[rows] * to_log2
            if masked:
                q_row = i * TQ + r0 + lax.broadcasted_iota(jnp.int32, (TQ_SUB, TQ), 0)
                k_row = j * TQ + lax.broadcasted_iota(jnp.int32, (TQ_SUB, TQ), 1)
                t = jnp.where(k_row <= q_row, jnp.where(k_row >= PAD, t, NEG_BIG), NEG_BIG)
            m_prev = m_ref[rows]
            m_next = jnp.maximum(m_prev, jnp.max(t, axis=1, keepdims=True))
            p = jnp.exp2(t - jnp.concatenate([m_next] * reps, axis=1))
            alpha = jnp.exp2(m_prev - m_next)
            l_ref[rows] = alpha * l_ref[rows] + jnp.sum(p, axis=1, keepdims=True)
            acc_ref[rows] = alpha * acc_ref[rows] + _dot(p.astype(bf16), v)
            m_ref[rows] = m_next

        for n in range(max(len(units), len(groups))):
            if n < len(units):
                unit_wait(u0 + units[n])
                if n == 0 and nxt is not None:
                    logits(*nxt)
                scores = unit_scores(u0 + units[n])
            if n < len(groups):
                row_group(groups[n])
            if n < len(units):
                unit_softmax(*scores)
                if n + 1 < len(units):
                    unit_refill(u0 + units[n])
        unit_refill(u0 + units[-1])

    first_units = list(range(UNITS_FIRST_BLOCK))
    unit_of = lambda j: [j + UNITS_FIRST_BLOCK - 1]

    logits(ta_ref, 0)

    @pl.when(i == 0)
    def _():
        block(ta_ref, 0, True, first_units)

    @pl.when(i > 0)
    def _():
        block(ta_ref, 0, True, first_units, nxt=(tb_ref, 1))
        n_inner = i - 1

        def pair(p, carry):
            j = 1 + 2 * p
            block(tb_ref, j, False, unit_of(j), nxt=(ta_ref, j + 1))
            block(ta_ref, j + 1, False, unit_of(j + 1), nxt=(tb_ref, j + 2))
            return carry

        lax.fori_loop(0, n_inner // 2, pair, 0)
        j = 1 + 2 * (n_inner // 2)

        @pl.when(n_inner % 2 == 1)
        def _():
            block(tb_ref, j, False, unit_of(j), nxt=(ta_ref, j + 1))
            block(ta_ref, j + 1, True, unit_of(j + 1))

        @pl.when(n_inner % 2 == 0)
        def _():
            block(tb_ref, j, True, unit_of(j))

    o_ref[...] = (acc_ref[...] / l_ref[...]).astype(bf16)

    def extra_unit(e, carry):
        unit(u0 + i + UNITS_FIRST_BLOCK + e)
        return carry

    lax.fori_loop(0, extra_ref[step], extra_unit, 0)

    @pl.when(step == pl.num_programs(0) * Q_TILES - 1)
    def _():
        for c in range(N_PAGE_STEPS):
            unit_wait(c)


def _attention(qh, kh, vh, page_table, q_lat, q_pe, ckv_new, kpe_new, cache_ckv, cache_krope_t):
    first, extra = _unit_schedule()
    whole = lambda a: pl.BlockSpec(a.shape, lambda h, i, *_: (0,) * a.ndim)
    grid_spec = pltpu.PrefetchScalarGridSpec(
        num_scalar_prefetch=3,
        grid=(MLA_HEADS, Q_TILES),
        in_specs=[
            pl.BlockSpec((1, TQ, QK_DIM), lambda h, i, *_: (h, i, 0)),
            pl.BlockSpec((1, R_ALL, QK_DIM), lambda h, i, *_: (h, 0, 0)),
            pl.BlockSpec((1, R_ALL, V_HEAD), lambda h, i, *_: (h, 0, 0)),
            whole(q_lat), whole(q_pe), whole(ckv_new), whole(kpe_new),
            pl.BlockSpec(memory_space=pl.ANY), pl.BlockSpec(memory_space=pl.ANY),
        ],
        out_specs=[
            pl.BlockSpec((TQ, V_HEAD), lambda h, i, *_: (i, h)),
            pl.BlockSpec((DEC_BATCH, MLA_HEADS, KV_RANK), lambda h, i, *_: (0, 0, 0)),
        ],
        scratch_shapes=[
            pltpu.VMEM((TQ, LANES), f32), pltpu.VMEM((TQ, LANES), f32), pltpu.VMEM((TQ, V_HEAD), f32),
            pltpu.VMEM((TQ, TQ), f32), pltpu.VMEM((TQ, TQ), f32),
            pltpu.VMEM((N_PAGE_STEPS, PAGES_PER_STEP, PAGE_SIZE, KV_RANK), f32),
            pltpu.VMEM((N_PAGE_STEPS, PAGES_PER_STEP, QK_ROPE, PAGE_SIZE), f32),
            pltpu.SemaphoreType.DMA((2, N_PAGE_STEPS)),
            pltpu.VMEM((KEYS_PER_STEP, KV_RANK), bf16),
            pltpu.VMEM((QK_ROPE, KEYS_PER_STEP), bf16),
            pltpu.VMEM((MLA_HEADS, LANES), f32), pltpu.VMEM((MLA_HEADS, LANES), f32),
            pltpu.VMEM((MLA_HEADS, KV_RANK), f32),
        ],
    )
    return pl.pallas_call(
        _attn_kernel,
        grid_spec=grid_spec,
        out_shape=[
            jax.ShapeDtypeStruct((P_ROWS, MLA_WIDTH), bf16),
            jax.ShapeDtypeStruct((DEC_BATCH, MLA_HEADS, KV_RANK), f32),
        ],
        compiler_params=_params(("arbitrary", "arbitrary")),
        name="attn",
    )(page_table, jnp.asarray(first), jnp.asarray(extra), qh, kh, vh, q_lat, q_pe, ckv_new, kpe_new,
      cache_ckv, cache_krope_t)


N_LEVELS = 7


def _hgrn_constants():
    t = np.arange(CHUNK)
    sums = (t[None, :] <= t[:, None]).astype(np.float32)
    x = t[:, None] ^ t[None, :]
    lev = np.where(x > 0, np.floor(np.log2(np.maximum(x, 1))).astype(np.int32), N_LEVELS)
    lev = np.where(t[None, :] > t[:, None], N_LEVELS + 1, lev).astype(np.int32)
    return sums, lev


def _split3(x):
    a = x.astype(bf16)
    r = x - a.astype(f32)
    b = r.astype(bf16)
    c = (r - b.astype(f32)).astype(bf16)
    return a, b, c


def _lower_bound(lb_raw_ref):
    a0 = lb_raw_ref[0:1, :]
    a1 = lb_raw_ref[1:2, :]
    m = jnp.maximum(a0, a1)
    e0 = jnp.exp(a0 - m)
    return e0 / (e0 + jnp.exp(a1 - m))


def _pair_block_reference(b, lvl):
    m = 1 << lvl
    n, w = b.shape
    if m >= SUBLANES:
        g = m // SUBLANES
        b4 = b.reshape(n // (2 * m), 2 * g, SUBLANES, w)
        return jnp.broadcast_to(b4[:, g - 1:g, SUBLANES - 1:, :], b4.shape).reshape(n, w)
    b3 = b.reshape(n // SUBLANES, SUBLANES, w)
    sub = lax.broadcasted_iota(jnp.int32, (1, SUBLANES, 1), 1)
    ref = b3[:, m - 1:m, :]
    for start in range(2 * m, SUBLANES, 2 * m):
        ref = jnp.where(sub >= start, b3[:, start + m - 1:start + m, :], ref)
    return jnp.broadcast_to(ref, b3.shape).reshape(n, w)


def _hgrn_prompt_kernel(hq_ref, hf_ref, hi_ref, hg_ref, lb_ref, on_ref, sums_ref, lev_ref,
                        o_ref, s_ref, st_ref):
    c = pl.program_id(0)

    @pl.when(c == 0)
    def _():
        st_ref[...] = jnp.zeros(st_ref.shape, f32)

    @pl.when(c == N_CHUNKS)
    def _():
        o_ref[...] = jnp.zeros(o_ref.shape, bf16)

    @pl.when(c < N_CHUNKS)
    def _():
        lb = _lower_bound(lb_ref)
        f_all = lb + (1.0 - lb) * _sigmoid(hf_ref[...])
        l1, l2, l3 = _split3(jnp.log(f_all))
        sums = sums_ref[...]
        b_all = _dot(sums, l1) + _dot(sums, l2) + _dot(sums, l3)

        row = lax.broadcasted_iota(jnp.int32, (CHUNK, 1), 0)
        lev = lev_ref[...]
        refs = [_pair_block_reference(b_all, lvl) for lvl in range(N_LEVELS)]
        for h in range(HGRN_HEADS):
            sl = slice(h * HGRN_K, (h + 1) * HGRN_K)
            q = _silu(hq_ref[:, sl])
            k = 1.0 - f_all[:, sl]
            v = hi_ref[:, sl].astype(bf16)
            b = b_all[:, sl]
            a = jnp.where(lev == N_LEVELS, _dot_nt(q.astype(bf16), k.astype(bf16)), 0.0)
            for lvl in range(N_LEVELS):
                is_q = ((row >> lvl) & 1) == 1
                d = b - refs[lvl][:, sl]
                w = (jnp.where(is_q, q, k) * jnp.exp(jnp.where(is_q, d, -d))).astype(bf16)
                a = jnp.where(lev == lvl, _dot_nt(w, w), a)

            b_suf = b[CHUNK - 1:CHUNK, :] - b
            st = st_ref[h]
            o = _dot(a.astype(bf16), v) + _dot_nt((q * jnp.exp(b)).astype(bf16), st.astype(bf16))
            st_new = st * jnp.exp(b[CHUNK - 1:CHUNK, :]) + _dot_tn(v, (k * jnp.exp(b_suf)).astype(bf16))
            st_ref[h] = st_new
            o_ref[:, sl] = (_rms(o, on_ref[...]) * _silu(hg_ref[:, sl])).astype(bf16)

    @pl.when(c == N_CHUNKS - 1)
    def _():
        for h in range(HGRN_HEADS):
            s_ref[h] = st_ref[h].T


def _hgrn_prompt(proj, lb_raw, o_norm):
    sums, lev = _hgrn_constants()
    blk = lambda part: pl.BlockSpec((CHUNK, HGRN_WIDTH), lambda c: (c, part))
    return pl.pallas_call(
        _hgrn_prompt_kernel,
        grid=(N_CHUNKS + DEC_BATCH // CHUNK,),
        in_specs=[
            blk(0), blk(1), blk(2), blk(3),
            pl.BlockSpec((2, HGRN_WIDTH), lambda c: (0, 0)),
            pl.BlockSpec((1, HGRN_V), lambda c: (0, 0)),
            pl.BlockSpec(sums.shape, lambda c: (0, 0)),
            pl.BlockSpec(lev.shape, lambda c: (0, 0)),
        ],
        out_specs=[
            pl.BlockSpec((CHUNK, HGRN_WIDTH), lambda c: (c, 0)),
            pl.BlockSpec((HGRN_HEADS, HGRN_K, HGRN_V), lambda c: (0, 0, 0)),
        ],
        out_shape=[
            jax.ShapeDtypeStruct((R_ALL, HGRN_WIDTH), bf16),
            jax.ShapeDtypeStruct((HGRN_HEADS, HGRN_K, HGRN_V), f32),
        ],
        scratch_shapes=[pltpu.VMEM((HGRN_HEADS, HGRN_V, HGRN_K), f32)],
        compiler_params=_params(("arbitrary",)),
        name="hgrn_prompt",
    )(proj, proj, proj, proj, lb_raw, o_norm, jnp.asarray(sums, bf16), jnp.asarray(lev))


def _hgrn_sample_kernel(hq_ref, hf_ref, hi_ref, hg_ref, lb_ref, on_ref, s0_ref, oh_hbm, o_ref, s_ref):
    del oh_hbm
    lb = _lower_bound(lb_ref)
    f = lb + (1.0 - lb) * _sigmoid(hf_ref[...])
    q = _silu(hq_ref[...])
    k = (1.0 - f).astype(bf16).astype(f32)
    v = hi_ref[...].astype(bf16).astype(f32)
    gate = _silu(hg_ref[...])
    f1 = f.astype(bf16).astype(f32)
    f2 = (f - f1).astype(bf16).astype(f32)
    f3 = ((f - f1) - f2).astype(bf16).astype(f32)

    n_rows = 16
    rid = lax.broadcasted_iota(jnp.int32, (n_rows, HGRN_K), 0)
    rid2 = lax.broadcasted_iota(jnp.int32, (n_rows, 2 * HGRN_V), 0)
    left = lax.broadcasted_iota(jnp.int32, (n_rows, 2 * HGRN_V), 1) < HGRN_V
    zero_v = jnp.zeros((1, HGRN_V), f32)

    at = lambda x, s, h: x[s:s + 1, h * HGRN_K:(h + 1) * HGRN_K]

    def mix(s, h):
        lhs = jnp.where(rid == 0, at(f1, s, h), jnp.where(rid == 1, at(f2, s, h), jnp.where(
            rid == 2, at(f3, s, h), jnp.where(rid == 3, at(k, s, h), 0.0))))
        rhs = jnp.where(rid2 < 3, jnp.where(left, 1.0, 0.0),
                        jnp.where(rid2 == 3, jnp.concatenate([zero_v, at(v, s, h)], axis=1), 0.0))
        return _dot_tn(lhs.astype(bf16), rhs.astype(bf16))

    outs = []
    for s0 in range(0, HGRN_SEQS, HGRN_STAGE_SEQS):
        pairs = [(s, h) for s in range(s0, s0 + HGRN_STAGE_SEQS) for h in range(HGRN_HEADS)]
        mixes = [mix(s, h) for s, h in pairs]
        for (s, h), mx in zip(pairs, mixes):
            s_ref[s, h] = mx[:, :HGRN_V] * s0_ref[s, h] + mx[:, HGRN_V:]
        o_raw = [_dot(jnp.where(rid == 0, at(q, s, h), 0.0).astype(bf16), s_ref[s, h].astype(bf16))[0:1]
                 for s, h in pairs]
        outs += [_rms(o, on_ref[...]) * at(gate, s, h) for (s, h), o in zip(pairs, o_raw)]
    o_ref[...] = jnp.concatenate(
        [jnp.concatenate(outs[s * HGRN_HEADS:(s + 1) * HGRN_HEADS], axis=1) for s in range(HGRN_SEQS)],
        axis=0).astype(bf16)


def _hgrn_sample(proj, lb_raw, o_norm, state, o_h):
    row_blk = P_ROWS // HGRN_SEQS
    blk = lambda part: pl.BlockSpec((HGRN_SEQS, HGRN_WIDTH), lambda b: (row_blk + b, part))
    state_spec = pl.BlockSpec((HGRN_SEQS, HGRN_HEADS, HGRN_K, HGRN_V), lambda b: (b, 0, 0, 0))
    return pl.pallas_call(
        _hgrn_sample_kernel,
        grid=(DEC_BATCH // HGRN_SEQS,),
        in_specs=[
            blk(0), blk(1), blk(2), blk(3),
            pl.BlockSpec((2, HGRN_WIDTH), lambda b: (0, 0)),
            pl.BlockSpec((1, HGRN_V), lambda b: (0, 0)),
            state_spec,
            pl.BlockSpec(memory_space=pl.ANY),
        ],
        out_specs=[blk(0), state_spec],
        out_shape=[
            jax.ShapeDtypeStruct(o_h.shape, o_h.dtype),
            jax.ShapeDtypeStruct(state.shape, f32),
        ],
        input_output_aliases={7: 0},
        compiler_params=_params(("arbitrary",)),
        name="hgrn_sample",
    )(proj, proj, proj, proj, lb_raw, o_norm, state, o_h)


def _q_latent_kernel(q_ref, wuk_ref, o_ref):
    o_ref[0] = _dot_nt(q_ref[0][:, :QK_NOPE], wuk_ref[...]).astype(bf16)


def _q_latent(qh, wkv):
    row_blk = P_ROWS // DEC_BATCH
    return pl.pallas_call(
        _q_latent_kernel,
        grid=(MLA_HEADS,),
        in_specs=[
            pl.BlockSpec((1, DEC_BATCH, QK_DIM), lambda h: (h, row_blk, 0)),
            pl.BlockSpec((KV_RANK, QK_NOPE), lambda h: (0, 2 * h)),
        ],
        out_specs=pl.BlockSpec((1, DEC_BATCH, KV_RANK), lambda h: (h, 0, 0)),
        out_shape=jax.ShapeDtypeStruct((MLA_HEADS, DEC_BATCH, KV_RANK), bf16),
        compiler_params=_params(("arbitrary",)),
        name="q_latent",
    )(qh, wkv)


def _v_up_kernel(o_ref, wuv_ref, out_ref):
    out_ref[...] = _dot(o_ref[0].astype(bf16), wuv_ref[...]).astype(bf16)


def _v_up(o_lat_h, wkv):
    return pl.pallas_call(
        _v_up_kernel,
        grid=(MLA_HEADS,),
        in_specs=[
            pl.BlockSpec((1, DEC_BATCH, KV_RANK), lambda h: (h, 0, 0)),
            pl.BlockSpec((KV_RANK, V_HEAD), lambda h: (0, 2 * h + 1)),
        ],
        out_specs=pl.BlockSpec((DEC_BATCH, V_HEAD), lambda h: (0, h)),
        out_shape=jax.ShapeDtypeStruct((DEC_BATCH, MLA_WIDTH), bf16),
        compiler_params=_params(("arbitrary",)),
        name="v_up",
    )(o_lat_h, wkv)


def _out_proj_kernel(oh_ref, oa_ref, x_ref, wh_ref, wa_ref, g_ref, o_ref):
    mix = _dot(oh_ref[...], wh_ref[...]) + _dot(oa_ref[...], wa_ref[...])
    o_ref[...] = x_ref[...] + _rms(mix, g_ref[...])


def _out_proj(o_h, o_a, x_all, w_out_b, g):
    return pl.pallas_call(
        _out_proj_kernel,
        grid=(R_ALL // TM_OUT,),
        in_specs=[
            pl.BlockSpec((TM_OUT, HGRN_WIDTH), lambda i: (i, 0)),
            pl.BlockSpec((TM_OUT, MLA_WIDTH), lambda i: (i, 0)),
            pl.BlockSpec((TM_OUT, D_MODEL), lambda i: (i, 0)),
            pl.BlockSpec((HGRN_WIDTH, D_MODEL), lambda i: (0, 0)),
            pl.BlockSpec((MLA_WIDTH, D_MODEL), lambda i: (1, 0)),
            pl.BlockSpec((1, D_MODEL), lambda i: (0, 0)),
        ],
        out_specs=pl.BlockSpec((TM_OUT, D_MODEL), lambda i: (i, 0)),
        out_shape=jax.ShapeDtypeStruct((R_ALL, D_MODEL), f32),
        compiler_params=_params(("arbitrary",)),
        name="out_proj",
    )(o_h, o_a, x_all, w_out_b, w_out_b, g)


def _ffn_kernel(x_ref, gpre_ref, gpost_ref, wg_ref, wv_ref, cw_ref, cb_ref, wd_ref, buf0_ref, buf1_ref,
                o_ref, tail_ref, h_ref, act_ref, carry_ref):
    i = pl.program_id(0)
    j = pl.program_id(1)
    col = pl.ds(pl.multiple_of(j * FF_TN, FF_TN), FF_TN)

    @pl.when(j == 0)
    def _():
        h_ref[...] = _rms(x_ref[...], gpre_ref[...]).astype(bf16)
        o_ref[...] = jnp.zeros(o_ref.shape, f32)

    @pl.when(i == 0)
    def _():
        carry_ref[:, col] = jnp.zeros((8, FF_TN), f32)

    h = h_ref[...]
    row = lax.broadcasted_iota(jnp.int32, (TM, 1), 0)
    g = jnp.where(row + i * TM >= PAD, _dot(h, wg_ref[...]), 0.0)
    val = _dot(h, wv_ref[...])
    prev = carry_ref[:, col]
    g1 = jnp.where(row == 0, prev[7:8], pltpu.roll(g, 1, 0))
    g2 = jnp.where(row == 0, prev[6:7], jnp.where(row == 1, prev[7:8], pltpu.roll(g, 2, 0)))
    w0, w1, w2 = cw_ref[0:1, :], cw_ref[1:2, :], cw_ref[2:3, :]
    y = cb_ref[...] + w0 * g2 + w1 * g1 + w2 * g
    act_ref[...] = (_silu(y) * val).astype(bf16)
    carry_ref[:, col] = g[TM - 8:, :]
    tail_ref[0] = g[TM - TAIL_ROWS:, :]

    @pl.when(i == pl.num_programs(0) - 1)
    def _():
        ys = cb_ref[...] + w0 * buf0_ref[...] + w1 * buf1_ref[...] + w2 * g[SAMPLE_LOCAL:, :]
        act_ref[SAMPLE_LOCAL:, :] = (_silu(ys) * val[SAMPLE_LOCAL:, :]).astype(bf16)

    o_ref[...] += _dot(act_ref[...], wd_ref[...])

    @pl.when(j == pl.num_programs(1) - 1)
    def _():
        o_ref[...] = x_ref[...] + _rms(o_ref[...], gpost_ref[...])


def _ffn(x1, g_pre, g_post, w_up_b, conv_w, conv_b, w_down_b, buf0, buf1):
    n_ff = D_FF // FF_TN
    return pl.pallas_call(
        _ffn_kernel,
        grid=(R_ALL // TM, n_ff),
        in_specs=[
            pl.BlockSpec((TM, D_MODEL), lambda i, j: (i, 0)),
            pl.BlockSpec((1, D_MODEL), lambda i, j: (0, 0)),
            pl.BlockSpec((1, D_MODEL), lambda i, j: (0, 0)),
            pl.BlockSpec((D_MODEL, FF_TN), lambda i, j: (0, j)),
            pl.BlockSpec((D_MODEL, FF_TN), lambda i, j: (0, j + n_ff)),
            pl.BlockSpec((3, FF_TN), lambda i, j: (0, j)),
            pl.BlockSpec((1, FF_TN), lambda i, j: (0, j)),
            pl.BlockSpec((FF_TN, D_MODEL), lambda i, j: (j, 0)),
            pl.BlockSpec((DEC_BATCH, FF_TN), lambda i, j: (0, j)),
            pl.BlockSpec((DEC_BATCH, FF_TN), lambda i, j: (0, j)),
        ],
        out_specs=[
            pl.BlockSpec((TM, D_MODEL), lambda i, j: (i, 0)),
            pl.BlockSpec((1, TAIL_ROWS, FF_TN), lambda i, j: (i, 0, j)),
        ],
        out_shape=[
            jax.ShapeDtypeStruct((R_ALL, D_MODEL), f32),
            jax.ShapeDtypeStruct((R_ALL // TM, TAIL_ROWS, D_FF), f32),
        ],
        scratch_shapes=[
            pltpu.VMEM((TM, D_MODEL), bf16),
            pltpu.VMEM((TM, FF_TN), bf16),
            pltpu.VMEM((8, D_FF), f32),
        ],
        compiler_params=_params(("arbitrary", "arbitrary")),
        name="conv_ffn",
    )(x1, g_pre, g_post, w_up_b, w_up_b, conv_w, conv_b, w_down_b, buf0, buf1)


def kernel(x_prompt, x_sample, cache_ckv, cache_krope, state_hgrn, state_conv, page_table, meta_tokens,
           lb_raw, g_mix_pre, g_mix_post, g_ffn_pre, g_ffn_post, w_in, hgrn_o_norm, q_a_norm, kv_a_norm,
           w_q_up, w_kv_up, w_out, w_ffn_up, conv_w, conv_b, w_ffn_down):
    x_all = jnp.concatenate([jnp.zeros((PAD, D_MODEL), f32), meta_tokens.astype(f32), x_prompt[0],
                             x_sample[:, 0]], axis=0)

    w_in_p = jnp.pad(w_in[0], ((0, 0), (0, IN_PAD - IN_TOTAL))).astype(bf16)
    wq = w_q_up[0].reshape(Q_RANK, MLA_HEADS, QK_DIM)
    wq_p = jnp.concatenate([wq[:, :, :QK_NOPE].reshape(Q_RANK, -1), wq[:, :, QK_NOPE:].reshape(Q_RANK, -1)],
                           axis=1).astype(bf16)
    wkv = w_kv_up[0].astype(bf16)
    w_out_b = w_out[0].astype(bf16)
    w_up_b = w_ffn_up[0].astype(bf16)
    w_down_b = w_ffn_down[0].astype(bf16)

    proj = _in_proj(x_all, g_mix_pre, w_in_p)
    qh, kh, vh, ckv, kpe = _mla_proj(proj, q_a_norm, kv_a_norm, wq_p, wkv)

    o_h, s_p = _hgrn_prompt(proj, lb_raw, hgrn_o_norm)
    o_h, s_s = _hgrn_sample(proj, lb_raw, hgrn_o_norm, state_hgrn[0], o_h)

    q_lat = _q_latent(qh, wkv).transpose(1, 0, 2)
    q_pe_s = qh[:, P_ROWS:, QK_NOPE:].transpose(1, 0, 2)
    ckv_s = ckv[P_ROWS:]
    kpe_s = kpe[P_ROWS:]
    o_a, o_lat = _attention(qh, kh, vh, page_table, q_lat, q_pe_s, ckv_s[:, None, :], kpe_s[:, None, :],
                            cache_ckv, cache_krope.transpose(0, 1, 3, 2))
    o_a = jnp.concatenate([o_a, _v_up(o_lat.transpose(1, 0, 2), wkv)], axis=0)

    x1 = _out_proj(o_h, o_a, x_all, w_out_b, g_mix_post)
    x2, tails = _ffn(x1, g_ffn_pre, g_ffn_post, w_up_b, conv_w[0], conv_b, w_down_b,
                     state_conv[0, :, 0], state_conv[0, :, 1])
    tail = tails[-1]

    n_tail = TAIL_ROWS - DEC_BATCH
    conv_p = tail[n_tail - 2:n_tail][None, None]
    conv_s = jnp.stack([state_conv[0, :, 1], tail[n_tail:]], axis=1)[None]
    return (x2[P_START:P_ROWS][None], x2[P_ROWS:][:, None],
            ckv[PAD:P_ROWS][None, None], kpe[PAD:P_ROWS][None, None],
            ckv_s[None, :, None], kpe_s[None, :, None],
            s_p[None, None], s_s[None], conv_p, conv_s)
```

```python
import functools

import numpy as np
import jax
import jax.numpy as jnp
from jax import lax
from jax.experimental import pallas as pl
from jax.experimental.pallas import tpu as pltpu

D_MODEL = 2048
SEQ = 8192
N_META = 16
DEC_BATCH = 128
PAST_LEN = 16384
PAGE_SIZE = 128
N_PAGES = PAST_LEN // PAGE_SIZE
HGRN_HEADS = 8
HGRN_K = 128
HGRN_V = 128
HGRN_WIDTH = HGRN_HEADS * HGRN_V
MLA_HEADS = 8
QK_NOPE = 128
QK_ROPE = 64
QK_DIM = QK_NOPE + QK_ROPE
V_HEAD = 128
Q_RANK = 512
KV_RANK = 256
MLA_WIDTH = MLA_HEADS * V_HEAD
ROPE_THETA = 10000.0
SOFTMAX_SCALE = QK_DIM ** -0.5
LOG2_E = 1.4426950408889634
D_FF = 5632
EPS = 1e-6
IN_TOTAL = 4 * HGRN_WIDTH + Q_RANK + KV_RANK + QK_ROPE

CHUNK = 128
PAD = CHUNK - N_META
P_START = PAD + N_META
P_ROWS = P_START + SEQ
R_ALL = P_ROWS + DEC_BATCH
N_CHUNKS = P_ROWS // CHUNK

LANES = 128
SUBLANES = 8
IN_PAD = 5120
IN_TN = 1024
TM = 768
TM_OUT = 384
TQ = 640
TQ_SUB = 320
FF_TN = 512
HGRN_SEQS = 16
HGRN_STAGE_SEQS = 8
SAMPLE_LOCAL = P_ROWS - (R_ALL // TM - 1) * TM
TAIL_ROWS = 136
PAGES_PER_STEP = 16
N_PAGE_STEPS = N_PAGES // PAGES_PER_STEP
KEYS_PER_STEP = PAGES_PER_STEP * PAGE_SIZE
NEG_BIG = -1e30
VMEM_LIMIT = 56 * 1024 * 1024

NT_DIMS = (((1,), (1,)), ((), ()))
TN_DIMS = (((0,), (0,)), ((), ()))

f32 = jnp.float32
bf16 = jnp.bfloat16


def _params(sem, vmem=VMEM_LIMIT):
    return pltpu.CompilerParams(dimension_semantics=sem, vmem_limit_bytes=vmem)


def _rms(x, g):
    return x * lax.rsqrt(jnp.mean(x * x, axis=-1, keepdims=True) + EPS) * g


def _sigmoid(x):
    return 1.0 / (1.0 + jnp.exp(-x))


def _silu(x):
    return x * _sigmoid(x)


def _dot(a, b):
    return jnp.dot(a, b, preferred_element_type=f32)


def _dot_nt(a, b):
    return lax.dot_general(a, b, NT_DIMS, preferred_element_type=f32)


def _dot_tn(a, b):
    return lax.dot_general(a, b, TN_DIMS, preferred_element_type=f32)


def _in_proj_kernel(x_ref, g_ref, w_ref, o_ref, h_ref):
    @pl.when(pl.program_id(1) == 0)
    def _():
        h_ref[...] = _rms(x_ref[...], g_ref[...]).astype(bf16)

    o_ref[...] = _dot_nt(h_ref[...], w_ref[...])


def _in_proj(x_all, g, w_in_t):
    return pl.pallas_call(
        _in_proj_kernel,
        grid=(R_ALL // TM, IN_PAD // IN_TN),
        in_specs=[
            pl.BlockSpec((TM, D_MODEL), lambda i, j: (i, 0)),
            pl.BlockSpec((1, D_MODEL), lambda i, j: (0, 0)),
            pl.BlockSpec((IN_TN, D_MODEL), lambda i, j: (j, 0)),
        ],
        out_specs=pl.BlockSpec((TM, IN_TN), lambda i, j: (i, j)),
        out_shape=jax.ShapeDtypeStruct((R_ALL, IN_PAD), f32),
        scratch_shapes=[pltpu.VMEM((TM, D_MODEL), bf16)],
        compiler_params=_params(("arbitrary", "arbitrary")),
        name="in_proj",
    )(x_all, g, w_in_t)


def _rope_tables(row0, n_rows):
    lane = lax.broadcasted_iota(jnp.int32, (1, LANES), 1)
    half = QK_ROPE // 2
    fidx = (lane & (half - 1)).astype(f32)
    inv = jnp.power(jnp.float32(ROPE_THETA), -fidx / half)
    rows = row0 + lax.broadcasted_iota(jnp.int32, (n_rows, 1), 0)
    pos = jnp.where(rows < P_ROWS, rows - PAD, PAST_LEN).astype(f32)
    ang = pos * inv
    sin = jnp.sin(ang)
    return jnp.cos(ang), jnp.where((lane & half) == 0, -sin, sin)


def _rotate_half(x, cos, sin_signed):
    n = x.shape[1]
    half = QK_ROPE // 2
    first_half = (lax.broadcasted_iota(jnp.int32, (1, n), 1) & half) == 0
    partner = jnp.where(first_half, pltpu.roll(x, n - half, 1), pltpu.roll(x, half, 1))
    return x * cos + partner * sin_signed


def _mla_proj_kernel(qd_ref, kvd_ref, kr_ref, qn_ref, kvn_ref, wq_ref, wkv_ref,
                     qh_ref, kh_ref, vh_ref, ckv_ref, kpe_ref):
    tm = qd_ref.shape[0]
    cos, sin_s = _rope_tables(pl.program_id(0) * tm, tm)

    q = _dot(_rms(qd_ref[...], qn_ref[...]).astype(bf16), wq_ref[...])
    reps = MLA_HEADS * QK_ROPE // LANES
    q_pe = _rotate_half(q[:, MLA_HEADS * QK_NOPE:],
                        jnp.concatenate([cos] * reps, axis=1),
                        jnp.concatenate([sin_s] * reps, axis=1))

    k_pe = _rotate_half(kr_ref[...], cos, sin_s)[:, :QK_ROPE]
    kpe_ref[...] = k_pe

    c_kv = _rms(kvd_ref[...], kvn_ref[...])
    ckv_ref[...] = c_kv
    kv = _dot(c_kv.astype(bf16), wkv_ref[...])

    k_pe_b = k_pe.astype(bf16)
    for h in range(MLA_HEADS):
        qh_ref[h, :, :QK_NOPE] = q[:, h * QK_NOPE:(h + 1) * QK_NOPE].astype(bf16)
        qh_ref[h, :, QK_NOPE:] = q_pe[:, h * QK_ROPE:(h + 1) * QK_ROPE].astype(bf16)
        base = h * (QK_NOPE + V_HEAD)
        kh_ref[h, :, :QK_NOPE] = kv[:, base:base + QK_NOPE].astype(bf16)
        kh_ref[h, :, QK_NOPE:] = k_pe_b
        vh_ref[h] = kv[:, base + QK_NOPE:base + QK_NOPE + V_HEAD].astype(bf16)


def _mla_proj(proj, q_a_norm, kv_a_norm, wq_p, wkv):
    col_q = (4 * HGRN_WIDTH) // Q_RANK
    col_kv = (4 * HGRN_WIDTH + Q_RANK) // KV_RANK
    col_kr = (4 * HGRN_WIDTH + Q_RANK + KV_RANK) // LANES
    head_spec = lambda d: pl.BlockSpec((MLA_HEADS, TM, d), lambda i: (0, i, 0))
    return pl.pallas_call(
        _mla_proj_kernel,
        grid=(R_ALL // TM,),
        in_specs=[
            pl.BlockSpec((TM, Q_RANK), lambda i: (i, col_q)),
            pl.BlockSpec((TM, KV_RANK), lambda i: (i, col_kv)),
            pl.BlockSpec((TM, LANES), lambda i: (i, col_kr)),
            pl.BlockSpec((1, Q_RANK), lambda i: (0, 0)),
            pl.BlockSpec((1, KV_RANK), lambda i: (0, 0)),
            pl.BlockSpec(wq_p.shape, lambda i: (0, 0)),
            pl.BlockSpec(wkv.shape, lambda i: (0, 0)),
        ],
        out_specs=[
            head_spec(QK_DIM), head_spec(QK_DIM), head_spec(V_HEAD),
            pl.BlockSpec((TM, KV_RANK), lambda i: (i, 0)),
            pl.BlockSpec((TM, QK_ROPE), lambda i: (i, 0)),
        ],
        out_shape=[
            jax.ShapeDtypeStruct((MLA_HEADS, R_ALL, QK_DIM), bf16),
            jax.ShapeDtypeStruct((MLA_HEADS, R_ALL, QK_DIM), bf16),
            jax.ShapeDtypeStruct((MLA_HEADS, R_ALL, V_HEAD), bf16),
            jax.ShapeDtypeStruct((R_ALL, KV_RANK), f32),
            jax.ShapeDtypeStruct((R_ALL, QK_ROPE), f32),
        ],
        compiler_params=_params(("arbitrary",)),
        name="mla_proj",
    )(proj, proj, proj, q_a_norm, kv_a_norm, wq_p, wkv)


def _attn_kernel(q_ref, k_ref, v_ref, o_ref, m_ref, l_ref, acc_ref, ta_ref, tb_ref):
    i = pl.program_id(1)
    q = q_ref[0]
    m_ref[...] = jnp.full(m_ref.shape, NEG_BIG, f32)
    l_ref[...] = jnp.zeros(l_ref.shape, f32)
    acc_ref[...] = jnp.zeros(acc_ref.shape, f32)
    reps = TQ // LANES
    groups = range(0, TQ, TQ_SUB)

    def logits(t_ref, j):
        t_ref[...] = _dot_nt(q, k_ref[0, pl.ds(pl.multiple_of(j * TQ, TQ), TQ), :])

    def block(t_ref, j, masked):
        v = v_ref[0, pl.ds(pl.multiple_of(j * TQ, TQ), TQ), :]
        for r0 in groups:
            rows = slice(r0, r0 + TQ_SUB)
            t = t_ref[rows] * (SOFTMAX_SCALE * LOG2_E)
            if masked:
                q_row = i * TQ + r0 + lax.broadcasted_iota(jnp.int32, (TQ_SUB, TQ), 0)
                k_row = j * TQ + lax.broadcasted_iota(jnp.int32, (TQ_SUB, TQ), 1)
                t = jnp.where(k_row <= q_row, jnp.where(k_row >= PAD, t, NEG_BIG), NEG_BIG)
            m_prev = m_ref[rows]
            m_next = jnp.maximum(m_prev, jnp.max(t, axis=1, keepdims=True))
            p = jnp.exp2(t - jnp.concatenate([m_next] * reps, axis=1))
            alpha = jnp.exp2(m_prev - m_next)
            l_ref[rows] = alpha * l_ref[rows] + jnp.sum(p, axis=1, keepdims=True)
            acc_ref[rows] = alpha * acc_ref[rows] + _dot(p.astype(bf16), v)
            m_ref[rows] = m_next

    logits(ta_ref, 0)

    @pl.when(i == 0)
    def _():
        block(ta_ref, 0, True)

    @pl.when(i > 0)
    def _():
        logits(tb_ref, 1)
        block(ta_ref, 0, True)
        n_inner = i - 1

        def pair(p, carry):
            j = 1 + 2 * p
            logits(ta_ref, j + 1)
            block(tb_ref, j, False)
            logits(tb_ref, j + 2)
            block(ta_ref, j + 1, False)
            return carry

        lax.fori_loop(0, n_inner // 2, pair, 0)
        j = 1 + 2 * (n_inner // 2)

        @pl.when(n_inner % 2 == 1)
        def _():
            logits(ta_ref, j + 1)
            block(tb_ref, j, False)
            block(ta_ref, j + 1, True)

        @pl.when(n_inner % 2 == 0)
        def _():
            block(tb_ref, j, True)

    o_ref[...] = (acc_ref[...] / l_ref[...]).astype(bf16)


def _attention(qh, kh, vh):
    return pl.pallas_call(
        _attn_kernel,
        grid=(MLA_HEADS, P_ROWS // TQ),
        in_specs=[
            pl.BlockSpec((1, TQ, QK_DIM), lambda h, i: (h, i, 0)),
            pl.BlockSpec((1, R_ALL, QK_DIM), lambda h, i: (h, 0, 0)),
            pl.BlockSpec((1, R_ALL, V_HEAD), lambda h, i: (h, 0, 0)),
        ],
        out_specs=pl.BlockSpec((TQ, V_HEAD), lambda h, i: (i, h)),
        out_shape=jax.ShapeDtypeStruct((P_ROWS, MLA_WIDTH), bf16),
        scratch_shapes=[pltpu.VMEM((TQ, LANES), f32), pltpu.VMEM((TQ, LANES), f32),
                        pltpu.VMEM((TQ, V_HEAD), f32),
                        pltpu.VMEM((TQ, TQ), f32), pltpu.VMEM((TQ, TQ), f32)],
        compiler_params=_params(("arbitrary", "arbitrary")),
        name="prompt_attn",
    )(qh, kh, vh)


N_LEVELS = 7


def _hgrn_constants():
    t = np.arange(CHUNK)
    sums = (t[None, :] <= t[:, None]).astype(np.float32)
    x = t[:, None] ^ t[None, :]
    lev = np.where(x > 0, np.floor(np.log2(np.maximum(x, 1))).astype(np.int32), N_LEVELS)
    lev = np.where(t[None, :] > t[:, None], N_LEVELS + 1, lev).astype(np.int32)
    return sums, lev


def _split3(x):
    a = x.astype(bf16)
    r = x - a.astype(f32)
    b = r.astype(bf16)
    c = (r - b.astype(f32)).astype(bf16)
    return a, b, c


def _lower_bound(lb_raw_ref):
    a0 = lb_raw_ref[0:1, :]
    a1 = lb_raw_ref[1:2, :]
    m = jnp.maximum(a0, a1)
    e0 = jnp.exp(a0 - m)
    return e0 / (e0 + jnp.exp(a1 - m))


def _pair_block_reference(b, lvl):
    m = 1 << lvl
    n, w = b.shape
    if m >= SUBLANES:
        g = m // SUBLANES
        b4 = b.reshape(n // (2 * m), 2 * g, SUBLANES, w)
        return jnp.broadcast_to(b4[:, g - 1:g, SUBLANES - 1:, :], b4.shape).reshape(n, w)
    b3 = b.reshape(n // SUBLANES, SUBLANES, w)
    sub = lax.broadcasted_iota(jnp.int32, (1, SUBLANES, 1), 1)
    ref = b3[:, m - 1:m, :]
    for start in range(2 * m, SUBLANES, 2 * m):
        ref = jnp.where(sub >= start, b3[:, start + m - 1:start + m, :], ref)
    return jnp.broadcast_to(ref, b3.shape).reshape(n, w)


def _hgrn_prompt_kernel(hq_ref, hf_ref, hi_ref, hg_ref, lb_ref, on_ref, sums_ref, lev_ref,
                        o_ref, s_ref, st_ref):
    c = pl.program_id(0)

    @pl.when(c == 0)
    def _():
        st_ref[...] = jnp.zeros(st_ref.shape, f32)

    @pl.when(c == N_CHUNKS)
    def _():
        o_ref[...] = jnp.zeros(o_ref.shape, bf16)

    @pl.when(c < N_CHUNKS)
    def _():
        lb = _lower_bound(lb_ref)
        f_all = lb + (1.0 - lb) * _sigmoid(hf_ref[...])
        l1, l2, l3 = _split3(jnp.log(f_all))
        sums = sums_ref[...]
        b_all = _dot(sums, l1) + _dot(sums, l2) + _dot(sums, l3)

        row = lax.broadcasted_iota(jnp.int32, (CHUNK, 1), 0)
        lev = lev_ref[...]
        refs = [_pair_block_reference(b_all, lvl) for lvl in range(N_LEVELS)]
        is_q = [((row >> lvl) & 1) == 1 for lvl in range(N_LEVELS)]
        sign = [jnp.where(m, 1.0, -1.0) for m in is_q]
        for h in range(HGRN_HEADS):
            sl = slice(h * HGRN_K, (h + 1) * HGRN_K)
            q = _silu(hq_ref[:, sl])
            k = 1.0 - f_all[:, sl]
            v = hi_ref[:, sl].astype(bf16)
            b = b_all[:, sl]
            a = jnp.where(lev == N_LEVELS, _dot_nt(q.astype(bf16), k.astype(bf16)), 0.0)
            for lvl in range(N_LEVELS):
                w = (jnp.where(is_q[lvl], q, k) * jnp.exp((b - refs[lvl][:, sl]) * sign[lvl])).astype(bf16)
                a = jnp.where(lev == lvl, _dot_nt(w, w), a)

            b_suf = b[CHUNK - 1:CHUNK, :] - b
            st = st_ref[h]
            o = _dot(a.astype(bf16), v) + _dot_nt((q * jnp.exp(b)).astype(bf16), st.astype(bf16))
            st_new = st * jnp.exp(b[CHUNK - 1:CHUNK, :]) + _dot_tn(v, (k * jnp.exp(b_suf)).astype(bf16))
            st_ref[h] = st_new
            o_ref[:, sl] = (_rms(o, on_ref[...]) * _silu(hg_ref[:, sl])).astype(bf16)

    @pl.when(c == N_CHUNKS - 1)
    def _():
        for h in range(HGRN_HEADS):
            s_ref[h] = st_ref[h].T


def _hgrn_prompt(proj, lb_raw, o_norm):
    sums, lev = _hgrn_constants()
    blk = lambda part: pl.BlockSpec((CHUNK, HGRN_WIDTH), lambda c: (c, part))
    return pl.pallas_call(
        _hgrn_prompt_kernel,
        grid=(N_CHUNKS + DEC_BATCH // CHUNK,),
        in_specs=[
            blk(0), blk(1), blk(2), blk(3),
            pl.BlockSpec((2, HGRN_WIDTH), lambda c: (0, 0)),
            pl.BlockSpec((1, HGRN_V), lambda c: (0, 0)),
            pl.BlockSpec(sums.shape, lambda c: (0, 0)),
            pl.BlockSpec(lev.shape, lambda c: (0, 0)),
        ],
        out_specs=[
            pl.BlockSpec((CHUNK, HGRN_WIDTH), lambda c: (c, 0)),
            pl.BlockSpec((HGRN_HEADS, HGRN_K, HGRN_V), lambda c: (0, 0, 0)),
        ],
        out_shape=[
            jax.ShapeDtypeStruct((R_ALL, HGRN_WIDTH), bf16),
            jax.ShapeDtypeStruct((HGRN_HEADS, HGRN_K, HGRN_V), f32),
        ],
        scratch_shapes=[pltpu.VMEM((HGRN_HEADS, HGRN_V, HGRN_K), f32)],
        compiler_params=_params(("arbitrary",)),
        name="hgrn_prompt",
    )(proj, proj, proj, proj, lb_raw, o_norm, jnp.asarray(sums, bf16), jnp.asarray(lev))


def _hgrn_sample_kernel(hq_ref, hf_ref, hi_ref, hg_ref, lb_ref, on_ref, s0_ref, oh_hbm, o_ref, s_ref):
    del oh_hbm
    lb = _lower_bound(lb_ref)
    f = lb + (1.0 - lb) * _sigmoid(hf_ref[...])
    q = _silu(hq_ref[...])
    k = (1.0 - f).astype(bf16).astype(f32)
    v = hi_ref[...].astype(bf16).astype(f32)
    gate = _silu(hg_ref[...])
    f1 = f.astype(bf16).astype(f32)
    f2 = (f - f1).astype(bf16).astype(f32)
    f3 = ((f - f1) - f2).astype(bf16).astype(f32)

    n_rows = 16
    rid = lax.broadcasted_iota(jnp.int32, (n_rows, HGRN_K), 0)
    rid2 = lax.broadcasted_iota(jnp.int32, (n_rows, 2 * HGRN_V), 0)
    left = lax.broadcasted_iota(jnp.int32, (n_rows, 2 * HGRN_V), 1) < HGRN_V
    zero_v = jnp.zeros((1, HGRN_V), f32)

    at = lambda x, s, h: x[s:s + 1, h * HGRN_K:(h + 1) * HGRN_K]

    def mix(s, h):
        lhs = jnp.where(rid == 0, at(f1, s, h), jnp.where(rid == 1, at(f2, s, h), jnp.where(
            rid == 2, at(f3, s, h), jnp.where(rid == 3, at(k, s, h), 0.0))))
        rhs = jnp.where(rid2 < 3, jnp.where(left, 1.0, 0.0),
                        jnp.where(rid2 == 3, jnp.concatenate([zero_v, at(v, s, h)], axis=1), 0.0))
        return _dot_tn(lhs.astype(bf16), rhs.astype(bf16))

    outs = []
    for s0 in range(0, HGRN_SEQS, HGRN_STAGE_SEQS):
        pairs = [(s, h) for s in range(s0, s0 + HGRN_STAGE_SEQS) for h in range(HGRN_HEADS)]
        mixes = [mix(s, h) for s, h in pairs]
        for (s, h), mx in zip(pairs, mixes):
            s_ref[s, h] = mx[:, :HGRN_V] * s0_ref[s, h] + mx[:, HGRN_V:]
        o_raw = [_dot(jnp.where(rid == 0, at(q, s, h), 0.0).astype(bf16), s_ref[s, h].astype(bf16))[0:1]
                 for s, h in pairs]
        outs += [_rms(o, on_ref[...]) * at(gate, s, h) for (s, h), o in zip(pairs, o_raw)]
    o_ref[...] = jnp.concatenate(
        [jnp.concatenate(outs[s * HGRN_HEADS:(s + 1) * HGRN_HEADS], axis=1) for s in range(HGRN_SEQS)],
        axis=0).astype(bf16)


def _hgrn_sample(proj, lb_raw, o_norm, state, o_h):
    row_blk = P_ROWS // HGRN_SEQS
    blk = lambda part: pl.BlockSpec((HGRN_SEQS, HGRN_WIDTH), lambda b: (row_blk + b, part))
    state_spec = pl.BlockSpec((HGRN_SEQS, HGRN_HEADS, HGRN_K, HGRN_V), lambda b: (b, 0, 0, 0))
    return pl.pallas_call(
        _hgrn_sample_kernel,
        grid=(DEC_BATCH // HGRN_SEQS,),
        in_specs=[
            blk(0), blk(1), blk(2), blk(3),
            pl.BlockSpec((2, HGRN_WIDTH), lambda b: (0, 0)),
            pl.BlockSpec((1, HGRN_V), lambda b: (0, 0)),
            state_spec,
            pl.BlockSpec(memory_space=pl.ANY),
        ],
        out_specs=[blk(0), state_spec],
        out_shape=[
            jax.ShapeDtypeStruct(o_h.shape, o_h.dtype),
            jax.ShapeDtypeStruct(state.shape, f32),
        ],
        input_output_aliases={7: 0},
        compiler_params=_params(("arbitrary",)),
        name="hgrn_sample",
    )(proj, proj, proj, proj, lb_raw, o_norm, state, o_h)


def _q_latent_kernel(q_ref, wuk_ref, o_ref):
    o_ref[0] = _dot_nt(q_ref[0][:, :QK_NOPE], wuk_ref[...]).astype(bf16)


def _q_latent(qh, wkv):
    row_blk = P_ROWS // DEC_BATCH
    return pl.pallas_call(
        _q_latent_kernel,
        grid=(MLA_HEADS,),
        in_specs=[
            pl.BlockSpec((1, DEC_BATCH, QK_DIM), lambda h: (h, row_blk, 0)),
            pl.BlockSpec((KV_RANK, QK_NOPE), lambda h: (0, 2 * h)),
        ],
        out_specs=pl.BlockSpec((1, DEC_BATCH, KV_RANK), lambda h: (h, 0, 0)),
        out_shape=jax.ShapeDtypeStruct((MLA_HEADS, DEC_BATCH, KV_RANK), bf16),
        compiler_params=_params(("arbitrary",)),
        name="q_latent",
    )(qh, wkv)


def _page_copies(pt_ref, ckv_hbm, kpe_hbm, ckv_buf, kpe_buf, sems, seq, c):
    copies = []
    for p in range(PAGES_PER_STEP):
        page = pt_ref[seq, c * PAGES_PER_STEP + p]
        copies.append(pltpu.make_async_copy(ckv_hbm.at[0, page], ckv_buf.at[c, p], sems.at[0, c]))
        copies.append(pltpu.make_async_copy(kpe_hbm.at[0, page], kpe_buf.at[c, p], sems.at[1, c]))
    return copies


def _paged_attn_kernel(pt_ref, ql_ref, qp_ref, cn_ref, kn_ref, ckv_hbm, kpe_hbm, o_ref,
                       ckv_buf, kpe_buf, sems, ckb_ref, kpb_ref):
    b = pl.program_id(0)
    copies = functools.partial(_page_copies, pt_ref, ckv_hbm, kpe_hbm, ckv_buf, kpe_buf, sems)
    to_log2 = SOFTMAX_SCALE * LOG2_E

    @pl.when(b == 0)
    def _():
        for c in range(N_PAGE_STEPS):
            for cp in copies(b, c):
                cp.start()

    ql = ql_ref[0]
    qp = qp_ref[0]

    def scores(c):
        for cp in copies(b, c):
            cp.wait()
        half = c % 2
        ckb_ref[half] = ckv_buf[c].reshape(KEYS_PER_STEP, KV_RANK).astype(bf16)
        kpb_ref[half] = jnp.concatenate([kpe_buf[c, p] for p in range(PAGES_PER_STEP)], axis=1).astype(bf16)
        return (_dot_nt(ql, ckb_ref[half]) + _dot(qp, kpb_ref[half])) * to_log2

    t_new = (jnp.sum(ql.astype(f32) * cn_ref[0], axis=1, keepdims=True)
             + jnp.sum(qp.astype(f32) * kn_ref[0], axis=1, keepdims=True)) * to_log2
    m = jnp.broadcast_to(t_new, (MLA_HEADS, LANES))
    l = jnp.ones((MLA_HEADS, LANES), f32)
    acc = jnp.broadcast_to(cn_ref[0], (MLA_HEADS, KV_RANK))

    t = scores(0)
    for c in range(N_PAGE_STEPS):
        t_next = scores(c + 1) if c + 1 < N_PAGE_STEPS else None
        m_next = jnp.maximum(m, jnp.max(t, axis=1, keepdims=True))
        p = jnp.exp2(t - jnp.concatenate([m_next] * (KEYS_PER_STEP // LANES), axis=1))
        alpha = jnp.exp2(m - m_next)
        l = alpha * l + jnp.sum(p, axis=1, keepdims=True)
        acc = jnp.concatenate([alpha] * (KV_RANK // LANES), axis=1) * acc + _dot(p.astype(bf16), ckb_ref[c % 2])
        m = m_next

        @pl.when(b + 1 < pl.num_programs(0))
        def _():
            for cp in copies(b + 1, c):
                cp.start()

        t = t_next

    o_ref[0] = acc / jnp.concatenate([l] * (KV_RANK // LANES), axis=1)


def _paged_attention(page_table, q_lat, q_pe, ckv_new, kpe_new, cache_ckv, cache_krope_t):
    seq_spec = lambda r, d: pl.BlockSpec((1, r, d), lambda b, pt: (b, 0, 0))
    grid_spec = pltpu.PrefetchScalarGridSpec(
        num_scalar_prefetch=1,
        grid=(DEC_BATCH,),
        in_specs=[
            seq_spec(MLA_HEADS, KV_RANK), seq_spec(MLA_HEADS, QK_ROPE),
            seq_spec(1, KV_RANK), seq_spec(1, QK_ROPE),
            pl.BlockSpec(memory_space=pl.ANY), pl.BlockSpec(memory_space=pl.ANY),
        ],
        out_specs=seq_spec(MLA_HEADS, KV_RANK),
        scratch_shapes=[
            pltpu.VMEM((N_PAGE_STEPS, PAGES_PER_STEP, PAGE_SIZE, KV_RANK), f32),
            pltpu.VMEM((N_PAGE_STEPS, PAGES_PER_STEP, QK_ROPE, PAGE_SIZE), f32),
            pltpu.SemaphoreType.DMA((2, N_PAGE_STEPS)),
            pltpu.VMEM((2, KEYS_PER_STEP, KV_RANK), bf16),
            pltpu.VMEM((2, QK_ROPE, KEYS_PER_STEP), bf16),
        ],
    )
    return pl.pallas_call(
        _paged_attn_kernel,
        grid_spec=grid_spec,
        out_shape=jax.ShapeDtypeStruct((DEC_BATCH, MLA_HEADS, KV_RANK), f32),
        compiler_params=_params(("arbitrary",)),
        name="paged_attn",
    )(page_table, q_lat, q_pe, ckv_new, kpe_new, cache_ckv, cache_krope_t)


def _v_up_kernel(o_ref, wuv_ref, out_ref):
    out_ref[...] = _dot(o_ref[0].astype(bf16), wuv_ref[...]).astype(bf16)


def _v_up(o_lat_h, wkv):
    return pl.pallas_call(
        _v_up_kernel,
        grid=(MLA_HEADS,),
        in_specs=[
            pl.BlockSpec((1, DEC_BATCH, KV_RANK), lambda h: (h, 0, 0)),
            pl.BlockSpec((KV_RANK, V_HEAD), lambda h: (0, 2 * h + 1)),
        ],
        out_specs=pl.BlockSpec((DEC_BATCH, V_HEAD), lambda h: (0, h)),
        out_shape=jax.ShapeDtypeStruct((DEC_BATCH, MLA_WIDTH), bf16),
        compiler_params=_params(("arbitrary",)),
        name="v_up",
    )(o_lat_h, wkv)


def _out_proj_kernel(oh_ref, oa_ref, x_ref, wh_ref, wa_ref, g_ref, o_ref):
    mix = _dot(oh_ref[...], wh_ref[...]) + _dot(oa_ref[...], wa_ref[...])
    o_ref[...] = x_ref[...] + _rms(mix, g_ref[...])


def _out_proj(o_h, o_a, x_all, w_out_b, g):
    return pl.pallas_call(
        _out_proj_kernel,
        grid=(R_ALL // TM_OUT,),
        in_specs=[
            pl.BlockSpec((TM_OUT, HGRN_WIDTH), lambda i: (i, 0)),
            pl.BlockSpec((TM_OUT, MLA_WIDTH), lambda i: (i, 0)),
            pl.BlockSpec((TM_OUT, D_MODEL), lambda i: (i, 0)),
            pl.BlockSpec((HGRN_WIDTH, D_MODEL), lambda i: (0, 0)),
            pl.BlockSpec((MLA_WIDTH, D_MODEL), lambda i: (1, 0)),
            pl.BlockSpec((1, D_MODEL), lambda i: (0, 0)),
        ],
        out_specs=pl.BlockSpec((TM_OUT, D_MODEL), lambda i: (i, 0)),
        out_shape=jax.ShapeDtypeStruct((R_ALL, D_MODEL), f32),
        compiler_params=_params(("arbitrary",)),
        name="out_proj",
    )(o_h, o_a, x_all, w_out_b, w_out_b, g)


def _ffn_kernel(x_ref, gpre_ref, gpost_ref, wg_ref, wv_ref, cw_ref, cb_ref, wd_ref, buf0_ref, buf1_ref,
                o_ref, tail_ref, h_ref, act_ref, carry_ref):
    i = pl.program_id(0)
    j = pl.program_id(1)
    col = pl.ds(pl.multiple_of(j * FF_TN, FF_TN), FF_TN)

    @pl.when(j == 0)
    def _():
        h_ref[...] = _rms(x_ref[...], gpre_ref[...]).astype(bf16)
        o_ref[...] = jnp.zeros(o_ref.shape, f32)

    @pl.when(i == 0)
    def _():
        carry_ref[:, col] = jnp.zeros((8, FF_TN), f32)

    h = h_ref[...]
    row = lax.broadcasted_iota(jnp.int32, (TM, 1), 0)
    g = jnp.where(row + i * TM >= PAD, _dot(h, wg_ref[...]), 0.0)
    val = _dot(h, wv_ref[...])
    prev = carry_ref[:, col]
    g1 = jnp.where(row == 0, prev[7:8], pltpu.roll(g, 1, 0))
    g2 = jnp.where(row == 0, prev[6:7], jnp.where(row == 1, prev[7:8], pltpu.roll(g, 2, 0)))
    w0, w1, w2 = cw_ref[0:1, :], cw_ref[1:2, :], cw_ref[2:3, :]
    y = cb_ref[...] + w0 * g2 + w1 * g1 + w2 * g
    act_ref[...] = (_silu(y) * val).astype(bf16)
    carry_ref[:, col] = g[TM - 8:, :]
    tail_ref[0] = g[TM - TAIL_ROWS:, :]

    @pl.when(i == pl.num_programs(0) - 1)
    def _():
        ys = cb_ref[...] + w0 * buf0_ref[...] + w1 * buf1_ref[...] + w2 * g[SAMPLE_LOCAL:, :]
        act_ref[SAMPLE_LOCAL:, :] = (_silu(ys) * val[SAMPLE_LOCAL:, :]).astype(bf16)

    o_ref[...] += _dot(act_ref[...], wd_ref[...])

    @pl.when(j == pl.num_programs(1) - 1)
    def _():
        o_ref[...] = x_ref[...] + _rms(o_ref[...], gpost_ref[...])


def _ffn(x1, g_pre, g_post, w_up_b, conv_w, conv_b, w_down_b, buf0, buf1):
    n_ff = D_FF // FF_TN
    return pl.pallas_call(
        _ffn_kernel,
        grid=(R_ALL // TM, n_ff),
        in_specs=[
            pl.BlockSpec((TM, D_MODEL), lambda i, j: (i, 0)),
            pl.BlockSpec((1, D_MODEL), lambda i, j: (0, 0)),
            pl.BlockSpec((1, D_MODEL), lambda i, j: (0, 0)),
            pl.BlockSpec((D_MODEL, FF_TN), lambda i, j: (0, j)),
            pl.BlockSpec((D_MODEL, FF_TN), lambda i, j: (0, j + n_ff)),
            pl.BlockSpec((3, FF_TN), lambda i, j: (0, j)),
            pl.BlockSpec((1, FF_TN), lambda i, j: (0, j)),
            pl.BlockSpec((FF_TN, D_MODEL), lambda i, j: (j, 0)),
            pl.BlockSpec((DEC_BATCH, FF_TN), lambda i, j: (0, j)),
            pl.BlockSpec((DEC_BATCH, FF_TN), lambda i, j: (0, j)),
        ],
        out_specs=[
            pl.BlockSpec((TM, D_MODEL), lambda i, j: (i, 0)),
            pl.BlockSpec((1, TAIL_ROWS, FF_TN), lambda i, j: (i, 0, j)),
        ],
        out_shape=[
            jax.ShapeDtypeStruct((R_ALL, D_MODEL), f32),
            jax.ShapeDtypeStruct((R_ALL // TM, TAIL_ROWS, D_FF), f32),
        ],
        scratch_shapes=[
            pltpu.VMEM((TM, D_MODEL), bf16),
            pltpu.VMEM((TM, FF_TN), bf16),
            pltpu.VMEM((8, D_FF), f32),
        ],
        compiler_params=_params(("arbitrary", "arbitrary")),
        name="conv_ffn",
    )(x1, g_pre, g_post, w_up_b, w_up_b, conv_w, conv_b, w_down_b, buf0, buf1)


def kernel(x_prompt, x_sample, cache_ckv, cache_krope, state_hgrn, state_conv, page_table, meta_tokens,
           lb_raw, g_mix_pre, g_mix_post, g_ffn_pre, g_ffn_post, w_in, hgrn_o_norm, q_a_norm, kv_a_norm,
           w_q_up, w_kv_up, w_out, w_ffn_up, conv_w, conv_b, w_ffn_down):
    x_all = jnp.concatenate([jnp.zeros((PAD, D_MODEL), f32), meta_tokens.astype(f32), x_prompt[0],
                             x_sample[:, 0]], axis=0)

    w_in_t = jnp.pad(w_in[0].T, ((0, IN_PAD - IN_TOTAL), (0, 0))).astype(bf16)
    wq = w_q_up[0].reshape(Q_RANK, MLA_HEADS, QK_DIM)
    wq_p = jnp.concatenate([wq[:, :, :QK_NOPE].reshape(Q_RANK, -1), wq[:, :, QK_NOPE:].reshape(Q_RANK, -1)],
                           axis=1).astype(bf16)
    wkv = w_kv_up[0].astype(bf16)
    w_out_b = w_out[0].astype(bf16)
    w_up_b = w_ffn_up[0].astype(bf16)
    w_down_b = w_ffn_down[0].astype(bf16)

    proj = _in_proj(x_all, g_mix_pre, w_in_t)
    qh, kh, vh, ckv, kpe = _mla_proj(proj, q_a_norm, kv_a_norm, wq_p, wkv)

    o_a = _attention(qh, kh, vh)
    o_h, s_p = _hgrn_prompt(proj, lb_raw, hgrn_o_norm)

    o_h, s_s = _hgrn_sample(proj, lb_raw, hgrn_o_norm, state_hgrn[0], o_h)
    q_lat = _q_latent(qh, wkv).transpose(1, 0, 2)
    q_pe_s = qh[:, P_ROWS:, QK_NOPE:].transpose(1, 0, 2)
    ckv_s = ckv[P_ROWS:]
    kpe_s = kpe[P_ROWS:]
    o_lat = _paged_attention(page_table, q_lat, q_pe_s, ckv_s[:, None, :], kpe_s[:, None, :],
                             cache_ckv, cache_krope.transpose(0, 1, 3, 2))
    o_a = jnp.concatenate([o_a, _v_up(o_lat.transpose(1, 0, 2), wkv)], axis=0)

    x1 = _out_proj(o_h, o_a, x_all, w_out_b, g_mix_post)
    x2, tails = _ffn(x1, g_ffn_pre, g_ffn_post, w_up_b, conv_w[0], conv_b, w_down_b,
                     state_conv[0, :, 0], state_conv[0, :, 1])
    tail = tails[-1]

    n_tail = TAIL_ROWS - DEC_BATCH
    conv_p = tail[n_tail - 2:n_tail][None, None]
    conv_s = jnp.stack([state_conv[0, :, 1], tail[n_tail:]], axis=1)[None]
    return (x2[P_START:P_ROWS][None], x2[P_ROWS:][:, None],
            ckv[PAD:P_ROWS][None, None], kpe[PAD:P_ROWS][None, None],
            ckv_s[None, :, None], kpe_s[None, :, None],
            s_p[None, None], s_s[None], conv_p, conv_s)
```

```python
import functools

import numpy as np
import jax
import jax.numpy as jnp
from jax import lax
from jax.experimental import pallas as pl
from jax.experimental.pallas import tpu as pltpu

D_MODEL = 2048
SEQ = 8192
N_META = 16
DEC_BATCH = 128
PAST_LEN = 16384
PAGE_SIZE = 128
N_PAGES = PAST_LEN // PAGE_SIZE
HGRN_HEADS = 8
HGRN_K = 128
HGRN_V = 128
HGRN_WIDTH = HGRN_HEADS * HGRN_V
MLA_HEADS = 8
QK_NOPE = 128
QK_ROPE = 64
QK_DIM = QK_NOPE + QK_ROPE
V_HEAD = 128
Q_RANK = 512
KV_RANK = 256
MLA_WIDTH = MLA_HEADS * V_HEAD
ROPE_THETA = 10000.0
SOFTMAX_SCALE = QK_DIM ** -0.5
LOG2_E = 1.4426950408889634
D_FF = 5632
EPS = 1e-6
IN_TOTAL = 4 * HGRN_WIDTH + Q_RANK + KV_RANK + QK_ROPE

CHUNK = 128
PAD = CHUNK - N_META
P_START = PAD + N_META
P_ROWS = P_START + SEQ
R_ALL = P_ROWS + DEC_BATCH
N_CHUNKS = P_ROWS // CHUNK

LANES = 128
SUBLANES = 8
IN_PAD = 5120
IN_TN = 1280
TM = 768
TM_OUT = 768
TQ = 640
TQ_SUB = 640
FF_TN = 512
HGRN_SEQS = 16
HGRN_STAGE_SEQS = 8
SAMPLE_LOCAL = P_ROWS - (R_ALL // TM - 1) * TM
TAIL_ROWS = 136
PAGES_PER_STEP = 16
N_PAGE_STEPS = N_PAGES // PAGES_PER_STEP
KEYS_PER_STEP = PAGES_PER_STEP * PAGE_SIZE
NEG_BIG = -1e30
VMEM_LIMIT = 56 * 1024 * 1024

NT_DIMS = (((1,), (1,)), ((), ()))
TN_DIMS = (((0,), (0,)), ((), ()))

f32 = jnp.float32
bf16 = jnp.bfloat16


def _params(sem, vmem=VMEM_LIMIT):
    return pltpu.CompilerParams(dimension_semantics=sem, vmem_limit_bytes=vmem)


def _rms(x, g):
    return x * lax.rsqrt(jnp.mean(x * x, axis=-1, keepdims=True) + EPS) * g


def _sigmoid(x):
    return 1.0 / (1.0 + jnp.exp(-x))


def _silu(x):
    return x * _sigmoid(x)


def _dot(a, b):
    return jnp.dot(a, b, preferred_element_type=f32)


def _dot_nt(a, b):
    return lax.dot_general(a, b, NT_DIMS, preferred_element_type=f32)


def _dot_tn(a, b):
    return lax.dot_general(a, b, TN_DIMS, preferred_element_type=f32)


def _in_proj_kernel(x_ref, g_ref, w_ref, o_ref, h_ref):
    @pl.when(pl.program_id(1) == 0)
    def _():
        h_ref[...] = _rms(x_ref[...], g_ref[...]).astype(bf16)

    o_ref[...] = _dot_nt(h_ref[...], w_ref[...])


def _in_proj(x_all, g, w_in_t):
    return pl.pallas_call(
        _in_proj_kernel,
        grid=(R_ALL // TM, IN_PAD // IN_TN),
        in_specs=[
            pl.BlockSpec((TM, D_MODEL), lambda i, j: (i, 0)),
            pl.BlockSpec((1, D_MODEL), lambda i, j: (0, 0)),
            pl.BlockSpec((IN_TN, D_MODEL), lambda i, j: (j, 0)),
        ],
        out_specs=pl.BlockSpec((TM, IN_TN), lambda i, j: (i, j)),
        out_shape=jax.ShapeDtypeStruct((R_ALL, IN_PAD), f32),
        scratch_shapes=[pltpu.VMEM((TM, D_MODEL), bf16)],
        compiler_params=_params(("arbitrary", "arbitrary")),
        name="in_proj",
    )(x_all, g, w_in_t)


def _rope_tables(row0, n_rows):
    lane = lax.broadcasted_iota(jnp.int32, (1, LANES), 1)
    half = QK_ROPE // 2
    fidx = (lane & (half - 1)).astype(f32)
    inv = jnp.power(jnp.float32(ROPE_THETA), -fidx / half)
    rows = row0 + lax.broadcasted_iota(jnp.int32, (n_rows, 1), 0)
    pos = jnp.where(rows < P_ROWS, rows - PAD, PAST_LEN).astype(f32)
    ang = pos * inv
    sin = jnp.sin(ang)
    return jnp.cos(ang), jnp.where((lane & half) == 0, -sin, sin)


def _rotate_half(x, cos, sin_signed):
    n = x.shape[1]
    half = QK_ROPE // 2
    first_half = (lax.broadcasted_iota(jnp.int32, (1, n), 1) & half) == 0
    partner = jnp.where(first_half, pltpu.roll(x, n - half, 1), pltpu.roll(x, half, 1))
    return x * cos + partner * sin_signed


def _mla_proj_kernel(qd_ref, kvd_ref, kr_ref, qn_ref, kvn_ref, wq_ref, wkv_ref,
                     qh_ref, kh_ref, vh_ref, ckv_ref, kpe_ref):
    tm = qd_ref.shape[0]
    cos, sin_s = _rope_tables(pl.program_id(0) * tm, tm)

    q = _dot(_rms(qd_ref[...], qn_ref[...]).astype(bf16), wq_ref[...])
    reps = MLA_HEADS * QK_ROPE // LANES
    q_pe = _rotate_half(q[:, MLA_HEADS * QK_NOPE:],
                        jnp.concatenate([cos] * reps, axis=1),
                        jnp.concatenate([sin_s] * reps, axis=1))

    k_pe = _rotate_half(kr_ref[...], cos, sin_s)[:, :QK_ROPE]
    kpe_ref[...] = k_pe

    c_kv = _rms(kvd_ref[...], kvn_ref[...])
    ckv_ref[...] = c_kv
    kv = _dot(c_kv.astype(bf16), wkv_ref[...])

    k_pe_b = k_pe.astype(bf16)
    for h in range(MLA_HEADS):
        qh_ref[h, :, :QK_NOPE] = q[:, h * QK_NOPE:(h + 1) * QK_NOPE].astype(bf16)
        qh_ref[h, :, QK_NOPE:] = q_pe[:, h * QK_ROPE:(h + 1) * QK_ROPE].astype(bf16)
        base = h * (QK_NOPE + V_HEAD)
        kh_ref[h, :, :QK_NOPE] = kv[:, base:base + QK_NOPE].astype(bf16)
        kh_ref[h, :, QK_NOPE:] = k_pe_b
        vh_ref[h] = kv[:, base + QK_NOPE:base + QK_NOPE + V_HEAD].astype(bf16)


def _mla_proj(proj, q_a_norm, kv_a_norm, wq_p, wkv):
    col_q = (4 * HGRN_WIDTH) // Q_RANK
    col_kv = (4 * HGRN_WIDTH + Q_RANK) // KV_RANK
    col_kr = (4 * HGRN_WIDTH + Q_RANK + KV_RANK) // LANES
    head_spec = lambda d: pl.BlockSpec((MLA_HEADS, TM, d), lambda i: (0, i, 0))
    return pl.pallas_call(
        _mla_proj_kernel,
        grid=(R_ALL // TM,),
        in_specs=[
            pl.BlockSpec((TM, Q_RANK), lambda i: (i, col_q)),
            pl.BlockSpec((TM, KV_RANK), lambda i: (i, col_kv)),
            pl.BlockSpec((TM, LANES), lambda i: (i, col_kr)),
            pl.BlockSpec((1, Q_RANK), lambda i: (0, 0)),
            pl.BlockSpec((1, KV_RANK), lambda i: (0, 0)),
            pl.BlockSpec(wq_p.shape, lambda i: (0, 0)),
            pl.BlockSpec(wkv.shape, lambda i: (0, 0)),
        ],
        out_specs=[
            head_spec(QK_DIM), head_spec(QK_DIM), head_spec(V_HEAD),
            pl.BlockSpec((TM, KV_RANK), lambda i: (i, 0)),
            pl.BlockSpec((TM, QK_ROPE), lambda i: (i, 0)),
        ],
        out_shape=[
            jax.ShapeDtypeStruct((MLA_HEADS, R_ALL, QK_DIM), bf16),
            jax.ShapeDtypeStruct((MLA_HEADS, R_ALL, QK_DIM), bf16),
            jax.ShapeDtypeStruct((MLA_HEADS, R_ALL, V_HEAD), bf16),
            jax.ShapeDtypeStruct((R_ALL, KV_RANK), f32),
            jax.ShapeDtypeStruct((R_ALL, QK_ROPE), f32),
        ],
        compiler_params=_params(("arbitrary",)),
        name="mla_proj",
    )(proj, proj, proj, q_a_norm, kv_a_norm, wq_p, wkv)


def _attn_kernel(q_ref, k_ref, v_ref, o_ref, m_ref, l_ref, acc_ref, ta_ref, tb_ref):
    i = pl.program_id(1)
    q = q_ref[0]
    m_ref[...] = jnp.full(m_ref.shape, NEG_BIG, f32)
    l_ref[...] = jnp.zeros(l_ref.shape, f32)
    acc_ref[...] = jnp.zeros(acc_ref.shape, f32)
    reps = TQ // LANES
    groups = range(0, TQ, TQ_SUB)

    def logits(t_ref, j):
        t_ref[...] = _dot_nt(q, k_ref[0, pl.ds(pl.multiple_of(j * TQ, TQ), TQ), :])

    def block(t_ref, j, masked):
        v = v_ref[0, pl.ds(pl.multiple_of(j * TQ, TQ), TQ), :]
        for r0 in groups:
            rows = slice(r0, r0 + TQ_SUB)
            t = t_ref[rows] * (SOFTMAX_SCALE * LOG2_E)
            if masked:
                q_row = i * TQ + r0 + lax.broadcasted_iota(jnp.int32, (TQ_SUB, TQ), 0)
                k_row = j * TQ + lax.broadcasted_iota(jnp.int32, (TQ_SUB, TQ), 1)
                t = jnp.where(k_row <= q_row, jnp.where(k_row >= PAD, t, NEG_BIG), NEG_BIG)
            m_prev = m_ref[rows]
            m_next = jnp.maximum(m_prev, jnp.max(t, axis=1, keepdims=True))
            p = jnp.exp2(t - jnp.concatenate([m_next] * reps, axis=1))
            alpha = jnp.exp2(m_prev - m_next)
            l_ref[rows] = alpha * l_ref[rows] + jnp.sum(p, axis=1, keepdims=True)
            acc_ref[rows] = alpha * acc_ref[rows] + _dot(p.astype(bf16), v)
            m_ref[rows] = m_next

    logits(ta_ref, 0)

    @pl.when(i == 0)
    def _():
        block(ta_ref, 0, True)

    @pl.when(i > 0)
    def _():
        logits(tb_ref, 1)
        block(ta_ref, 0, True)
        n_inner = i - 1

        def pair(p, carry):
            j = 1 + 2 * p
            logits(ta_ref, j + 1)
            block(tb_ref, j, False)
            logits(tb_ref, j + 2)
            block(ta_ref, j + 1, False)
            return carry

        lax.fori_loop(0, n_inner // 2, pair, 0)
        j = 1 + 2 * (n_inner // 2)

        @pl.when(n_inner % 2 == 1)
        def _():
            logits(ta_ref, j + 1)
            block(tb_ref, j, False)
            block(ta_ref, j + 1, True)

        @pl.when(n_inner % 2 == 0)
        def _():
            block(tb_ref, j, True)

    o_ref[...] = (acc_ref[...] / l_ref[...]).astype(bf16)


def _attention(qh, kh, vh):
    return pl.pallas_call(
        _attn_kernel,
        grid=(MLA_HEADS, P_ROWS // TQ),
        in_specs=[
            pl.BlockSpec((1, TQ, QK_DIM), lambda h, i: (h, i, 0)),
            pl.BlockSpec((1, R_ALL, QK_DIM), lambda h, i: (h, 0, 0)),
            pl.BlockSpec((1, R_ALL, V_HEAD), lambda h, i: (h, 0, 0)),
        ],
        out_specs=pl.BlockSpec((TQ, V_HEAD), lambda h, i: (i, h)),
        out_shape=jax.ShapeDtypeStruct((P_ROWS, MLA_WIDTH), bf16),
        scratch_shapes=[pltpu.VMEM((TQ, LANES), f32), pltpu.VMEM((TQ, LANES), f32),
                        pltpu.VMEM((TQ, V_HEAD), f32),
                        pltpu.VMEM((TQ, TQ), f32), pltpu.VMEM((TQ, TQ), f32)],
        compiler_params=_params(("arbitrary", "arbitrary")),
        name="prompt_attn",
    )(qh, kh, vh)


N_LEVELS = 7


def _hgrn_constants():
    t = np.arange(CHUNK)
    sums = (t[None, :] <= t[:, None]).astype(np.float32)
    x = t[:, None] ^ t[None, :]
    lev = np.where(x > 0, np.floor(np.log2(np.maximum(x, 1))).astype(np.int32), N_LEVELS)
    lev = np.where(t[None, :] > t[:, None], N_LEVELS + 1, lev).astype(np.int32)
    return sums, lev


def _split3(x):
    a = x.astype(bf16)
    r = x - a.astype(f32)
    b = r.astype(bf16)
    c = (r - b.astype(f32)).astype(bf16)
    return a, b, c


def _lower_bound(lb_raw_ref):
    a0 = lb_raw_ref[0:1, :]
    a1 = lb_raw_ref[1:2, :]
    m = jnp.maximum(a0, a1)
    e0 = jnp.exp(a0 - m)
    return e0 / (e0 + jnp.exp(a1 - m))


def _pair_block_reference(b, lvl):
    m = 1 << lvl
    n, w = b.shape
    if m >= SUBLANES:
        g = m // SUBLANES
        b4 = b.reshape(n // (2 * m), 2 * g, SUBLANES, w)
        return jnp.broadcast_to(b4[:, g - 1:g, SUBLANES - 1:, :], b4.shape).reshape(n, w)
    b3 = b.reshape(n // SUBLANES, SUBLANES, w)
    sub = lax.broadcasted_iota(jnp.int32, (1, SUBLANES, 1), 1)
    ref = b3[:, m - 1:m, :]
    for start in range(2 * m, SUBLANES, 2 * m):
        ref = jnp.where(sub >= start, b3[:, start + m - 1:start + m, :], ref)
    return jnp.broadcast_to(ref, b3.shape).reshape(n, w)


def _hgrn_prompt_kernel(hq_ref, hf_ref, hi_ref, hg_ref, lb_ref, on_ref, sums_ref, lev_ref,
                        o_ref, s_ref, st_ref):
    c = pl.program_id(0)

    @pl.when(c == 0)
    def _():
        st_ref[...] = jnp.zeros(st_ref.shape, f32)

    @pl.when(c == N_CHUNKS)
    def _():
        o_ref[...] = jnp.zeros(o_ref.shape, bf16)

    @pl.when(c < N_CHUNKS)
    def _():
        lb = _lower_bound(lb_ref)
        f_all = lb + (1.0 - lb) * _sigmoid(hf_ref[...])
        l1, l2, l3 = _split3(jnp.log(f_all))
        sums = sums_ref[...]
        b_all = _dot(sums, l1) + _dot(sums, l2) + _dot(sums, l3)

        row = lax.broadcasted_iota(jnp.int32, (CHUNK, 1), 0)
        lev = lev_ref[...]
        refs = [_pair_block_reference(b_all, lvl) for lvl in range(N_LEVELS)]
        is_q = [((row >> lvl) & 1) == 1 for lvl in range(N_LEVELS)]
        sign = [jnp.where(m, 1.0, -1.0) for m in is_q]
        for h in range(HGRN_HEADS):
            sl = slice(h * HGRN_K, (h + 1) * HGRN_K)
            q = _silu(hq_ref[:, sl])
            k = 1.0 - f_all[:, sl]
            v = hi_ref[:, sl].astype(bf16)
            b = b_all[:, sl]
            a = jnp.where(lev == N_LEVELS, _dot_nt(q.astype(bf16), k.astype(bf16)), 0.0)
            for lvl in range(N_LEVELS):
                w = (jnp.where(is_q[lvl], q, k) * jnp.exp((b - refs[lvl][:, sl]) * sign[lvl])).astype(bf16)
                a = jnp.where(lev == lvl, _dot_nt(w, w), a)

            b_suf = b[CHUNK - 1:CHUNK, :] - b
            st = st_ref[h]
            o = _dot(a.astype(bf16), v) + _dot_nt((q * jnp.exp(b)).astype(bf16), st.astype(bf16))
            st_new = st * jnp.exp(b[CHUNK - 1:CHUNK, :]) + _dot_tn(v, (k * jnp.exp(b_suf)).astype(bf16))
            st_ref[h] = st_new
            o_ref[:, sl] = (_rms(o, on_ref[...]) * _silu(hg_ref[:, sl])).astype(bf16)

    @pl.when(c == N_CHUNKS - 1)
    def _():
        for h in range(HGRN_HEADS):
            s_ref[h] = st_ref[h].T


def _hgrn_prompt(proj, lb_raw, o_norm):
    sums, lev = _hgrn_constants()
    blk = lambda part: pl.BlockSpec((CHUNK, HGRN_WIDTH), lambda c: (c, part))
    return pl.pallas_call(
        _hgrn_prompt_kernel,
        grid=(N_CHUNKS + DEC_BATCH // CHUNK,),
        in_specs=[
            blk(0), blk(1), blk(2), blk(3),
            pl.BlockSpec((2, HGRN_WIDTH), lambda c: (0, 0)),
            pl.BlockSpec((1, HGRN_V), lambda c: (0, 0)),
            pl.BlockSpec(sums.shape, lambda c: (0, 0)),
            pl.BlockSpec(lev.shape, lambda c: (0, 0)),
        ],
        out_specs=[
            pl.BlockSpec((CHUNK, HGRN_WIDTH), lambda c: (c, 0)),
            pl.BlockSpec((HGRN_HEADS, HGRN_K, HGRN_V), lambda c: (0, 0, 0)),
        ],
        out_shape=[
            jax.ShapeDtypeStruct((R_ALL, HGRN_WIDTH), bf16),
            jax.ShapeDtypeStruct((HGRN_HEADS, HGRN_K, HGRN_V), f32),
        ],
        scratch_shapes=[pltpu.VMEM((HGRN_HEADS, HGRN_V, HGRN_K), f32)],
        compiler_params=_params(("arbitrary",)),
        name="hgrn_prompt",
    )(proj, proj, proj, proj, lb_raw, o_norm, jnp.asarray(sums, bf16), jnp.asarray(lev))


def _hgrn_sample_kernel(hq_ref, hf_ref, hi_ref, hg_ref, lb_ref, on_ref, s0_ref, oh_hbm, o_ref, s_ref):
    del oh_hbm
    lb = _lower_bound(lb_ref)
    f = lb + (1.0 - lb) * _sigmoid(hf_ref[...])
    q = _silu(hq_ref[...])
    k = (1.0 - f).astype(bf16).astype(f32)
    v = hi_ref[...].astype(bf16).astype(f32)
    gate = _silu(hg_ref[...])
    f1 = f.astype(bf16).astype(f32)
    f2 = (f - f1).astype(bf16).astype(f32)
    f3 = ((f - f1) - f2).astype(bf16).astype(f32)

    n_rows = 16
    rid = lax.broadcasted_iota(jnp.int32, (n_rows, HGRN_K), 0)
    rid2 = lax.broadcasted_iota(jnp.int32, (n_rows, 2 * HGRN_V), 0)
    left = lax.broadcasted_iota(jnp.int32, (n_rows, 2 * HGRN_V), 1) < HGRN_V
    zero_v = jnp.zeros((1, HGRN_V), f32)

    at = lambda x, s, h: x[s:s + 1, h * HGRN_K:(h + 1) * HGRN_K]

    def mix(s, h):
        lhs = jnp.where(rid == 0, at(f1, s, h), jnp.where(rid == 1, at(f2, s, h), jnp.where(
            rid == 2, at(f3, s, h), jnp.where(rid == 3, at(k, s, h), 0.0))))
        rhs = jnp.where(rid2 < 3, jnp.where(left, 1.0, 0.0),
                        jnp.where(rid2 == 3, jnp.concatenate([zero_v, at(v, s, h)], axis=1), 0.0))
        return _dot_tn(lhs.astype(bf16), rhs.astype(bf16))

    outs = []
    for s0 in range(0, HGRN_SEQS, HGRN_STAGE_SEQS):
        pairs = [(s, h) for s in range(s0, s0 + HGRN_STAGE_SEQS) for h in range(HGRN_HEADS)]
        mixes = [mix(s, h) for s, h in pairs]
        for (s, h), mx in zip(pairs, mixes):
            s_ref[s, h] = mx[:, :HGRN_V] * s0_ref[s, h] + mx[:, HGRN_V:]
        o_raw = [_dot(jnp.where(rid == 0, at(q, s, h), 0.0).astype(bf16), s_ref[s, h].astype(bf16))[0:1]
                 for s, h in pairs]
        outs += [_rms(o, on_ref[...]) * at(gate, s, h) for (s, h), o in zip(pairs, o_raw)]
    o_ref[...] = jnp.concatenate(
        [jnp.concatenate(outs[s * HGRN_HEADS:(s + 1) * HGRN_HEADS], axis=1) for s in range(HGRN_SEQS)],
        axis=0).astype(bf16)


def _hgrn_sample(proj, lb_raw, o_norm, state, o_h):
    row_blk = P_ROWS // HGRN_SEQS
    blk = lambda part: pl.BlockSpec((HGRN_SEQS, HGRN_WIDTH), lambda b: (row_blk + b, part))
    state_spec = pl.BlockSpec((HGRN_SEQS, HGRN_HEADS, HGRN_K, HGRN_V), lambda b: (b, 0, 0, 0))
    return pl.pallas_call(
        _hgrn_sample_kernel,
        grid=(DEC_BATCH // HGRN_SEQS,),
        in_specs=[
            blk(0), blk(1), blk(2), blk(3),
            pl.BlockSpec((2, HGRN_WIDTH), lambda b: (0, 0)),
            pl.BlockSpec((1, HGRN_V), lambda b: (0, 0)),
            state_spec,
            pl.BlockSpec(memory_space=pl.ANY),
        ],
        out_specs=[blk(0), state_spec],
        out_shape=[
            jax.ShapeDtypeStruct(o_h.shape, o_h.dtype),
            jax.ShapeDtypeStruct(state.shape, f32),
        ],
        input_output_aliases={7: 0},
        compiler_params=_params(("arbitrary",)),
        name="hgrn_sample",
    )(proj, proj, proj, proj, lb_raw, o_norm, state, o_h)


def _q_latent_kernel(q_ref, wuk_ref, o_ref):
    o_ref[0] = _dot_nt(q_ref[0][:, :QK_NOPE], wuk_ref[...]).astype(bf16)


def _q_latent(qh, wkv):
    row_blk = P_ROWS // DEC_BATCH
    return pl.pallas_call(
        _q_latent_kernel,
        grid=(MLA_HEADS,),
        in_specs=[
            pl.BlockSpec((1, DEC_BATCH, QK_DIM), lambda h: (h, row_blk, 0)),
            pl.BlockSpec((KV_RANK, QK_NOPE), lambda h: (0, 2 * h)),
        ],
        out_specs=pl.BlockSpec((1, DEC_BATCH, KV_RANK), lambda h: (h, 0, 0)),
        out_shape=jax.ShapeDtypeStruct((MLA_HEADS, DEC_BATCH, KV_RANK), bf16),
        compiler_params=_params(("arbitrary",)),
        name="q_latent",
    )(qh, wkv)


def _page_copies(pt_ref, ckv_hbm, kpe_hbm, ckv_buf, kpe_buf, sems, seq, c):
    copies = []
    for p in range(PAGES_PER_STEP):
        page = pt_ref[seq, c * PAGES_PER_STEP + p]
        copies.append(pltpu.make_async_copy(ckv_hbm.at[0, page], ckv_buf.at[c, p], sems.at[0, c]))
        copies.append(pltpu.make_async_copy(kpe_hbm.at[0, page], kpe_buf.at[c, p], sems.at[1, c]))
    return copies


def _paged_attn_kernel(pt_ref, ql_ref, qp_ref, cn_ref, kn_ref, ckv_hbm, kpe_hbm, o_ref,
                       ckv_buf, kpe_buf, sems, ckb_ref, kpb_ref):
    b = pl.program_id(0)
    copies = functools.partial(_page_copies, pt_ref, ckv_hbm, kpe_hbm, ckv_buf, kpe_buf, sems)
    to_log2 = SOFTMAX_SCALE * LOG2_E

    @pl.when(b == 0)
    def _():
        for c in range(N_PAGE_STEPS):
            for cp in copies(b, c):
                cp.start()

    ql = ql_ref[0]
    qp = qp_ref[0]

    def scores(c):
        for cp in copies(b, c):
            cp.wait()
        half = c % 2
        ckb_ref[half] = ckv_buf[c].reshape(KEYS_PER_STEP, KV_RANK).astype(bf16)
        kpb_ref[half] = jnp.concatenate([kpe_buf[c, p] for p in range(PAGES_PER_STEP)], axis=1).astype(bf16)
        return (_dot_nt(ql, ckb_ref[half]) + _dot(qp, kpb_ref[half])) * to_log2

    t_new = (jnp.sum(ql.astype(f32) * cn_ref[0], axis=1, keepdims=True)
             + jnp.sum(qp.astype(f32) * kn_ref[0], axis=1, keepdims=True)) * to_log2
    m = jnp.broadcast_to(t_new, (MLA_HEADS, LANES))
    l = jnp.ones((MLA_HEADS, LANES), f32)
    acc = jnp.broadcast_to(cn_ref[0], (MLA_HEADS, KV_RANK))

    t = scores(0)
    for c in range(N_PAGE_STEPS):
        t_next = scores(c + 1) if c + 1 < N_PAGE_STEPS else None
        m_next = jnp.maximum(m, jnp.max(t, axis=1, keepdims=True))
        p = jnp.exp2(t - jnp.concatenate([m_next] * (KEYS_PER_STEP // LANES), axis=1))
        alpha = jnp.exp2(m - m_next)
        l = alpha * l + jnp.sum(p, axis=1, keepdims=True)
        acc = jnp.concatenate([alpha] * (KV_RANK // LANES), axis=1) * acc + _dot(p.astype(bf16), ckb_ref[c % 2])
        m = m_next

        @pl.when(b + 1 < pl.num_programs(0))
        def _():
            for cp in copies(b + 1, c):
                cp.start()

        t = t_next

    o_ref[0] = acc / jnp.concatenate([l] * (KV_RANK // LANES), axis=1)


def _paged_attention(page_table, q_lat, q_pe, ckv_new, kpe_new, cache_ckv, cache_krope_t):
    seq_spec = lambda r, d: pl.BlockSpec((1, r, d), lambda b, pt: (b, 0, 0))
    grid_spec = pltpu.PrefetchScalarGridSpec(
        num_scalar_prefetch=1,
        grid=(DEC_BATCH,),
        in_specs=[
            seq_spec(MLA_HEADS, KV_RANK), seq_spec(MLA_HEADS, QK_ROPE),
            seq_spec(1, KV_RANK), seq_spec(1, QK_ROPE),
            pl.BlockSpec(memory_space=pl.ANY), pl.BlockSpec(memory_space=pl.ANY),
        ],
        out_specs=seq_spec(MLA_HEADS, KV_RANK),
        scratch_shapes=[
            pltpu.VMEM((N_PAGE_STEPS, PAGES_PER_STEP, PAGE_SIZE, KV_RANK), f32),
            pltpu.VMEM((N_PAGE_STEPS, PAGES_PER_STEP, QK_ROPE, PAGE_SIZE), f32),
            pltpu.SemaphoreType.DMA((2, N_PAGE_STEPS)),
            pltpu.VMEM((2, KEYS_PER_STEP, KV_RANK), bf16),
            pltpu.VMEM((2, QK_ROPE, KEYS_PER_STEP), bf16),
        ],
    )
    return pl.pallas_call(
        _paged_attn_kernel,
        grid_spec=grid_spec,
        out_shape=jax.ShapeDtypeStruct((DEC_BATCH, MLA_HEADS, KV_RANK), f32),
        compiler_params=_params(("arbitrary",)),
        name="paged_attn",
    )(page_table, q_lat, q_pe, ckv_new, kpe_new, cache_ckv, cache_krope_t)


def _v_up_kernel(o_ref, wuv_ref, out_ref):
    out_ref[...] = _dot(o_ref[0].astype(bf16), wuv_ref[...]).astype(bf16)


def _v_up(o_lat_h, wkv):
    return pl.pallas_call(
        _v_up_kernel,
        grid=(MLA_HEADS,),
        in_specs=[
            pl.BlockSpec((1, DEC_BATCH, KV_RANK), lambda h: (h, 0, 0)),
            pl.BlockSpec((KV_RANK, V_HEAD), lambda h: (0, 2 * h + 1)),
        ],
        out_specs=pl.BlockSpec((DEC_BATCH, V_HEAD), lambda h: (0, h)),
        out_shape=jax.ShapeDtypeStruct((DEC_BATCH, MLA_WIDTH), bf16),
        compiler_params=_params(("arbitrary",)),
        name="v_up",
    )(o_lat_h, wkv)


def _out_proj_kernel(oh_ref, oa_ref, x_ref, wh_ref, wa_ref, g_ref, o_ref):
    mix = _dot(oh_ref[...], wh_ref[...]) + _dot(oa_ref[...], wa_ref[...])
    o_ref[...] = x_ref[...] + _rms(mix, g_ref[...])


def _out_proj(o_h, o_a, x_all, w_out_b, g):
    return pl.pallas_call(
        _out_proj_kernel,
        grid=(R_ALL // TM_OUT,),
        in_specs=[
            pl.BlockSpec((TM_OUT, HGRN_WIDTH), lambda i: (i, 0)),
            pl.BlockSpec((TM_OUT, MLA_WIDTH), lambda i: (i, 0)),
            pl.BlockSpec((TM_OUT, D_MODEL), lambda i: (i, 0)),
            pl.BlockSpec((HGRN_WIDTH, D_MODEL), lambda i: (0, 0)),
            pl.BlockSpec((MLA_WIDTH, D_MODEL), lambda i: (1, 0)),
            pl.BlockSpec((1, D_MODEL), lambda i: (0, 0)),
        ],
        out_specs=pl.BlockSpec((TM_OUT, D_MODEL), lambda i: (i, 0)),
        out_shape=jax.ShapeDtypeStruct((R_ALL, D_MODEL), f32),
        compiler_params=_params(("arbitrary",)),
        name="out_proj",
    )(o_h, o_a, x_all, w_out_b, w_out_b, g)


def _ffn_kernel(x_ref, gpre_ref, gpost_ref, wg_ref, wv_ref, cw_ref, cb_ref, wd_ref, buf0_ref, buf1_ref,
                o_ref, tail_ref, h_ref, act_ref, carry_ref):
    i = pl.program_id(0)
    j = pl.program_id(1)
    col = pl.ds(pl.multiple_of(j * FF_TN, FF_TN), FF_TN)

    @pl.when(j == 0)
    def _():
        h_ref[...] = _rms(x_ref[...], gpre_ref[...]).astype(bf16)
        o_ref[...] = jnp.zeros(o_ref.shape, f32)

    @pl.when(i == 0)
    def _():
        carry_ref[:, col] = jnp.zeros((8, FF_TN), f32)

    h = h_ref[...]
    row = lax.broadcasted_iota(jnp.int32, (TM, 1), 0)
    g = jnp.where(row + i * TM >= PAD, _dot(h, wg_ref[...]), 0.0)
    val = _dot(h, wv_ref[...])
    prev = carry_ref[:, col]
    g1 = jnp.where(row == 0, prev[7:8], pltpu.roll(g, 1, 0))
    g2 = jnp.where(row == 0, prev[6:7], jnp.where(row == 1, prev[7:8], pltpu.roll(g, 2, 0)))
    w0, w1, w2 = cw_ref[0:1, :], cw_ref[1:2, :], cw_ref[2:3, :]
    y = cb_ref[...] + w0 * g2 + w1 * g1 + w2 * g
    act_ref[...] = (_silu(y) * val).astype(bf16)
    carry_ref[:, col] = g[TM - 8:, :]
    tail_ref[0] = g[TM - TAIL_ROWS:, :]

    @pl.when(i == pl.num_programs(0) - 1)
    def _():
        ys = cb_ref[...] + w0 * buf0_ref[...] + w1 * buf1_ref[...] + w2 * g[SAMPLE_LOCAL:, :]
        act_ref[SAMPLE_LOCAL:, :] = (_silu(ys) * val[SAMPLE_LOCAL:, :]).astype(bf16)

    o_ref[...] += _dot(act_ref[...], wd_ref[...])

    @pl.when(j == pl.num_programs(1) - 1)
    def _():
        o_ref[...] = x_ref[...] + _rms(o_ref[...], gpost_ref[...])


def _ffn(x1, g_pre, g_post, w_up_b, conv_w, conv_b, w_down_b, buf0, buf1):
    n_ff = D_FF // FF_TN
    return pl.pallas_call(
        _ffn_kernel,
        grid=(R_ALL // TM, n_ff),
        in_specs=[
            pl.BlockSpec((TM, D_MODEL), lambda i, j: (i, 0)),
            pl.BlockSpec((1, D_MODEL), lambda i, j: (0, 0)),
            pl.BlockSpec((1, D_MODEL), lambda i, j: (0, 0)),
            pl.BlockSpec((D_MODEL, FF_TN), lambda i, j: (0, j)),
            pl.BlockSpec((D_MODEL, FF_TN), lambda i, j: (0, j + n_ff)),
            pl.BlockSpec((3, FF_TN), lambda i, j: (0, j)),
            pl.BlockSpec((1, FF_TN), lambda i, j: (0, j)),
            pl.BlockSpec((FF_TN, D_MODEL), lambda i, j: (j, 0)),
            pl.BlockSpec((DEC_BATCH, FF_TN), lambda i, j: (0, j)),
            pl.BlockSpec((DEC_BATCH, FF_TN), lambda i, j: (0, j)),
        ],
        out_specs=[
            pl.BlockSpec((TM, D_MODEL), lambda i, j: (i, 0)),
            pl.BlockSpec((1, TAIL_ROWS, FF_TN), lambda i, j: (i, 0, j)),
        ],
        out_shape=[
            jax.ShapeDtypeStruct((R_ALL, D_MODEL), f32),
            jax.ShapeDtypeStruct((R_ALL // TM, TAIL_ROWS, D_FF), f32),
        ],
        scratch_shapes=[
            pltpu.VMEM((TM, D_MODEL), bf16),
            pltpu.VMEM((TM, FF_TN), bf16),
            pltpu.VMEM((8, D_FF), f32),
        ],
        compiler_params=_params(("arbitrary", "arbitrary")),
        name="conv_ffn",
    )(x1, g_pre, g_post, w_up_b, w_up_b, conv_w, conv_b, w_down_b, buf0, buf1)


def kernel(x_prompt, x_sample, cache_ckv, cache_krope, state_hgrn, state_conv, page_table, meta_tokens,
           lb_raw, g_mix_pre, g_mix_post, g_ffn_pre, g_ffn_post, w_in, hgrn_o_norm, q_a_norm, kv_a_norm,
           w_q_up, w_kv_up, w_out, w_ffn_up, conv_w, conv_b, w_ffn_down):
    x_all = jnp.concatenate([jnp.zeros((PAD, D_MODEL), f32), meta_tokens.astype(f32), x_prompt[0],
                             x_sample[:, 0]], axis=0)

    w_in_t = jnp.pad(w_in[0].T, ((0, IN_PAD - IN_TOTAL), (0, 0))).astype(bf16)
    wq = w_q_up[0].reshape(Q_RANK, MLA_HEADS, QK_DIM)
    wq_p = jnp.concatenate([wq[:, :, :QK_NOPE].reshape(Q_RANK, -1), wq[:, :, QK_NOPE:].reshape(Q_RANK, -1)],
                           axis=1).astype(bf16)
    wkv = w_kv_up[0].astype(bf16)
    w_out_b = w_out[0].astype(bf16)
    w_up_b = w_ffn_up[0].astype(bf16)
    w_down_b = w_ffn_down[0].astype(bf16)

    proj = _in_proj(x_all, g_mix_pre, w_in_t)
    qh, kh, vh, ckv, kpe = _mla_proj(proj, q_a_norm, kv_a_norm, wq_p, wkv)

    o_a = _attention(qh, kh, vh)
    o_h, s_p = _hgrn_prompt(proj, lb_raw, hgrn_o_norm)

    o_h, s_s = _hgrn_sample(proj, lb_raw, hgrn_o_norm, state_hgrn[0], o_h)
    q_lat = _q_latent(qh, wkv).transpose(1, 0, 2)
    q_pe_s = qh[:, P_ROWS:, QK_NOPE:].transpose(1, 0, 2)
    ckv_s = ckv[P_ROWS:]
    kpe_s = kpe[P_ROWS:]
    o_lat = _paged_attention(page_table, q_lat, q_pe_s, ckv_s[:, None, :], kpe_s[:, None, :],
                             cache_ckv, cache_krope.transpose(0, 1, 3, 2))
    o_a = jnp.concatenate([o_a, _v_up(o_lat.transpose(1, 0, 2), wkv)], axis=0)

    x1 = _out_proj(o_h, o_a, x_all, w_out_b, g_mix_post)
    x2, tails = _ffn(x1, g_ffn_pre, g_ffn_post, w_up_b, conv_w[0], conv_b, w_down_b,
                     state_conv[0, :, 0], state_conv[0, :, 1])
    tail = tails[-1]

    n_tail = TAIL_ROWS - DEC_BATCH
    conv_p = tail[n_tail - 2:n_tail][None, None]
    conv_s = jnp.stack([state_conv[0, :, 1], tail[n_tail:]], axis=1)[None]
    return (x2[P_START:P_ROWS][None], x2[P_ROWS:][:, None],
            ckv[PAD:P_ROWS][None, None], kpe[PAD:P_ROWS][None, None],
            ckv_s[None, :, None], kpe_s[None, :, None],
            s_p[None, None], s_s[None], conv_p, conv_s)
```

```python
import functools

import numpy as np
import jax
import jax.numpy as jnp
from jax import lax
from jax.experimental import pallas as pl
from jax.experimental.pallas import tpu as pltpu

D_MODEL = 2048
SEQ = 8192
N_META = 16
DEC_BATCH = 128
PAST_LEN = 16384
PAGE_SIZE = 128
N_PAGES = PAST_LEN // PAGE_SIZE
HGRN_HEADS = 8
HGRN_K = 128
HGRN_V = 128
HGRN_WIDTH = HGRN_HEADS * HGRN_V
MLA_HEADS = 8
QK_NOPE = 128
QK_ROPE = 64
QK_DIM = QK_NOPE + QK_ROPE
V_HEAD = 128
Q_RANK = 512
KV_RANK = 256
MLA_WIDTH = MLA_HEADS * V_HEAD
ROPE_THETA = 10000.0
SOFTMAX_SCALE = QK_DIM ** -0.5
LOG2_E = 1.4426950408889634
D_FF = 5632
EPS = 1e-6
IN_TOTAL = 4 * HGRN_WIDTH + Q_RANK + KV_RANK + QK_ROPE

CHUNK = 128
PAD = CHUNK - N_META
P_START = PAD + N_META
P_ROWS = P_START + SEQ
R_ALL = P_ROWS + DEC_BATCH
N_CHUNKS = P_ROWS // CHUNK

LANES = 128
SUBLANES = 8
IN_PAD = 5120
IN_TN = 1280
TM = 768
TM_OUT = 768
TQ = 640
TQ_SUB = 640
FF_TN = 512
HGRN_SEQS = 16
HGRN_STAGE_SEQS = 8
SAMPLE_LOCAL = P_ROWS - (R_ALL // TM - 1) * TM
TAIL_ROWS = 136
PAGES_PER_STEP = 16
N_PAGE_STEPS = N_PAGES // PAGES_PER_STEP
KEYS_PER_STEP = PAGES_PER_STEP * PAGE_SIZE
NEG_BIG = -1e30
VMEM_LIMIT = 56 * 1024 * 1024

NT_DIMS = (((1,), (1,)), ((), ()))
TN_DIMS = (((0,), (0,)), ((), ()))

f32 = jnp.float32
bf16 = jnp.bfloat16


def _params(sem, vmem=VMEM_LIMIT):
    return pltpu.CompilerParams(dimension_semantics=sem, vmem_limit_bytes=vmem)


def _rms(x, g):
    return x * lax.rsqrt(jnp.mean(x * x, axis=-1, keepdims=True) + EPS) * g


def _sigmoid(x):
    return 1.0 / (1.0 + jnp.exp(-x))


def _silu(x):
    return x * _sigmoid(x)


def _dot(a, b):
    return jnp.dot(a, b, preferred_element_type=f32)


def _dot_nt(a, b):
    return lax.dot_general(a, b, NT_DIMS, preferred_element_type=f32)


def _dot_tn(a, b):
    return lax.dot_general(a, b, TN_DIMS, preferred_element_type=f32)


def _in_proj_kernel(x_ref, g_ref, w_ref, o_ref, h_ref):
    @pl.when(pl.program_id(1) == 0)
    def _():
        h_ref[...] = _rms(x_ref[...], g_ref[...]).astype(bf16)

    o_ref[...] = _dot_nt(h_ref[...], w_ref[...])


def _in_proj(x_all, g, w_in_t):
    return pl.pallas_call(
        _in_proj_kernel,
        grid=(R_ALL // TM, IN_PAD // IN_TN),
        in_specs=[
            pl.BlockSpec((TM, D_MODEL), lambda i, j: (i, 0)),
            pl.BlockSpec((1, D_MODEL), lambda i, j: (0, 0)),
            pl.BlockSpec((IN_TN, D_MODEL), lambda i, j: (j, 0)),
        ],
        out_specs=pl.BlockSpec((TM, IN_TN), lambda i, j: (i, j)),
        out_shape=jax.ShapeDtypeStruct((R_ALL, IN_PAD), f32),
        scratch_shapes=[pltpu.VMEM((TM, D_MODEL), bf16)],
        compiler_params=_params(("arbitrary", "arbitrary")),
        name="in_proj",
    )(x_all, g, w_in_t)


def _rope_tables(row0, n_rows):
    lane = lax.broadcasted_iota(jnp.int32, (1, LANES), 1)
    half = QK_ROPE // 2
    fidx = (lane & (half - 1)).astype(f32)
    inv = jnp.power(jnp.float32(ROPE_THETA), -fidx / half)
    rows = row0 + lax.broadcasted_iota(jnp.int32, (n_rows, 1), 0)
    pos = jnp.where(rows < P_ROWS, rows - PAD, PAST_LEN).astype(f32)
    ang = pos * inv
    sin = jnp.sin(ang)
    return jnp.cos(ang), jnp.where((lane & half) == 0, -sin, sin)


def _rotate_half(x, cos, sin_signed):
    n = x.shape[1]
    half = QK_ROPE // 2
    first_half = (lax.broadcasted_iota(jnp.int32, (1, n), 1) & half) == 0
    partner = jnp.where(first_half, pltpu.roll(x, n - half, 1), pltpu.roll(x, half, 1))
    return x * cos + partner * sin_signed


def _mla_proj_kernel(qd_ref, kvd_ref, kr_ref, qn_ref, kvn_ref, wq_ref, wkv_ref,
                     qh_ref, kh_ref, vh_ref, ckv_ref, kpe_ref):
    tm = qd_ref.shape[0]
    cos, sin_s = _rope_tables(pl.program_id(0) * tm, tm)

    q = _dot(_rms(qd_ref[...], qn_ref[...]).astype(bf16), wq_ref[...])
    reps = MLA_HEADS * QK_ROPE // LANES
    q_pe = _rotate_half(q[:, MLA_HEADS * QK_NOPE:],
                        jnp.concatenate([cos] * reps, axis=1),
                        jnp.concatenate([sin_s] * reps, axis=1))

    k_pe = _rotate_half(kr_ref[...], cos, sin_s)[:, :QK_ROPE]
    kpe_ref[...] = k_pe

    c_kv = _rms(kvd_ref[...], kvn_ref[...])
    ckv_ref[...] = c_kv
    kv = _dot(c_kv.astype(bf16), wkv_ref[...])

    k_pe_b = k_pe.astype(bf16)
    for h in range(MLA_HEADS):
        qh_ref[h, :, :QK_NOPE] = q[:, h * QK_NOPE:(h + 1) * QK_NOPE].astype(bf16)
        qh_ref[h, :, QK_NOPE:] = q_pe[:, h * QK_ROPE:(h + 1) * QK_ROPE].astype(bf16)
        base = h * (QK_NOPE + V_HEAD)
        kh_ref[h, :, :QK_NOPE] = kv[:, base:base + QK_NOPE].astype(bf16)
        kh_ref[h, :, QK_NOPE:] = k_pe_b
        vh_ref[h] = kv[:, base + QK_NOPE:base + QK_NOPE + V_HEAD].astype(bf16)


def _mla_proj(proj, q_a_norm, kv_a_norm, wq_p, wkv):
    col_q = (4 * HGRN_WIDTH) // Q_RANK
    col_kv = (4 * HGRN_WIDTH + Q_RANK) // KV_RANK
    col_kr = (4 * HGRN_WIDTH + Q_RANK + KV_RANK) // LANES
    head_spec = lambda d: pl.BlockSpec((MLA_HEADS, TM, d), lambda i: (0, i, 0))
    return pl.pallas_call(
        _mla_proj_kernel,
        grid=(R_ALL // TM,),
        in_specs=[
            pl.BlockSpec((TM, Q_RANK), lambda i: (i, col_q)),
            pl.BlockSpec((TM, KV_RANK), lambda i: (i, col_kv)),
            pl.BlockSpec((TM, LANES), lambda i: (i, col_kr)),
            pl.BlockSpec((1, Q_RANK), lambda i: (0, 0)),
            pl.BlockSpec((1, KV_RANK), lambda i: (0, 0)),
            pl.BlockSpec(wq_p.shape, lambda i: (0, 0)),
            pl.BlockSpec(wkv.shape, lambda i: (0, 0)),
        ],
        out_specs=[
            head_spec(QK_DIM), head_spec(QK_DIM), head_spec(V_HEAD),
            pl.BlockSpec((TM, KV_RANK), lambda i: (i, 0)),
            pl.BlockSpec((TM, QK_ROPE), lambda i: (i, 0)),
        ],
        out_shape=[
            jax.ShapeDtypeStruct((MLA_HEADS, R_ALL, QK_DIM), bf16),
            jax.ShapeDtypeStruct((MLA_HEADS, R_ALL, QK_DIM), bf16),
            jax.ShapeDtypeStruct((MLA_HEADS, R_ALL, V_HEAD), bf16),
            jax.ShapeDtypeStruct((R_ALL, KV_RANK), f32),
            jax.ShapeDtypeStruct((R_ALL, QK_ROPE), f32),
        ],
        compiler_params=_params(("arbitrary",)),
        name="mla_proj",
    )(proj, proj, proj, q_a_norm, kv_a_norm, wq_p, wkv)


def _attn_kernel(q_ref, k_ref, v_ref, o_ref, m_ref, l_ref, acc_ref, ta_ref, tb_ref):
    i = pl.program_id(1)
    q = q_ref[0]
    m_ref[...] = jnp.full(m_ref.shape, NEG_BIG, f32)
    l_ref[...] = jnp.zeros(l_ref.shape, f32)
    acc_ref[...] = jnp.zeros(acc_ref.shape, f32)
    reps = TQ // LANES
    groups = range(0, TQ, TQ_SUB)

    def logits(t_ref, j):
        t_ref[...] = _dot_nt(q, k_ref[0, pl.ds(pl.multiple_of(j * TQ, TQ), TQ), :])

    def block(t_ref, j, masked):
        v = v_ref[0, pl.ds(pl.multiple_of(j * TQ, TQ), TQ), :]
        for r0 in groups:
            rows = slice(r0, r0 + TQ_SUB)
            t = t_ref[rows] * (SOFTMAX_SCALE * LOG2_E)
            if masked:
                q_row = i * TQ + r0 + lax.broadcasted_iota(jnp.int32, (TQ_SUB, TQ), 0)
                k_row = j * TQ + lax.broadcasted_iota(jnp.int32, (TQ_SUB, TQ), 1)
                t = jnp.where(k_row <= q_row, jnp.where(k_row >= PAD, t, NEG_BIG), NEG_BIG)
            m_prev = m_ref[rows]
            m_next = jnp.maximum(m_prev, jnp.max(t, axis=1, keepdims=True))
            p = jnp.exp2(t - jnp.concatenate([m_next] * reps, axis=1))
            alpha = jnp.exp2(m_prev - m_next)
            l_ref[rows] = alpha * l_ref[rows] + jnp.sum(p, axis=1, keepdims=True)
            acc_ref[rows] = alpha * acc_ref[rows] + _dot(p.astype(bf16), v)
            m_ref[rows] = m_next

    logits(ta_ref, 0)

    @pl.when(i == 0)
    def _():
        block(ta_ref, 0, True)

    @pl.when(i > 0)
    def _():
        logits(tb_ref, 1)
        block(ta_ref, 0, True)
        n_inner = i - 1

        def pair(p, carry):
            j = 1 + 2 * p
            logits(ta_ref, j + 1)
            block(tb_ref, j, False)
            logits(tb_ref, j + 2)
            block(ta_ref, j + 1, False)
            return carry

        lax.fori_loop(0, n_inner // 2, pair, 0)
        j = 1 + 2 * (n_inner // 2)

        @pl.when(n_inner % 2 == 1)
        def _():
            logits(ta_ref, j + 1)
            block(tb_ref, j, False)
            block(ta_ref, j + 1, True)

        @pl.when(n_inner % 2 == 0)
        def _():
            block(tb_ref, j, True)

    o_ref[...] = (acc_ref[...] / l_ref[...]).astype(bf16)


def _attention(qh, kh, vh):
    return pl.pallas_call(
        _attn_kernel,
        grid=(MLA_HEADS, P_ROWS // TQ),
        in_specs=[
            pl.BlockSpec((1, TQ, QK_DIM), lambda h, i: (h, i, 0)),
            pl.BlockSpec((1, R_ALL, QK_DIM), lambda h, i: (h, 0, 0)),
            pl.BlockSpec((1, R_ALL, V_HEAD), lambda h, i: (h, 0, 0)),
        ],
        out_specs=pl.BlockSpec((TQ, V_HEAD), lambda h, i: (i, h)),
        out_shape=jax.ShapeDtypeStruct((P_ROWS, MLA_WIDTH), bf16),
        scratch_shapes=[pltpu.VMEM((TQ, LANES), f32), pltpu.VMEM((TQ, LANES), f32),
                        pltpu.VMEM((TQ, V_HEAD), f32),
                        pltpu.VMEM((TQ, TQ), f32), pltpu.VMEM((TQ, TQ), f32)],
        compiler_params=_params(("arbitrary", "arbitrary")),
        name="prompt_attn",
    )(qh, kh, vh)


N_LEVELS = 7


def _hgrn_constants():
    t = np.arange(CHUNK)
    sums = (t[None, :] <= t[:, None]).astype(np.float32)
    x = t[:, None] ^ t[None, :]
    lev = np.where(x > 0, np.floor(np.log2(np.maximum(x, 1))).astype(np.int32), N_LEVELS)
    lev = np.where(t[None, :] > t[:, None], N_LEVELS + 1, lev).astype(np.int32)
    return sums, lev


def _split3(x):
    a = x.astype(bf16)
    r = x - a.astype(f32)
    b = r.astype(bf16)
    c = (r - b.astype(f32)).astype(bf16)
    return a, b, c


def _lower_bound(lb_raw_ref):
    a0 = lb_raw_ref[0:1, :]
    a1 = lb_raw_ref[1:2, :]
    m = jnp.maximum(a0, a1)
    e0 = jnp.exp(a0 - m)
    return e0 / (e0 + jnp.exp(a1 - m))


def _pair_block_reference(b, lvl):
    m = 1 << lvl
    n, w = b.shape
    if m >= SUBLANES:
        g = m // SUBLANES
        b4 = b.reshape(n // (2 * m), 2 * g, SUBLANES, w)
        return jnp.broadcast_to(b4[:, g - 1:g, SUBLANES - 1:, :], b4.shape).reshape(n, w)
    b3 = b.reshape(n // SUBLANES, SUBLANES, w)
    sub = lax.broadcasted_iota(jnp.int32, (1, SUBLANES, 1), 1)
    ref = b3[:, m - 1:m, :]
    for start in range(2 * m, SUBLANES, 2 * m):
        ref = jnp.where(sub >= start, b3[:, start + m - 1:start + m, :], ref)
    return jnp.broadcast_to(ref, b3.shape).reshape(n, w)


def _hgrn_prompt_kernel(hq_ref, hf_ref, hi_ref, hg_ref, lb_ref, on_ref, sums_ref, lev_ref,
                        o_ref, s_ref, st_ref):
    c = pl.program_id(0)

    @pl.when(c == 0)
    def _():
        st_ref[...] = jnp.zeros(st_ref.shape, f32)

    @pl.when(c == N_CHUNKS)
    def _():
        o_ref[...] = jnp.zeros(o_ref.shape, bf16)

    @pl.when(c < N_CHUNKS)
    def _():
        lb = _lower_bound(lb_ref)
        f_all = lb + (1.0 - lb) * _sigmoid(hf_ref[...])
        l1, l2, l3 = _split3(jnp.log(f_all))
        sums = sums_ref[...]
        b_all = _dot(sums, l1) + _dot(sums, l2) + _dot(sums, l3)

        row = lax.broadcasted_iota(jnp.int32, (CHUNK, 1), 0)
        lev = lev_ref[...]
        refs = [_pair_block_reference(b_all, lvl) for lvl in range(N_LEVELS)]
        is_q = [((row >> lvl) & 1) == 1 for lvl in range(N_LEVELS)]
        sign = [jnp.where(m, 1.0, -1.0) for m in is_q]
        for h in range(HGRN_HEADS):
            sl = slice(h * HGRN_K, (h + 1) * HGRN_K)
            q = _silu(hq_ref[:, sl])
            k = 1.0 - f_all[:, sl]
            v = hi_ref[:, sl].astype(bf16)
            b = b_all[:, sl]
            a = jnp.where(lev == N_LEVELS, _dot_nt(q.astype(bf16), k.astype(bf16)), 0.0)
            for lvl in range(N_LEVELS):
                w = (jnp.where(is_q[lvl], q, k) * jnp.exp((b - refs[lvl][:, sl]) * sign[lvl])).astype(bf16)
                a = jnp.where(lev == lvl, _dot_nt(w, w), a)

            b_suf = b[CHUNK - 1:CHUNK, :] - b
            st = st_ref[h]
            o = _dot(a.astype(bf16), v) + _dot_nt((q * jnp.exp(b)).astype(bf16), st.astype(bf16))
            st_new = st * jnp.exp(b[CHUNK - 1:CHUNK, :]) + _dot_tn(v, (k * jnp.exp(b_suf)).astype(bf16))
            st_ref[h] = st_new
            o_ref[:, sl] = (_rms(o, on_ref[...]) * _silu(hg_ref[:, sl])).astype(bf16)

    @pl.when(c == N_CHUNKS - 1)
    def _():
        for h in range(HGRN_HEADS):
            s_ref[h] = st_ref[h].T


def _hgrn_prompt(proj, lb_raw, o_norm):
    sums, lev = _hgrn_constants()
    blk = lambda part: pl.BlockSpec((CHUNK, HGRN_WIDTH), lambda c: (c, part))
    return pl.pallas_call(
        _hgrn_prompt_kernel,
        grid=(N_CHUNKS + DEC_BATCH // CHUNK,),
        in_specs=[
            blk(0), blk(1), blk(2), blk(3),
            pl.BlockSpec((2, HGRN_WIDTH), lambda c: (0, 0)),
            pl.BlockSpec((1, HGRN_V), lambda c: (0, 0)),
            pl.BlockSpec(sums.shape, lambda c: (0, 0)),
            pl.BlockSpec(lev.shape, lambda c: (0, 0)),
        ],
        out_specs=[
            pl.BlockSpec((CHUNK, HGRN_WIDTH), lambda c: (c, 0)),
            pl.BlockSpec((HGRN_HEADS, HGRN_K, HGRN_V), lambda c: (0, 0, 0)),
        ],
        out_shape=[
            jax.ShapeDtypeStruct((R_ALL, HGRN_WIDTH), bf16),
            jax.ShapeDtypeStruct((HGRN_HEADS, HGRN_K, HGRN_V), f32),
        ],
        scratch_shapes=[pltpu.VMEM((HGRN_HEADS, HGRN_V, HGRN_K), f32)],
        compiler_params=_params(("arbitrary",)),
        name="hgrn_prompt",
    )(proj, proj, proj, proj, lb_raw, o_norm, jnp.asarray(sums, bf16), jnp.asarray(lev))


def _hgrn_sample_kernel(hq_ref, hf_ref, hi_ref, hg_ref, lb_ref, on_ref, s0_ref, oh_hbm, o_ref, s_ref):
    del oh_hbm
    lb = _lower_bound(lb_ref)
    f = lb + (1.0 - lb) * _sigmoid(hf_ref[...])
    q = _silu(hq_ref[...])
    k = (1.0 - f).astype(bf16).astype(f32)
    v = hi_ref[...].astype(bf16).astype(f32)
    gate = _silu(hg_ref[...])
    f1 = f.astype(bf16).astype(f32)
    f2 = (f - f1).astype(bf16).astype(f32)
    f3 = ((f - f1) - f2).astype(bf16).astype(f32)

    n_rows = 16
    rid = lax.broadcasted_iota(jnp.int32, (n_rows, HGRN_K), 0)
    rid2 = lax.broadcasted_iota(jnp.int32, (n_rows, 2 * HGRN_V), 0)
    left = lax.broadcasted_iota(jnp.int32, (n_rows, 2 * HGRN_V), 1) < HGRN_V
    zero_v = jnp.zeros((1, HGRN_V), f32)

    at = lambda x, s, h: x[s:s + 1, h * HGRN_K:(h + 1) * HGRN_K]

    def mix(s, h):
        lhs = jnp.where(rid == 0, at(f1, s, h), jnp.where(rid == 1, at(f2, s, h), jnp.where(
            rid == 2, at(f3, s, h), jnp.where(rid == 3, at(k, s, h), 0.0))))
        rhs = jnp.where(rid2 < 3, jnp.where(left, 1.0, 0.0),
                        jnp.where(rid2 == 3, jnp.concatenate([zero_v, at(v, s, h)], axis=1), 0.0))
        return _dot_tn(lhs.astype(bf16), rhs.astype(bf16))

    outs = []
    for s0 in range(0, HGRN_SEQS, HGRN_STAGE_SEQS):
        pairs = [(s, h) for s in range(s0, s0 + HGRN_STAGE_SEQS) for h in range(HGRN_HEADS)]
        mixes = [mix(s, h) for s, h in pairs]
        for (s, h), mx in zip(pairs, mixes):
            s_ref[s, h] = mx[:, :HGRN_V] * s0_ref[s, h] + mx[:, HGRN_V:]
        o_raw = [_dot(jnp.where(rid == 0, at(q, s, h), 0.0).astype(bf16), s_ref[s, h].astype(bf16))[0:1]
                 for s, h in pairs]
        outs += [_rms(o, on_ref[...]) * at(gate, s, h) for (s, h), o in zip(pairs, o_raw)]
    o_ref[...] = jnp.concatenate(
        [jnp.concatenate(outs[s * HGRN_HEADS:(s + 1) * HGRN_HEADS], axis=1) for s in range(HGRN_SEQS)],
        axis=0).astype(bf16)


def _hgrn_sample(proj, lb_raw, o_norm, state, o_h):
    row_blk = P_ROWS // HGRN_SEQS
    blk = lambda part: pl.BlockSpec((HGRN_SEQS, HGRN_WIDTH), lambda b: (row_blk + b, part))
    state_spec = pl.BlockSpec((HGRN_SEQS, HGRN_HEADS, HGRN_K, HGRN_V), lambda b: (b, 0, 0, 0))
    return pl.pallas_call(
        _hgrn_sample_kernel,
        grid=(DEC_BATCH // HGRN_SEQS,),
        in_specs=[
            blk(0), blk(1), blk(2), blk(3),
            pl.BlockSpec((2, HGRN_WIDTH), lambda b: (0, 0)),
            pl.BlockSpec((1, HGRN_V), lambda b: (0, 0)),
            state_spec,
            pl.BlockSpec(memory_space=pl.ANY),
        ],
        out_specs=[blk(0), state_spec],
        out_shape=[
            jax.ShapeDtypeStruct(o_h.shape, o_h.dtype),
            jax.ShapeDtypeStruct(state.shape, f32),
        ],
        input_output_aliases={7: 0},
        compiler_params=_params(("arbitrary",)),
        name="hgrn_sample",
    )(proj, proj, proj, proj, lb_raw, o_norm, state, o_h)


def _q_latent_kernel(q_ref, wuk_ref, o_ref):
    o_ref[0] = _dot_nt(q_ref[0][:, :QK_NOPE], wuk_ref[...]).astype(bf16)


def _q_latent(qh, wkv):
    row_blk = P_ROWS // DEC_BATCH
    return pl.pallas_call(
        _q_latent_kernel,
        grid=(MLA_HEADS,),
        in_specs=[
            pl.BlockSpec((1, DEC_BATCH, QK_DIM), lambda h: (h, row_blk, 0)),
            pl.BlockSpec((KV_RANK, QK_NOPE), lambda h: (0, 2 * h)),
        ],
        out_specs=pl.BlockSpec((1, DEC_BATCH, KV_RANK), lambda h: (h, 0, 0)),
        out_shape=jax.ShapeDtypeStruct((MLA_HEADS, DEC_BATCH, KV_RANK), bf16),
        compiler_params=_params(("arbitrary",)),
        name="q_latent",
    )(qh, wkv)


def _page_copies(pt_ref, ckv_hbm, kpe_hbm, ckv_buf, kpe_buf, sems, seq, c):
    copies = []
    for p in range(PAGES_PER_STEP):
        page = pt_ref[seq, c * PAGES_PER_STEP + p]
        copies.append(pltpu.make_async_copy(ckv_hbm.at[0, page], ckv_buf.at[c, p], sems.at[0, c]))
        copies.append(pltpu.make_async_copy(kpe_hbm.at[0, page], kpe_buf.at[c, p], sems.at[1, c]))
    return copies


def _paged_attn_kernel(pt_ref, ql_ref, qp_ref, cn_ref, kn_ref, ckv_hbm, kpe_hbm, o_ref,
                       ckv_buf, kpe_buf, sems, ckb_ref, kpb_ref):
    b = pl.program_id(0)
    copies = functools.partial(_page_copies, pt_ref, ckv_hbm, kpe_hbm, ckv_buf, kpe_buf, sems)
    to_log2 = SOFTMAX_SCALE * LOG2_E

    @pl.when(b == 0)
    def _():
        for c in range(N_PAGE_STEPS):
            for cp in copies(b, c):
                cp.start()

    ql = ql_ref[0]
    qp = qp_ref[0]

    def scores(c):
        for cp in copies(b, c):
            cp.wait()
        half = c % 2
        ckb_ref[half] = ckv_buf[c].reshape(KEYS_PER_STEP, KV_RANK).astype(bf16)
        kpb_ref[half] = jnp.concatenate([kpe_buf[c, p] for p in range(PAGES_PER_STEP)], axis=1).astype(bf16)
        return (_dot_nt(ql, ckb_ref[half]) + _dot(qp, kpb_ref[half])) * to_log2

    t_new = (jnp.sum(ql.astype(f32) * cn_ref[0], axis=1, keepdims=True)
             + jnp.sum(qp.astype(f32) * kn_ref[0], axis=1, keepdims=True)) * to_log2
    m = jnp.broadcast_to(t_new, (MLA_HEADS, LANES))
    l = jnp.ones((MLA_HEADS, LANES), f32)
    acc = jnp.broadcast_to(cn_ref[0], (MLA_HEADS, KV_RANK))

    t = scores(0)
    for c in range(N_PAGE_STEPS):
        t_next = scores(c + 1) if c + 1 < N_PAGE_STEPS else None
        m_next = jnp.maximum(m, jnp.max(t, axis=1, keepdims=True))
        p = jnp.exp2(t - jnp.concatenate([m_next] * (KEYS_PER_STEP // LANES), axis=1))
        alpha = jnp.exp2(m - m_next)
        l = alpha * l + jnp.sum(p, axis=1, keepdims=True)
        acc = jnp.concatenate([alpha] * (KV_RANK // LANES), axis=1) * acc + _dot(p.astype(bf16), ckb_ref[c % 2])
        m = m_next

        @pl.when(b + 1 < pl.num_programs(0))
        def _():
            for cp in copies(b + 1, c):
                cp.start()

        t = t_next

    o_ref[0] = acc / jnp.concatenate([l] * (KV_RANK // LANES), axis=1)


def _paged_attention(page_table, q_lat, q_pe, ckv_new, kpe_new, cache_ckv, cache_krope_t):
    seq_spec = lambda r, d: pl.BlockSpec((1, r, d), lambda b, pt: (b, 0, 0))
    grid_spec = pltpu.PrefetchScalarGridSpec(
        num_scalar_prefetch=1,
        grid=(DEC_BATCH,),
        in_specs=[
            seq_spec(MLA_HEADS, KV_RANK), seq_spec(MLA_HEADS, QK_ROPE),
            seq_spec(1, KV_RANK), seq_spec(1, QK_ROPE),
            pl.BlockSpec(memory_space=pl.ANY), pl.BlockSpec(memory_space=pl.ANY),
        ],
        out_specs=seq_spec(MLA_HEADS, KV_RANK),
        scratch_shapes=[
            pltpu.VMEM((N_PAGE_STEPS, PAGES_PER_STEP, PAGE_SIZE, KV_RANK), f32),
            pltpu.VMEM((N_PAGE_STEPS, PAGES_PER_STEP, QK_ROPE, PAGE_SIZE), f32),
            pltpu.SemaphoreType.DMA((2, N_PAGE_STEPS)),
            pltpu.VMEM((2, KEYS_PER_STEP, KV_RANK), bf16),
            pltpu.VMEM((2, QK_ROPE, KEYS_PER_STEP), bf16),
        ],
    )
    return pl.pallas_call(
        _paged_attn_kernel,
        grid_spec=grid_spec,
        out_shape=jax.ShapeDtypeStruct((DEC_BATCH, MLA_HEADS, KV_RANK), f32),
        compiler_params=_params(("arbitrary",)),
        name="paged_attn",
    )(page_table, q_lat, q_pe, ckv_new, kpe_new, cache_ckv, cache_krope_t)


def _v_up_kernel(o_ref, wuv_ref, out_ref):
    out_ref[...] = _dot(o_ref[0].astype(bf16), wuv_ref[...]).astype(bf16)


def _v_up(o_lat_h, wkv):
    return pl.pallas_call(
        _v_up_kernel,
        grid=(MLA_HEADS,),
        in_specs=[
            pl.BlockSpec((1, DEC_BATCH, KV_RANK), lambda h: (h, 0, 0)),
            pl.BlockSpec((KV_RANK, V_HEAD), lambda h: (0, 2 * h + 1)),
        ],
        out_specs=pl.BlockSpec((DEC_BATCH, V_HEAD), lambda h: (0, h)),
        out_shape=jax.ShapeDtypeStruct((DEC_BATCH, MLA_WIDTH), bf16),
        compiler_params=_params(("arbitrary",)),
        name="v_up",
    )(o_lat_h, wkv)


def _out_proj_kernel(oh_ref, oa_ref, x_ref, wh_ref, wa_ref, g_ref, o_ref):
    mix = _dot(oh_ref[...], wh_ref[...]) + _dot(oa_ref[...], wa_ref[...])
    o_ref[...] = x_ref[...] + _rms(mix, g_ref[...])


def _out_proj(o_h, o_a, x_all, w_out_b, g):
    return pl.pallas_call(
        _out_proj_kernel,
        grid=(R_ALL // TM_OUT,),
        in_specs=[
            pl.BlockSpec((TM_OUT, HGRN_WIDTH), lambda i: (i, 0)),
            pl.BlockSpec((TM_OUT, MLA_WIDTH), lambda i: (i, 0)),
            pl.BlockSpec((TM_OUT, D_MODEL), lambda i: (i, 0)),
            pl.BlockSpec((HGRN_WIDTH, D_MODEL), lambda i: (0, 0)),
            pl.BlockSpec((MLA_WIDTH, D_MODEL), lambda i: (1, 0)),
            pl.BlockSpec((1, D_MODEL), lambda i: (0, 0)),
        ],
        out_specs=pl.BlockSpec((TM_OUT, D_MODEL), lambda i: (i, 0)),
        out_shape=jax.ShapeDtypeStruct((R_ALL, D_MODEL), f32),
        compiler_params=_params(("arbitrary",)),
        name="out_proj",
    )(o_h, o_a, x_all, w_out_b, w_out_b, g)


def _ffn_out_copies(acc_ref, sems, yp_hbm, ys_hbm, tile, slot, kind):
    if kind == "first":
        return [pltpu.make_async_copy(acc_ref.at[slot, pl.ds(P_START, TM - P_START)],
                                      yp_hbm.at[pl.ds(0, TM - P_START)], sems.at[slot])]
    if kind == "middle":
        return [pltpu.make_async_copy(acc_ref.at[slot],
                                      yp_hbm.at[pl.ds(pl.multiple_of(tile * TM - P_START, CHUNK), TM)],
                                      sems.at[slot])]
    return [pltpu.make_async_copy(acc_ref.at[slot, pl.ds(0, SAMPLE_LOCAL)],
                                  yp_hbm.at[pl.ds(SEQ - SAMPLE_LOCAL, SAMPLE_LOCAL)], sems.at[slot]),
            pltpu.make_async_copy(acc_ref.at[slot, pl.ds(SAMPLE_LOCAL, DEC_BATCH)], ys_hbm, sems.at[slot])]


def _ffn_kernel(x_ref, gpre_ref, gpost_ref, wg_ref, wv_ref, cw_ref, cb_ref, wd_ref, buf0_ref, buf1_ref,
                yp_hbm, ys_hbm, tail_ref, h_ref, act_ref, carry_ref, acc_ref, sems):
    i = pl.program_id(0)
    j = pl.program_id(1)
    last_i = pl.num_programs(0) - 1
    slot = i % 2
    col = pl.ds(pl.multiple_of(j * FF_TN, FF_TN), FF_TN)
    out_copies = functools.partial(_ffn_out_copies, acc_ref, sems, yp_hbm, ys_hbm)

    @pl.when(j == 0)
    def _():
        @pl.when(i == 2)
        def _():
            for cp in out_copies(0, slot, "first"):
                cp.wait()

        @pl.when(i > 2)
        def _():
            for cp in out_copies(i - 2, slot, "middle"):
                cp.wait()

        h_ref[...] = _rms(x_ref[...], gpre_ref[...]).astype(bf16)
        acc_ref[slot] = jnp.zeros((TM, D_MODEL), f32)

    @pl.when(i == 0)
    def _():
        carry_ref[:, col] = jnp.zeros((8, FF_TN), f32)

    h = h_ref[...]
    row = lax.broadcasted_iota(jnp.int32, (TM, 1), 0)
    g = jnp.where(row + i * TM >= PAD, _dot(h, wg_ref[...]), 0.0)
    val = _dot(h, wv_ref[...])
    prev = carry_ref[:, col]
    g1 = jnp.where(row == 0, prev[7:8], pltpu.roll(g, 1, 0))
    g2 = jnp.where(row == 0, prev[6:7], jnp.where(row == 1, prev[7:8], pltpu.roll(g, 2, 0)))
    w0, w1, w2 = cw_ref[0:1, :], cw_ref[1:2, :], cw_ref[2:3, :]
    y = cb_ref[...] + w0 * g2 + w1 * g1 + w2 * g
    act_ref[...] = (_silu(y) * val).astype(bf16)
    carry_ref[:, col] = g[TM - 8:, :]
    tail_ref[0] = g[TM - TAIL_ROWS:, :]

    @pl.when(i == last_i)
    def _():
        ys = cb_ref[...] + w0 * buf0_ref[...] + w1 * buf1_ref[...] + w2 * g[SAMPLE_LOCAL:, :]
        act_ref[SAMPLE_LOCAL:, :] = (_silu(ys) * val[SAMPLE_LOCAL:, :]).astype(bf16)

    acc_ref[slot] += _dot(act_ref[...], wd_ref[...])

    @pl.when(j == pl.num_programs(1) - 1)
    def _():
        acc_ref[slot] = x_ref[...] + _rms(acc_ref[slot], gpost_ref[...])

        @pl.when(i == 0)
        def _():
            for cp in out_copies(i, slot, "first"):
                cp.start()

        @pl.when((i > 0) & (i < last_i))
        def _():
            for cp in out_copies(i, slot, "middle"):
                cp.start()

        @pl.when(i == last_i)
        def _():
            for cp in out_copies(i, slot, "last"):
                cp.start()
            for cp in out_copies(i - 1, 1 - slot, "middle"):
                cp.wait()
            for cp in out_copies(i, slot, "last"):
                cp.wait()


def _ffn(x1, g_pre, g_post, w_up_b, conv_w, conv_b, w_down_b, buf0, buf1):
    n_ff = D_FF // FF_TN
    return pl.pallas_call(
        _ffn_kernel,
        grid=(R_ALL // TM, n_ff),
        in_specs=[
            pl.BlockSpec((TM, D_MODEL), lambda i, j: (i, 0)),
            pl.BlockSpec((1, D_MODEL), lambda i, j: (0, 0)),
            pl.BlockSpec((1, D_MODEL), lambda i, j: (0, 0)),
            pl.BlockSpec((D_MODEL, FF_TN), lambda i, j: (0, j)),
            pl.BlockSpec((D_MODEL, FF_TN), lambda i, j: (0, j + n_ff)),
            pl.BlockSpec((3, FF_TN), lambda i, j: (0, j)),
            pl.BlockSpec((1, FF_TN), lambda i, j: (0, j)),
            pl.BlockSpec((FF_TN, D_MODEL), lambda i, j: (j, 0)),
            pl.BlockSpec((DEC_BATCH, FF_TN), lambda i, j: (0, j)),
            pl.BlockSpec((DEC_BATCH, FF_TN), lambda i, j: (0, j)),
        ],
        out_specs=[
            pl.BlockSpec(memory_space=pl.ANY),
            pl.BlockSpec(memory_space=pl.ANY),
            pl.BlockSpec((1, TAIL_ROWS, FF_TN), lambda i, j: (i, 0, j)),
        ],
        out_shape=[
            jax.ShapeDtypeStruct((SEQ, D_MODEL), f32),
            jax.ShapeDtypeStruct((DEC_BATCH, D_MODEL), f32),
            jax.ShapeDtypeStruct((R_ALL // TM, TAIL_ROWS, D_FF), f32),
        ],
        scratch_shapes=[
            pltpu.VMEM((TM, D_MODEL), bf16),
            pltpu.VMEM((TM, FF_TN), bf16),
            pltpu.VMEM((8, D_FF), f32),
            pltpu.VMEM((2, TM, D_MODEL), f32),
            pltpu.SemaphoreType.DMA((2,)),
        ],
        compiler_params=_params(("arbitrary", "arbitrary")),
        name="conv_ffn",
    )(x1, g_pre, g_post, w_up_b, w_up_b, conv_w, conv_b, w_down_b, buf0, buf1)


def kernel(x_prompt, x_sample, cache_ckv, cache_krope, state_hgrn, state_conv, page_table, meta_tokens,
           lb_raw, g_mix_pre, g_mix_post, g_ffn_pre, g_ffn_post, w_in, hgrn_o_norm, q_a_norm, kv_a_norm,
           w_q_up, w_kv_up, w_out, w_ffn_up, conv_w, conv_b, w_ffn_down):
    x_all = jnp.concatenate([jnp.zeros((PAD, D_MODEL), f32), meta_tokens.astype(f32), x_prompt[0],
                             x_sample[:, 0]], axis=0)

    w_in_t = jnp.pad(w_in[0].T, ((0, IN_PAD - IN_TOTAL), (0, 0))).astype(bf16)
    wq = w_q_up[0].reshape(Q_RANK, MLA_HEADS, QK_DIM)
    wq_p = jnp.concatenate([wq[:, :, :QK_NOPE].reshape(Q_RANK, -1), wq[:, :, QK_NOPE:].reshape(Q_RANK, -1)],
                           axis=1).astype(bf16)
    wkv = w_kv_up[0].astype(bf16)
    w_out_b = w_out[0].astype(bf16)
    w_up_b = w_ffn_up[0].astype(bf16)
    w_down_b = w_ffn_down[0].astype(bf16)

    proj = _in_proj(x_all, g_mix_pre, w_in_t)
    qh, kh, vh, ckv, kpe = _mla_proj(proj, q_a_norm, kv_a_norm, wq_p, wkv)

    o_a = _attention(qh, kh, vh)
    o_h, s_p = _hgrn_prompt(proj, lb_raw, hgrn_o_norm)

    o_h, s_s = _hgrn_sample(proj, lb_raw, hgrn_o_norm, state_hgrn[0], o_h)
    q_lat = _q_latent(qh, wkv).transpose(1, 0, 2)
    q_pe_s = qh[:, P_ROWS:, QK_NOPE:].transpose(1, 0, 2)
    ckv_s = ckv[P_ROWS:]
    kpe_s = kpe[P_ROWS:]
    o_lat = _paged_attention(page_table, q_lat, q_pe_s, ckv_s[:, None, :], kpe_s[:, None, :],
                             cache_ckv, cache_krope.transpose(0, 1, 3, 2))
    o_a = jnp.concatenate([o_a, _v_up(o_lat.transpose(1, 0, 2), wkv)], axis=0)

    x1 = _out_proj(o_h, o_a, x_all, w_out_b, g_mix_post)
    y_p, y_s, tails = _ffn(x1, g_ffn_pre, g_ffn_post, w_up_b, conv_w[0], conv_b, w_down_b,
                           state_conv[0, :, 0], state_conv[0, :, 1])
    tail = tails[-1]

    n_tail = TAIL_ROWS - DEC_BATCH
    conv_p = tail[n_tail - 2:n_tail][None, None]
    conv_s = jnp.stack([state_conv[0, :, 1], tail[n_tail:]], axis=1)[None]
    return (y_p[None], y_s[:, None],
            ckv[PAD:P_ROWS][None, None], kpe[PAD:P_ROWS][None, None],
            ckv_s[None, :, None], kpe_s[None, :, None],
            s_p[None, None], s_s[None], conv_p, conv_s)
```

```python
import functools

import numpy as np
import jax
import jax.numpy as jnp
from jax import lax
from jax.experimental import pallas as pl
from jax.experimental.pallas import tpu as pltpu

D_MODEL = 2048
SEQ = 8192
N_META = 16
DEC_BATCH = 128
PAST_LEN = 16384
PAGE_SIZE = 128
N_PAGES = PAST_LEN // PAGE_SIZE
HGRN_HEADS = 8
HGRN_K = 128
HGRN_V = 128
HGRN_WIDTH = HGRN_HEADS * HGRN_V
MLA_HEADS = 8
QK_NOPE = 128
QK_ROPE = 64
QK_DIM = QK_NOPE + QK_ROPE
V_HEAD = 128
Q_RANK = 512
KV_RANK = 256
MLA_WIDTH = MLA_HEADS * V_HEAD
ROPE_THETA = 10000.0
SOFTMAX_SCALE = QK_DIM ** -0.5
LOG2_E = 1.4426950408889634
D_FF = 5632
EPS = 1e-6
IN_TOTAL = 4 * HGRN_WIDTH + Q_RANK + KV_RANK + QK_ROPE

CHUNK = 128
PAD = CHUNK - N_META
P_START = PAD + N_META
P_ROWS = P_START + SEQ
R_ALL = P_ROWS + DEC_BATCH
N_CHUNKS = P_ROWS // CHUNK

LANES = 128
SUBLANES = 8
IN_PAD = 5120
IN_TN = 1280
TM = 768
TQ = 640
TQ_SUB = 640
FF_TN = 512
HGRN_SEQS = 16
HGRN_STAGE_SEQS = 8
SAMPLE_LOCAL = P_ROWS - (R_ALL // TM - 1) * TM
TAIL_ROWS = 136
PAGES_PER_STEP = 16
N_PAGE_STEPS = N_PAGES // PAGES_PER_STEP
KEYS_PER_STEP = PAGES_PER_STEP * PAGE_SIZE
NEG_BIG = -1e30
VMEM_LIMIT = 56 * 1024 * 1024

NT_DIMS = (((1,), (1,)), ((), ()))
TN_DIMS = (((0,), (0,)), ((), ()))

f32 = jnp.float32
bf16 = jnp.bfloat16


def _params(sem, vmem=VMEM_LIMIT):
    return pltpu.CompilerParams(dimension_semantics=sem, vmem_limit_bytes=vmem)


def _rms(x, g):
    return x * lax.rsqrt(jnp.mean(x * x, axis=-1, keepdims=True) + EPS) * g


def _sigmoid(x):
    return 1.0 / (1.0 + jnp.exp(-x))


def _silu(x):
    return x * _sigmoid(x)


def _dot(a, b):
    return jnp.dot(a, b, preferred_element_type=f32)


def _dot_nt(a, b):
    return lax.dot_general(a, b, NT_DIMS, preferred_element_type=f32)


def _dot_tn(a, b):
    return lax.dot_general(a, b, TN_DIMS, preferred_element_type=f32)


N_TILES = R_ALL // TM


def _x_tile_copies(head_hbm, xp_hbm, xs_hbm, buf, sems, tile, kind):
    slot = tile % 2
    body = TM - P_START
    if kind == "first":
        return [pltpu.make_async_copy(head_hbm, buf.at[slot, pl.ds(0, P_START)], sems.at[slot]),
                pltpu.make_async_copy(xp_hbm.at[pl.ds(0, body)], buf.at[slot, pl.ds(P_START, body)], sems.at[slot])]
    if kind == "middle":
        return [pltpu.make_async_copy(xp_hbm.at[pl.ds(pl.multiple_of(tile * TM - P_START, CHUNK), TM)],
                                      buf.at[slot], sems.at[slot])]
    return [pltpu.make_async_copy(xp_hbm.at[pl.ds(SEQ - body, body)], buf.at[slot, pl.ds(0, body)], sems.at[slot]),
            pltpu.make_async_copy(xs_hbm, buf.at[slot, pl.ds(body, DEC_BATCH)], sems.at[slot])]


def _x_tile_ready(copies, i):
    last = N_TILES - 1

    @pl.when(i == 0)
    def _():
        for cp in copies(i, "first"):
            cp.start()
        for cp in copies(i, "first"):
            cp.wait()
        for cp in copies(i + 1, "middle"):
            cp.start()

    @pl.when((i > 0) & (i < last - 1))
    def _():
        for cp in copies(i, "middle"):
            cp.wait()
        for cp in copies(i + 1, "middle"):
            cp.start()

    @pl.when(i == last - 1)
    def _():
        for cp in copies(i, "middle"):
            cp.wait()
        for cp in copies(i + 1, "last"):
            cp.start()

    @pl.when(i == last)
    def _():
        for cp in copies(i, "last"):
            cp.wait()


def _in_proj_kernel(head_hbm, xp_hbm, xs_hbm, g_ref, w_ref, o_ref, h_ref, x_buf, sems):
    i = pl.program_id(0)

    @pl.when(pl.program_id(1) == 0)
    def _():
        _x_tile_ready(functools.partial(_x_tile_copies, head_hbm, xp_hbm, xs_hbm, x_buf, sems), i)
        h_ref[...] = _rms(x_buf[i % 2], g_ref[...]).astype(bf16)

    o_ref[...] = _dot_nt(h_ref[...], w_ref[...])


def _in_proj(x_head, x_prompt, x_sample, g, w_in_t):
    hbm = pl.BlockSpec(memory_space=pl.ANY)
    return pl.pallas_call(
        _in_proj_kernel,
        grid=(N_TILES, IN_PAD // IN_TN),
        in_specs=[
            hbm, hbm, hbm,
            pl.BlockSpec((1, D_MODEL), lambda i, j: (0, 0)),
            pl.BlockSpec((IN_TN, D_MODEL), lambda i, j: (j, 0)),
        ],
        out_specs=pl.BlockSpec((TM, IN_TN), lambda i, j: (i, j)),
        out_shape=jax.ShapeDtypeStruct((R_ALL, IN_PAD), f32),
        scratch_shapes=[pltpu.VMEM((TM, D_MODEL), bf16), pltpu.VMEM((2, TM, D_MODEL), f32),
                        pltpu.SemaphoreType.DMA((2,))],
        compiler_params=_params(("arbitrary", "arbitrary")),
        name="in_proj",
    )(x_head, x_prompt, x_sample, g, w_in_t)


def _rope_tables(row0, n_rows):
    lane = lax.broadcasted_iota(jnp.int32, (1, LANES), 1)
    half = QK_ROPE // 2
    fidx = (lane & (half - 1)).astype(f32)
    inv = jnp.power(jnp.float32(ROPE_THETA), -fidx / half)
    rows = row0 + lax.broadcasted_iota(jnp.int32, (n_rows, 1), 0)
    pos = jnp.where(rows < P_ROWS, rows - PAD, PAST_LEN).astype(f32)
    ang = pos * inv
    sin = jnp.sin(ang)
    return jnp.cos(ang), jnp.where((lane & half) == 0, -sin, sin)


def _rotate_half(x, cos, sin_signed):
    n = x.shape[1]
    half = QK_ROPE // 2
    first_half = (lax.broadcasted_iota(jnp.int32, (1, n), 1) & half) == 0
    partner = jnp.where(first_half, pltpu.roll(x, n - half, 1), pltpu.roll(x, half, 1))
    return x * cos + partner * sin_signed


def _mla_proj_kernel(qd_ref, kvd_ref, kr_ref, qn_ref, kvn_ref, wq_ref, wkv_ref,
                     qh_ref, kh_ref, vh_ref, ckv_ref, kpe_ref):
    tm = qd_ref.shape[0]
    cos, sin_s = _rope_tables(pl.program_id(0) * tm, tm)

    q = _dot(_rms(qd_ref[...], qn_ref[...]).astype(bf16), wq_ref[...])
    reps = MLA_HEADS * QK_ROPE // LANES
    q_pe = _rotate_half(q[:, MLA_HEADS * QK_NOPE:],
                        jnp.concatenate([cos] * reps, axis=1),
                        jnp.concatenate([sin_s] * reps, axis=1))

    k_pe = _rotate_half(kr_ref[...], cos, sin_s)[:, :QK_ROPE]
    kpe_ref[...] = k_pe

    c_kv = _rms(kvd_ref[...], kvn_ref[...])
    ckv_ref[...] = c_kv
    kv = _dot(c_kv.astype(bf16), wkv_ref[...])

    k_pe_b = k_pe.astype(bf16)
    for h in range(MLA_HEADS):
        qh_ref[h, :, :QK_NOPE] = q[:, h * QK_NOPE:(h + 1) * QK_NOPE].astype(bf16)
        qh_ref[h, :, QK_NOPE:] = q_pe[:, h * QK_ROPE:(h + 1) * QK_ROPE].astype(bf16)
        base = h * (QK_NOPE + V_HEAD)
        kh_ref[h, :, :QK_NOPE] = kv[:, base:base + QK_NOPE].astype(bf16)
        kh_ref[h, :, QK_NOPE:] = k_pe_b
        vh_ref[h] = kv[:, base + QK_NOPE:base + QK_NOPE + V_HEAD].astype(bf16)


def _mla_proj(proj, q_a_norm, kv_a_norm, wq_p, wkv):
    col_q = (4 * HGRN_WIDTH) // Q_RANK
    col_kv = (4 * HGRN_WIDTH + Q_RANK) // KV_RANK
    col_kr = (4 * HGRN_WIDTH + Q_RANK + KV_RANK) // LANES
    head_spec = lambda d: pl.BlockSpec((MLA_HEADS, TM, d), lambda i: (0, i, 0))
    return pl.pallas_call(
        _mla_proj_kernel,
        grid=(R_ALL // TM,),
        in_specs=[
            pl.BlockSpec((TM, Q_RANK), lambda i: (i, col_q)),
            pl.BlockSpec((TM, KV_RANK), lambda i: (i, col_kv)),
            pl.BlockSpec((TM, LANES), lambda i: (i, col_kr)),
            pl.BlockSpec((1, Q_RANK), lambda i: (0, 0)),
            pl.BlockSpec((1, KV_RANK), lambda i: (0, 0)),
            pl.BlockSpec(wq_p.shape, lambda i: (0, 0)),
            pl.BlockSpec(wkv.shape, lambda i: (0, 0)),
        ],
        out_specs=[
            head_spec(QK_DIM), head_spec(QK_DIM), head_spec(V_HEAD),
            pl.BlockSpec((TM, KV_RANK), lambda i: (i, 0)),
            pl.BlockSpec((TM, QK_ROPE), lambda i: (i, 0)),
        ],
        out_shape=[
            jax.ShapeDtypeStruct((MLA_HEADS, R_ALL, QK_DIM), bf16),
            jax.ShapeDtypeStruct((MLA_HEADS, R_ALL, QK_DIM), bf16),
            jax.ShapeDtypeStruct((MLA_HEADS, R_ALL, V_HEAD), bf16),
            jax.ShapeDtypeStruct((R_ALL, KV_RANK), f32),
            jax.ShapeDtypeStruct((R_ALL, QK_ROPE), f32),
        ],
        compiler_params=_params(("arbitrary",)),
        name="mla_proj",
    )(proj, proj, proj, q_a_norm, kv_a_norm, wq_p, wkv)


def _attn_kernel(q_ref, k_ref, v_ref, o_ref, m_ref, l_ref, acc_ref, ta_ref, tb_ref):
    i = pl.program_id(1)
    q = q_ref[0]
    m_ref[...] = jnp.full(m_ref.shape, NEG_BIG, f32)
    l_ref[...] = jnp.zeros(l_ref.shape, f32)
    acc_ref[...] = jnp.zeros(acc_ref.shape, f32)
    reps = TQ // LANES
    groups = range(0, TQ, TQ_SUB)

    def logits(t_ref, j):
        t_ref[...] = _dot_nt(q, k_ref[0, pl.ds(pl.multiple_of(j * TQ, TQ), TQ), :])

    def block(t_ref, j, masked):
        v = v_ref[0, pl.ds(pl.multiple_of(j * TQ, TQ), TQ), :]
        for r0 in groups:
            rows = slice(r0, r0 + TQ_SUB)
            t = t_ref[rows] * (SOFTMAX_SCALE * LOG2_E)
            if masked:
                q_row = i * TQ + r0 + lax.broadcasted_iota(jnp.int32, (TQ_SUB, TQ), 0)
                k_row = j * TQ + lax.broadcasted_iota(jnp.int32, (TQ_SUB, TQ), 1)
                t = jnp.where(k_row <= q_row, jnp.where(k_row >= PAD, t, NEG_BIG), NEG_BIG)
            m_prev = m_ref[rows]
            m_next = jnp.maximum(m_prev, jnp.max(t, axis=1, keepdims=True))
            p = jnp.exp2(t - jnp.concatenate([m_next] * reps, axis=1))
            alpha = jnp.exp2(m_prev - m_next)
            l_ref[rows] = alpha * l_ref[rows] + jnp.sum(p, axis=1, keepdims=True)
            acc_ref[rows] = alpha * acc_ref[rows] + _dot(p.astype(bf16), v)
            m_ref[rows] = m_next

    logits(ta_ref, 0)

    @pl.when(i == 0)
    def _():
        block(ta_ref, 0, True)

    @pl.when(i > 0)
    def _():
        logits(tb_ref, 1)
        block(ta_ref, 0, True)
        n_inner = i - 1

        def pair(p, carry):
            j = 1 + 2 * p
            logits(ta_ref, j + 1)
            block(tb_ref, j, False)
            logits(tb_ref, j + 2)
            block(ta_ref, j + 1, False)
            return carry

        lax.fori_loop(0, n_inner // 2, pair, 0)
        j = 1 + 2 * (n_inner // 2)

        @pl.when(n_inner % 2 == 1)
        def _():
            logits(ta_ref, j + 1)
            block(tb_ref, j, False)
            block(ta_ref, j + 1, True)

        @pl.when(n_inner % 2 == 0)
        def _():
            block(tb_ref, j, True)

    o_ref[...] = (acc_ref[...] / l_ref[...]).astype(bf16)


def _attention(qh, kh, vh):
    return pl.pallas_call(
        _attn_kernel,
        grid=(MLA_HEADS, P_ROWS // TQ),
        in_specs=[
            pl.BlockSpec((1, TQ, QK_DIM), lambda h, i: (h, i, 0)),
            pl.BlockSpec((1, R_ALL, QK_DIM), lambda h, i: (h, 0, 0)),
            pl.BlockSpec((1, R_ALL, V_HEAD), lambda h, i: (h, 0, 0)),
        ],
        out_specs=pl.BlockSpec((TQ, V_HEAD), lambda h, i: (i, h)),
        out_shape=jax.ShapeDtypeStruct((P_ROWS, MLA_WIDTH), bf16),
        scratch_shapes=[pltpu.VMEM((TQ, LANES), f32), pltpu.VMEM((TQ, LANES), f32),
                        pltpu.VMEM((TQ, V_HEAD), f32),
                        pltpu.VMEM((TQ, TQ), f32), pltpu.VMEM((TQ, TQ), f32)],
        compiler_params=_params(("arbitrary", "arbitrary")),
        name="prompt_attn",
    )(qh, kh, vh)


N_LEVELS = 7


def _hgrn_constants():
    t = np.arange(CHUNK)
    sums = (t[None, :] <= t[:, None]).astype(np.float32)
    x = t[:, None] ^ t[None, :]
    lev = np.where(x > 0, np.floor(np.log2(np.maximum(x, 1))).astype(np.int32), N_LEVELS)
    lev = np.where(t[None, :] > t[:, None], N_LEVELS + 1, lev).astype(np.int32)
    return sums, lev


def _split3(x):
    a = x.astype(bf16)
    r = x - a.astype(f32)
    b = r.astype(bf16)
    c = (r - b.astype(f32)).astype(bf16)
    return a, b, c


def _lower_bound(lb_raw_ref):
    a0 = lb_raw_ref[0:1, :]
    a1 = lb_raw_ref[1:2, :]
    m = jnp.maximum(a0, a1)
    e0 = jnp.exp(a0 - m)
    return e0 / (e0 + jnp.exp(a1 - m))


def _pair_block_reference(b, lvl):
    m = 1 << lvl
    n, w = b.shape
    if m >= SUBLANES:
        g = m // SUBLANES
        b4 = b.reshape(n // (2 * m), 2 * g, SUBLANES, w)
        return jnp.broadcast_to(b4[:, g - 1:g, SUBLANES - 1:, :], b4.shape).reshape(n, w)
    b3 = b.reshape(n // SUBLANES, SUBLANES, w)
    sub = lax.broadcasted_iota(jnp.int32, (1, SUBLANES, 1), 1)
    ref = b3[:, m - 1:m, :]
    for start in range(2 * m, SUBLANES, 2 * m):
        ref = jnp.where(sub >= start, b3[:, start + m - 1:start + m, :], ref)
    return jnp.broadcast_to(ref, b3.shape).reshape(n, w)


def _hgrn_prompt_kernel(hq_ref, hf_ref, hi_ref, hg_ref, lb_ref, on_ref, sums_ref, lev_ref,
                        o_ref, s_ref, st_ref):
    c = pl.program_id(0)

    @pl.when(c == 0)
    def _():
        st_ref[...] = jnp.zeros(st_ref.shape, f32)

    @pl.when(c == N_CHUNKS)
    def _():
        o_ref[...] = jnp.zeros(o_ref.shape, bf16)

    @pl.when(c < N_CHUNKS)
    def _():
        lb = _lower_bound(lb_ref)
        f_all = lb + (1.0 - lb) * _sigmoid(hf_ref[...])
        l1, l2, l3 = _split3(jnp.log(f_all))
        sums = sums_ref[...]
        b_all = _dot(sums, l1) + _dot(sums, l2) + _dot(sums, l3)

        row = lax.broadcasted_iota(jnp.int32, (CHUNK, 1), 0)
        lev = lev_ref[...]
        refs = [_pair_block_reference(b_all, lvl) for lvl in range(N_LEVELS)]
        is_q = [((row >> lvl) & 1) == 1 for lvl in range(N_LEVELS)]
        sign = [jnp.where(m, 1.0, -1.0) for m in is_q]
        for h in range(HGRN_HEADS):
            sl = slice(h * HGRN_K, (h + 1) * HGRN_K)
            q = _silu(hq_ref[:, sl])
            k = 1.0 - f_all[:, sl]
            v = hi_ref[:, sl].astype(bf16)
            b = b_all[:, sl]
            a = jnp.where(lev == N_LEVELS, _dot_nt(q.astype(bf16), k.astype(bf16)), 0.0)
            for lvl in range(N_LEVELS):
                w = (jnp.where(is_q[lvl], q, k) * jnp.exp((b - refs[lvl][:, sl]) * sign[lvl])).astype(bf16)
                a = jnp.where(lev == lvl, _dot_nt(w, w), a)

            b_suf = b[CHUNK - 1:CHUNK, :] - b
            st = st_ref[h]
            o = _dot(a.astype(bf16), v) + _dot_nt((q * jnp.exp(b)).astype(bf16), st.astype(bf16))
            st_new = st * jnp.exp(b[CHUNK - 1:CHUNK, :]) + _dot_tn(v, (k * jnp.exp(b_suf)).astype(bf16))
            st_ref[h] = st_new
            o_ref[:, sl] = (_rms(o, on_ref[...]) * _silu(hg_ref[:, sl])).astype(bf16)

    @pl.when(c == N_CHUNKS - 1)
    def _():
        for h in range(HGRN_HEADS):
            s_ref[h] = st_ref[h].T


def _hgrn_prompt(proj, lb_raw, o_norm):
    sums, lev = _hgrn_constants()
    blk = lambda part: pl.BlockSpec((CHUNK, HGRN_WIDTH), lambda c: (c, part))
    return pl.pallas_call(
        _hgrn_prompt_kernel,
        grid=(N_CHUNKS + DEC_BATCH // CHUNK,),
        in_specs=[
            blk(0), blk(1), blk(2), blk(3),
            pl.BlockSpec((2, HGRN_WIDTH), lambda c: (0, 0)),
            pl.BlockSpec((1, HGRN_V), lambda c: (0, 0)),
            pl.BlockSpec(sums.shape, lambda c: (0, 0)),
            pl.BlockSpec(lev.shape, lambda c: (0, 0)),
        ],
        out_specs=[
            pl.BlockSpec((CHUNK, HGRN_WIDTH), lambda c: (c, 0)),
            pl.BlockSpec((HGRN_HEADS, HGRN_K, HGRN_V), lambda c: (0, 0, 0)),
        ],
        out_shape=[
            jax.ShapeDtypeStruct((R_ALL, HGRN_WIDTH), bf16),
            jax.ShapeDtypeStruct((HGRN_HEADS, HGRN_K, HGRN_V), f32),
        ],
        scratch_shapes=[pltpu.VMEM((HGRN_HEADS, HGRN_V, HGRN_K), f32)],
        compiler_params=_params(("arbitrary",)),
        name="hgrn_prompt",
    )(proj, proj, proj, proj, lb_raw, o_norm, jnp.asarray(sums, bf16), jnp.asarray(lev))


def _hgrn_sample_kernel(hq_ref, hf_ref, hi_ref, hg_ref, lb_ref, on_ref, s0_ref, oh_hbm, o_ref, s_ref):
    del oh_hbm
    lb = _lower_bound(lb_ref)
    f = lb + (1.0 - lb) * _sigmoid(hf_ref[...])
    q = _silu(hq_ref[...])
    k = (1.0 - f).astype(bf16).astype(f32)
    v = hi_ref[...].astype(bf16).astype(f32)
    gate = _silu(hg_ref[...])
    f1 = f.astype(bf16).astype(f32)
    f2 = (f - f1).astype(bf16).astype(f32)
    f3 = ((f - f1) - f2).astype(bf16).astype(f32)

    n_rows = 16
    rid = lax.broadcasted_iota(jnp.int32, (n_rows, HGRN_K), 0)
    rid2 = lax.broadcasted_iota(jnp.int32, (n_rows, 2 * HGRN_V), 0)
    left = lax.broadcasted_iota(jnp.int32, (n_rows, 2 * HGRN_V), 1) < HGRN_V
    zero_v = jnp.zeros((1, HGRN_V), f32)

    at = lambda x, s, h: x[s:s + 1, h * HGRN_K:(h + 1) * HGRN_K]

    def mix(s, h):
        lhs = jnp.where(rid == 0, at(f1, s, h), jnp.where(rid == 1, at(f2, s, h), jnp.where(
            rid == 2, at(f3, s, h), jnp.where(rid == 3, at(k, s, h), 0.0))))
        rhs = jnp.where(rid2 < 3, jnp.where(left, 1.0, 0.0),
                        jnp.where(rid2 == 3, jnp.concatenate([zero_v, at(v, s, h)], axis=1), 0.0))
        return _dot_tn(lhs.astype(bf16), rhs.astype(bf16))

    outs = []
    for s0 in range(0, HGRN_SEQS, HGRN_STAGE_SEQS):
        pairs = [(s, h) for s in range(s0, s0 + HGRN_STAGE_SEQS) for h in range(HGRN_HEADS)]
        mixes = [mix(s, h) for s, h in pairs]
        for (s, h), mx in zip(pairs, mixes):
            s_ref[s, h] = mx[:, :HGRN_V] * s0_ref[s, h] + mx[:, HGRN_V:]
        o_raw = [_dot(jnp.where(rid == 0, at(q, s, h), 0.0).astype(bf16), s_ref[s, h].astype(bf16))[0:1]
                 for s, h in pairs]
        outs += [_rms(o, on_ref[...]) * at(gate, s, h) for (s, h), o in zip(pairs, o_raw)]
    o_ref[...] = jnp.concatenate(
        [jnp.concatenate(outs[s * HGRN_HEADS:(s + 1) * HGRN_HEADS], axis=1) for s in range(HGRN_SEQS)],
        axis=0).astype(bf16)


def _hgrn_sample(proj, lb_raw, o_norm, state, o_h):
    row_blk = P_ROWS // HGRN_SEQS
    blk = lambda part: pl.BlockSpec((HGRN_SEQS, HGRN_WIDTH), lambda b: (row_blk + b, part))
    state_spec = pl.BlockSpec((HGRN_SEQS, HGRN_HEADS, HGRN_K, HGRN_V), lambda b: (b, 0, 0, 0))
    return pl.pallas_call(
        _hgrn_sample_kernel,
        grid=(DEC_BATCH // HGRN_SEQS,),
        in_specs=[
            blk(0), blk(1), blk(2), blk(3),
            pl.BlockSpec((2, HGRN_WIDTH), lambda b: (0, 0)),
            pl.BlockSpec((1, HGRN_V), lambda b: (0, 0)),
            state_spec,
            pl.BlockSpec(memory_space=pl.ANY),
        ],
        out_specs=[blk(0), state_spec],
        out_shape=[
            jax.ShapeDtypeStruct(o_h.shape, o_h.dtype),
            jax.ShapeDtypeStruct(state.shape, f32),
        ],
        input_output_aliases={7: 0},
        compiler_params=_params(("arbitrary",)),
        name="hgrn_sample",
    )(proj, proj, proj, proj, lb_raw, o_norm, state, o_h)


def _q_latent_kernel(q_ref, wuk_ref, o_ref):
    o_ref[0] = _dot_nt(q_ref[0][:, :QK_NOPE], wuk_ref[...]).astype(bf16)


def _q_latent(qh, wkv):
    row_blk = P_ROWS // DEC_BATCH
    return pl.pallas_call(
        _q_latent_kernel,
        grid=(MLA_HEADS,),
        in_specs=[
            pl.BlockSpec((1, DEC_BATCH, QK_DIM), lambda h: (h, row_blk, 0)),
            pl.BlockSpec((KV_RANK, QK_NOPE), lambda h: (0, 2 * h)),
        ],
        out_specs=pl.BlockSpec((1, DEC_BATCH, KV_RANK), lambda h: (h, 0, 0)),
        out_shape=jax.ShapeDtypeStruct((MLA_HEADS, DEC_BATCH, KV_RANK), bf16),
        compiler_params=_params(("arbitrary",)),
        name="q_latent",
    )(qh, wkv)


def _page_copies(pt_ref, ckv_hbm, kpe_hbm, ckv_buf, kpe_buf, sems, seq, c):
    copies = []
    for p in range(PAGES_PER_STEP):
        page = pt_ref[seq, c * PAGES_PER_STEP + p]
        copies.append(pltpu.make_async_copy(ckv_hbm.at[0, page], ckv_buf.at[c, p], sems.at[0, c]))
        copies.append(pltpu.make_async_copy(kpe_hbm.at[0, page], kpe_buf.at[c, p], sems.at[1, c]))
    return copies


def _paged_attn_kernel(pt_ref, ql_ref, qp_ref, cn_ref, kn_ref, ckv_hbm, kpe_hbm, o_ref,
                       ckv_buf, kpe_buf, sems, ckb_ref, kpb_ref):
    b = pl.program_id(0)
    copies = functools.partial(_page_copies, pt_ref, ckv_hbm, kpe_hbm, ckv_buf, kpe_buf, sems)
    to_log2 = SOFTMAX_SCALE * LOG2_E

    @pl.when(b == 0)
    def _():
        for c in range(N_PAGE_STEPS):
            for cp in copies(b, c):
                cp.start()

    ql = ql_ref[0]
    qp = qp_ref[0]

    def scores(c):
        for cp in copies(b, c):
            cp.wait()
        half = c % 2
        ckb_ref[half] = ckv_buf[c].reshape(KEYS_PER_STEP, KV_RANK).astype(bf16)
        kpb_ref[half] = jnp.concatenate([kpe_buf[c, p] for p in range(PAGES_PER_STEP)], axis=1).astype(bf16)
        return (_dot_nt(ql, ckb_ref[half]) + _dot(qp, kpb_ref[half])) * to_log2

    t_new = (jnp.sum(ql.astype(f32) * cn_ref[0], axis=1, keepdims=True)
             + jnp.sum(qp.astype(f32) * kn_ref[0], axis=1, keepdims=True)) * to_log2
    m = jnp.broadcast_to(t_new, (MLA_HEADS, LANES))
    l = jnp.ones((MLA_HEADS, LANES), f32)
    acc = jnp.broadcast_to(cn_ref[0], (MLA_HEADS, KV_RANK))

    t = scores(0)
    for c in range(N_PAGE_STEPS):
        t_next = scores(c + 1) if c + 1 < N_PAGE_STEPS else None
        m_next = jnp.maximum(m, jnp.max(t, axis=1, keepdims=True))
        p = jnp.exp2(t - jnp.concatenate([m_next] * (KEYS_PER_STEP // LANES), axis=1))
        alpha = jnp.exp2(m - m_next)
        l = alpha * l + jnp.sum(p, axis=1, keepdims=True)
        acc = jnp.concatenate([alpha] * (KV_RANK // LANES), axis=1) * acc + _dot(p.astype(bf16), ckb_ref[c % 2])
        m = m_next

        @pl.when(b + 1 < pl.num_programs(0))
        def _():
            for cp in copies(b + 1, c):
                cp.start()

        t = t_next

    o_ref[0] = acc / jnp.concatenate([l] * (KV_RANK // LANES), axis=1)


def _paged_attention(page_table, q_lat, q_pe, ckv_new, kpe_new, cache_ckv, cache_krope_t):
    seq_spec = lambda r, d: pl.BlockSpec((1, r, d), lambda b, pt: (b, 0, 0))
    grid_spec = pltpu.PrefetchScalarGridSpec(
        num_scalar_prefetch=1,
        grid=(DEC_BATCH,),
        in_specs=[
            seq_spec(MLA_HEADS, KV_RANK), seq_spec(MLA_HEADS, QK_ROPE),
            seq_spec(1, KV_RANK), seq_spec(1, QK_ROPE),
            pl.BlockSpec(memory_space=pl.ANY), pl.BlockSpec(memory_space=pl.ANY),
        ],
        out_specs=seq_spec(MLA_HEADS, KV_RANK),
        scratch_shapes=[
            pltpu.VMEM((N_PAGE_STEPS, PAGES_PER_STEP, PAGE_SIZE, KV_RANK), f32),
            pltpu.VMEM((N_PAGE_STEPS, PAGES_PER_STEP, QK_ROPE, PAGE_SIZE), f32),
            pltpu.SemaphoreType.DMA((2, N_PAGE_STEPS)),
            pltpu.VMEM((2, KEYS_PER_STEP, KV_RANK), bf16),
            pltpu.VMEM((2, QK_ROPE, KEYS_PER_STEP), bf16),
        ],
    )
    return pl.pallas_call(
        _paged_attn_kernel,
        grid_spec=grid_spec,
        out_shape=jax.ShapeDtypeStruct((DEC_BATCH, MLA_HEADS, KV_RANK), f32),
        compiler_params=_params(("arbitrary",)),
        name="paged_attn",
    )(page_table, q_lat, q_pe, ckv_new, kpe_new, cache_ckv, cache_krope_t)


def _v_up_kernel(o_ref, wuv_ref, out_ref):
    out_ref[...] = _dot(o_ref[0].astype(bf16), wuv_ref[...]).astype(bf16)


def _v_up(o_lat_h, wkv):
    return pl.pallas_call(
        _v_up_kernel,
        grid=(MLA_HEADS,),
        in_specs=[
            pl.BlockSpec((1, DEC_BATCH, KV_RANK), lambda h: (h, 0, 0)),
            pl.BlockSpec((KV_RANK, V_HEAD), lambda h: (0, 2 * h + 1)),
        ],
        out_specs=pl.BlockSpec((DEC_BATCH, V_HEAD), lambda h: (0, h)),
        out_shape=jax.ShapeDtypeStruct((DEC_BATCH, MLA_WIDTH), bf16),
        compiler_params=_params(("arbitrary",)),
        name="v_up",
    )(o_lat_h, wkv)


def _out_proj_kernel(oh_ref, oa_ref, head_hbm, xp_hbm, xs_hbm, wh_ref, wa_ref, g_ref, o_ref, x_buf, sems):
    i = pl.program_id(0)
    _x_tile_ready(functools.partial(_x_tile_copies, head_hbm, xp_hbm, xs_hbm, x_buf, sems), i)
    mix = _dot(oh_ref[...], wh_ref[...]) + _dot(oa_ref[...], wa_ref[...])
    o_ref[...] = x_buf[i % 2] + _rms(mix, g_ref[...])


def _out_proj(o_h, o_a, x_head, x_prompt, x_sample, w_out_b, g):
    hbm = pl.BlockSpec(memory_space=pl.ANY)
    return pl.pallas_call(
        _out_proj_kernel,
        grid=(N_TILES,),
        in_specs=[
            pl.BlockSpec((TM, HGRN_WIDTH), lambda i: (i, 0)),
            pl.BlockSpec((TM, MLA_WIDTH), lambda i: (i, 0)),
            hbm, hbm, hbm,
            pl.BlockSpec((HGRN_WIDTH, D_MODEL), lambda i: (0, 0)),
            pl.BlockSpec((MLA_WIDTH, D_MODEL), lambda i: (1, 0)),
            pl.BlockSpec((1, D_MODEL), lambda i: (0, 0)),
        ],
        out_specs=pl.BlockSpec((TM, D_MODEL), lambda i: (i, 0)),
        out_shape=jax.ShapeDtypeStruct((R_ALL, D_MODEL), f32),
        scratch_shapes=[pltpu.VMEM((2, TM, D_MODEL), f32), pltpu.SemaphoreType.DMA((2,))],
        compiler_params=_params(("arbitrary",)),
        name="out_proj",
    )(o_h, o_a, x_head, x_prompt, x_sample, w_out_b, w_out_b, g)


def _ffn_out_copies(acc_ref, sems, yp_hbm, ys_hbm, tile, slot, kind):
    if kind == "first":
        return [pltpu.make_async_copy(acc_ref.at[slot, pl.ds(P_START, TM - P_START)],
                                      yp_hbm.at[pl.ds(0, TM - P_START)], sems.at[slot])]
    if kind == "middle":
        return [pltpu.make_async_copy(acc_ref.at[slot],
                                      yp_hbm.at[pl.ds(pl.multiple_of(tile * TM - P_START, CHUNK), TM)],
                                      sems.at[slot])]
    return [pltpu.make_async_copy(acc_ref.at[slot, pl.ds(0, SAMPLE_LOCAL)],
                                  yp_hbm.at[pl.ds(SEQ - SAMPLE_LOCAL, SAMPLE_LOCAL)], sems.at[slot]),
            pltpu.make_async_copy(acc_ref.at[slot, pl.ds(SAMPLE_LOCAL, DEC_BATCH)], ys_hbm, sems.at[slot])]


def _ffn_kernel(x_ref, gpre_ref, gpost_ref, wg_ref, wv_ref, cw_ref, cb_ref, wd_ref, buf0_ref, buf1_ref,
                yp_hbm, ys_hbm, tail_ref, h_ref, act_ref, carry_ref, acc_ref, sems):
    i = pl.program_id(0)
    j = pl.program_id(1)
    last_i = pl.num_programs(0) - 1
    slot = i % 2
    col = pl.ds(pl.multiple_of(j * FF_TN, FF_TN), FF_TN)
    out_copies = functools.partial(_ffn_out_copies, acc_ref, sems, yp_hbm, ys_hbm)

    @pl.when(j == 0)
    def _():
        @pl.when(i == 2)
        def _():
            for cp in out_copies(0, slot, "first"):
                cp.wait()

        @pl.when(i > 2)
        def _():
            for cp in out_copies(i - 2, slot, "middle"):
                cp.wait()

        h_ref[...] = _rms(x_ref[...], gpre_ref[...]).astype(bf16)
        acc_ref[slot] = jnp.zeros((TM, D_MODEL), f32)

    @pl.when(i == 0)
    def _():
        carry_ref[:, col] = jnp.zeros((8, FF_TN), f32)

    h = h_ref[...]
    row = lax.broadcasted_iota(jnp.int32, (TM, 1), 0)
    g = jnp.where(row + i * TM >= PAD, _dot(h, wg_ref[...]), 0.0)
    val = _dot(h, wv_ref[...])
    prev = carry_ref[:, col]
    g1 = jnp.where(row == 0, prev[7:8], pltpu.roll(g, 1, 0))
    g2 = jnp.where(row == 0, prev[6:7], jnp.where(row == 1, prev[7:8], pltpu.roll(g, 2, 0)))
    w0, w1, w2 = cw_ref[0:1, :], cw_ref[1:2, :], cw_ref[2:3, :]
    y = cb_ref[...] + w0 * g2 + w1 * g1 + w2 * g
    act_ref[...] = (_silu(y) * val).astype(bf16)
    carry_ref[:, col] = g[TM - 8:, :]
    tail_ref[0] = g[TM - TAIL_ROWS:, :]

    @pl.when(i == last_i)
    def _():
        ys = cb_ref[...] + w0 * buf0_ref[...] + w1 * buf1_ref[...] + w2 * g[SAMPLE_LOCAL:, :]
        act_ref[SAMPLE_LOCAL:, :] = (_silu(ys) * val[SAMPLE_LOCAL:, :]).astype(bf16)

    acc_ref[slot] += _dot(act_ref[...], wd_ref[...])

    @pl.when(j == pl.num_programs(1) - 1)
    def _():
        acc_ref[slot] = x_ref[...] + _rms(acc_ref[slot], gpost_ref[...])

        @pl.when(i == 0)
        def _():
            for cp in out_copies(i, slot, "first"):
                cp.start()

        @pl.when((i > 0) & (i < last_i))
        def _():
            for cp in out_copies(i, slot, "middle"):
                cp.start()

        @pl.when(i == last_i)
        def _():
            for cp in out_copies(i, slot, "last"):
                cp.start()
            for cp in out_copies(i - 1, 1 - slot, "middle"):
                cp.wait()
            for cp in out_copies(i, slot, "last"):
                cp.wait()


def _ffn(x1, g_pre, g_post, w_up_b, conv_w, conv_b, w_down_b, buf0, buf1):
    n_ff = D_FF // FF_TN
    n_tiles = R_ALL // TM
    sample_buf = pl.BlockSpec((DEC_BATCH, FF_TN), lambda i, j: (0, jnp.where(i == n_tiles - 1, j, 0)))
    return pl.pallas_call(
        _ffn_kernel,
        grid=(R_ALL // TM, n_ff),
        in_specs=[
            pl.BlockSpec((TM, D_MODEL), lambda i, j: (i, 0)),
            pl.BlockSpec((1, D_MODEL), lambda i, j: (0, 0)),
            pl.BlockSpec((1, D_MODEL), lambda i, j: (0, 0)),
            pl.BlockSpec((D_MODEL, FF_TN), lambda i, j: (0, j)),
            pl.BlockSpec((D_MODEL, FF_TN), lambda i, j: (0, j + n_ff)),
            pl.BlockSpec((3, FF_TN), lambda i, j: (0, j)),
            pl.BlockSpec((1, FF_TN), lambda i, j: (0, j)),
            pl.BlockSpec((FF_TN, D_MODEL), lambda i, j: (j, 0)),
            sample_buf, sample_buf,
        ],
        out_specs=[
            pl.BlockSpec(memory_space=pl.ANY),
            pl.BlockSpec(memory_space=pl.ANY),
            pl.BlockSpec((1, TAIL_ROWS, FF_TN), lambda i, j: (i, 0, j)),
        ],
        out_shape=[
            jax.ShapeDtypeStruct((SEQ, D_MODEL), f32),
            jax.ShapeDtypeStruct((DEC_BATCH, D_MODEL), f32),
            jax.ShapeDtypeStruct((R_ALL // TM, TAIL_ROWS, D_FF), f32),
        ],
        scratch_shapes=[
            pltpu.VMEM((TM, D_MODEL), bf16),
            pltpu.VMEM((TM, FF_TN), bf16),
            pltpu.VMEM((8, D_FF), f32),
            pltpu.VMEM((2, TM, D_MODEL), f32),
            pltpu.SemaphoreType.DMA((2,)),
        ],
        compiler_params=_params(("arbitrary", "arbitrary")),
        name="conv_ffn",
    )(x1, g_pre, g_post, w_up_b, w_up_b, conv_w, conv_b, w_down_b, buf0, buf1)


def kernel(x_prompt, x_sample, cache_ckv, cache_krope, state_hgrn, state_conv, page_table, meta_tokens,
           lb_raw, g_mix_pre, g_mix_post, g_ffn_pre, g_ffn_post, w_in, hgrn_o_norm, q_a_norm, kv_a_norm,
           w_q_up, w_kv_up, w_out, w_ffn_up, conv_w, conv_b, w_ffn_down):
    x_head = jnp.concatenate([jnp.zeros((PAD, D_MODEL), f32), meta_tokens.astype(f32)], axis=0)
    x_p = x_prompt[0]
    x_s = x_sample[:, 0]

    w_in_t = jnp.concatenate([w_in[0].T.astype(bf16), jnp.zeros((IN_PAD - IN_TOTAL, D_MODEL), bf16)], axis=0)
    wq = w_q_up[0].reshape(Q_RANK, MLA_HEADS, QK_DIM)
    wq_p = jnp.concatenate([wq[:, :, :QK_NOPE].reshape(Q_RANK, -1), wq[:, :, QK_NOPE:].reshape(Q_RANK, -1)],
                           axis=1).astype(bf16)
    wkv = w_kv_up[0].astype(bf16)
    w_out_b = w_out[0].astype(bf16)
    w_up_b = w_ffn_up[0].astype(bf16)
    w_down_b = w_ffn_down[0].astype(bf16)

    proj = _in_proj(x_head, x_p, x_s, g_mix_pre, w_in_t)
    qh, kh, vh, ckv, kpe = _mla_proj(proj, q_a_norm, kv_a_norm, wq_p, wkv)

    o_a = _attention(qh, kh, vh)
    o_h, s_p = _hgrn_prompt(proj, lb_raw, hgrn_o_norm)

    o_h, s_s = _hgrn_sample(proj, lb_raw, hgrn_o_norm, state_hgrn[0], o_h)
    q_lat = _q_latent(qh, wkv).transpose(1, 0, 2)
    q_pe_s = qh[:, P_ROWS:, QK_NOPE:].transpose(1, 0, 2)
    ckv_s = ckv[P_ROWS:]
    kpe_s = kpe[P_ROWS:]
    o_lat = _paged_attention(page_table, q_lat, q_pe_s, ckv_s[:, None, :], kpe_s[:, None, :],
                             cache_ckv, cache_krope.transpose(0, 1, 3, 2))
    o_a = jnp.concatenate([o_a, _v_up(o_lat.transpose(1, 0, 2), wkv)], axis=0)

    x1 = _out_proj(o_h, o_a, x_head, x_p, x_s, w_out_b, g_mix_post)
    y_p, y_s, tails = _ffn(x1, g_ffn_pre, g_ffn_post, w_up_b, conv_w[0], conv_b, w_down_b,
                           state_conv[0, :, 0], state_conv[0, :, 1])
    tail = tails[-1]

    n_tail = TAIL_ROWS - DEC_BATCH
    conv_p = tail[n_tail - 2:n_tail][None, None]
    conv_s = jnp.stack([state_conv[0, :, 1], tail[n_tail:]], axis=1)[None]
    return (y_p[None], y_s[:, None],
            ckv[PAD:P_ROWS][None, None], kpe[PAD:P_ROWS][None, None],
            ckv_s[None, :, None], kpe_s[None, :, None],
            s_p[None, None], s_s[None], conv_p, conv_s)
```

```python
import functools

import numpy as np
import jax
import jax.numpy as jnp
from jax import lax
from jax.experimental import pallas as pl
from jax.experimental.pallas import tpu as pltpu

D_MODEL = 2048
SEQ = 8192
N_META = 16
DEC_BATCH = 128
PAST_LEN = 16384
PAGE_SIZE = 128
N_PAGES = PAST_LEN // PAGE_SIZE
HGRN_HEADS = 8
HGRN_K = 128
HGRN_V = 128
HGRN_WIDTH = HGRN_HEADS * HGRN_V
MLA_HEADS = 8
QK_NOPE = 128
QK_ROPE = 64
QK_DIM = QK_NOPE + QK_ROPE
V_HEAD = 128
Q_RANK = 512
KV_RANK = 256
MLA_WIDTH = MLA_HEADS * V_HEAD
ROPE_THETA = 10000.0
SOFTMAX_SCALE = QK_DIM ** -0.5
LOG2_E = 1.4426950408889634
D_FF = 5632
EPS = 1e-6
IN_TOTAL = 4 * HGRN_WIDTH + Q_RANK + KV_RANK + QK_ROPE

CHUNK = 128
PAD = CHUNK - N_META
P_START = PAD + N_META
P_ROWS = P_START + SEQ
R_ALL = P_ROWS + DEC_BATCH
N_CHUNKS = P_ROWS // CHUNK

LANES = 128
SUBLANES = 8
IN_PAD = 5120
IN_TN = 1280
IN_FULL_TILES = IN_TOTAL // IN_TN
TM = 768
TQ = 640
TQ_SUB = 640
FF_TN = 512
HGRN_SEQS = 16
HGRN_STAGE_SEQS = 8
SAMPLE_LOCAL = P_ROWS - (R_ALL // TM - 1) * TM
TAIL_ROWS = 136
PAGES_PER_STEP = 16
N_PAGE_STEPS = N_PAGES // PAGES_PER_STEP
KEYS_PER_STEP = PAGES_PER_STEP * PAGE_SIZE
NEG_BIG = -1e30
VMEM_LIMIT = 56 * 1024 * 1024

NT_DIMS = (((1,), (1,)), ((), ()))
TN_DIMS = (((0,), (0,)), ((), ()))

f32 = jnp.float32
bf16 = jnp.bfloat16


def _params(sem, vmem=VMEM_LIMIT):
    return pltpu.CompilerParams(dimension_semantics=sem, vmem_limit_bytes=vmem)


def _rms(x, g):
    return x * lax.rsqrt(jnp.mean(x * x, axis=-1, keepdims=True) + EPS) * g


def _sigmoid(x):
    return 1.0 / (1.0 + jnp.exp(-x))


def _silu(x):
    return x * _sigmoid(x)


def _dot(a, b):
    return jnp.dot(a, b, preferred_element_type=f32)


def _dot_nt(a, b):
    return lax.dot_general(a, b, NT_DIMS, preferred_element_type=f32)


def _dot_tn(a, b):
    return lax.dot_general(a, b, TN_DIMS, preferred_element_type=f32)


N_TILES = R_ALL // TM


def _x_tile_copies(head_hbm, xp_hbm, xs_hbm, buf, sems, tile, kind):
    slot = tile % 2
    body = TM - P_START
    if kind == "first":
        return [pltpu.make_async_copy(head_hbm, buf.at[slot, pl.ds(0, P_START)], sems.at[slot]),
                pltpu.make_async_copy(xp_hbm.at[pl.ds(0, body)], buf.at[slot, pl.ds(P_START, body)], sems.at[slot])]
    if kind == "middle":
        return [pltpu.make_async_copy(xp_hbm.at[pl.ds(pl.multiple_of(tile * TM - P_START, CHUNK), TM)],
                                      buf.at[slot], sems.at[slot])]
    return [pltpu.make_async_copy(xp_hbm.at[pl.ds(SEQ - body, body)], buf.at[slot, pl.ds(0, body)], sems.at[slot]),
            pltpu.make_async_copy(xs_hbm, buf.at[slot, pl.ds(body, DEC_BATCH)], sems.at[slot])]


def _x_tile_ready(copies, i):
    last = N_TILES - 1

    @pl.when(i == 0)
    def _():
        for cp in copies(i, "first"):
            cp.start()
        for cp in copies(i, "first"):
            cp.wait()
        for cp in copies(i + 1, "middle"):
            cp.start()

    @pl.when((i > 0) & (i < last - 1))
    def _():
        for cp in copies(i, "middle"):
            cp.wait()
        for cp in copies(i + 1, "middle"):
            cp.start()

    @pl.when(i == last - 1)
    def _():
        for cp in copies(i, "middle"):
            cp.wait()
        for cp in copies(i + 1, "last"):
            cp.start()

    @pl.when(i == last)
    def _():
        for cp in copies(i, "last"):
            cp.wait()


def _in_proj_kernel(head_hbm, xp_hbm, xs_hbm, g_ref, w_ref, wtail_ref, o_ref, h_ref, x_buf, sems):
    i = pl.program_id(0)
    j = pl.program_id(1)

    @pl.when(j == 0)
    def _():
        _x_tile_ready(functools.partial(_x_tile_copies, head_hbm, xp_hbm, xs_hbm, x_buf, sems), i)
        h_ref[...] = _rms(x_buf[i % 2], g_ref[...]).astype(bf16)

    @pl.when(j < IN_FULL_TILES)
    def _():
        o_ref[...] = _dot_nt(h_ref[...], w_ref[...])

    @pl.when(j >= IN_FULL_TILES)
    def _():
        o_ref[...] = _dot_nt(h_ref[...], wtail_ref[...])


def _in_proj(x_head, x_prompt, x_sample, g, w_in_t, w_in_tail):
    hbm = pl.BlockSpec(memory_space=pl.ANY)
    return pl.pallas_call(
        _in_proj_kernel,
        grid=(N_TILES, IN_PAD // IN_TN),
        in_specs=[
            hbm, hbm, hbm,
            pl.BlockSpec((1, D_MODEL), lambda i, j: (0, 0)),
            pl.BlockSpec((IN_TN, D_MODEL), lambda i, j: (jnp.minimum(j, IN_FULL_TILES - 1), 0)),
            pl.BlockSpec((IN_TN, D_MODEL), lambda i, j: (0, 0)),
        ],
        out_specs=pl.BlockSpec((TM, IN_TN), lambda i, j: (i, j)),
        out_shape=jax.ShapeDtypeStruct((R_ALL, IN_PAD), f32),
        scratch_shapes=[pltpu.VMEM((TM, D_MODEL), bf16), pltpu.VMEM((2, TM, D_MODEL), f32),
                        pltpu.SemaphoreType.DMA((2,))],
        compiler_params=_params(("arbitrary", "arbitrary")),
        name="in_proj",
    )(x_head, x_prompt, x_sample, g, w_in_t, w_in_tail)


def _rope_tables(row0, n_rows):
    lane = lax.broadcasted_iota(jnp.int32, (1, LANES), 1)
    half = QK_ROPE // 2
    fidx = (lane & (half - 1)).astype(f32)
    inv = jnp.power(jnp.float32(ROPE_THETA), -fidx / half)
    rows = row0 + lax.broadcasted_iota(jnp.int32, (n_rows, 1), 0)
    pos = jnp.where(rows < P_ROWS, rows - PAD, PAST_LEN).astype(f32)
    ang = pos * inv
    sin = jnp.sin(ang)
    return jnp.cos(ang), jnp.where((lane & half) == 0, -sin, sin)


def _rotate_half(x, cos, sin_signed):
    n = x.shape[1]
    half = QK_ROPE // 2
    first_half = (lax.broadcasted_iota(jnp.int32, (1, n), 1) & half) == 0
    partner = jnp.where(first_half, pltpu.roll(x, n - half, 1), pltpu.roll(x, half, 1))
    return x * cos + partner * sin_signed


def _mla_proj_kernel(qd_ref, kvd_ref, kr_ref, qn_ref, kvn_ref, wq_ref, wkv_ref,
                     qh_ref, kh_ref, vh_ref, ckv_ref, kpe_ref):
    tm = qd_ref.shape[0]
    cos, sin_s = _rope_tables(pl.program_id(0) * tm, tm)

    q = _dot(_rms(qd_ref[...], qn_ref[...]).astype(bf16), wq_ref[...])
    reps = MLA_HEADS * QK_ROPE // LANES
    q_pe = _rotate_half(q[:, MLA_HEADS * QK_NOPE:],
                        jnp.concatenate([cos] * reps, axis=1),
                        jnp.concatenate([sin_s] * reps, axis=1))

    k_pe = _rotate_half(kr_ref[...], cos, sin_s)[:, :QK_ROPE]
    kpe_ref[...] = k_pe

    c_kv = _rms(kvd_ref[...], kvn_ref[...])
    ckv_ref[...] = c_kv
    kv = _dot(c_kv.astype(bf16), wkv_ref[...])

    k_pe_b = k_pe.astype(bf16)
    for h in range(MLA_HEADS):
        qh_ref[h, :, :QK_NOPE] = q[:, h * QK_NOPE:(h + 1) * QK_NOPE].astype(bf16)
        qh_ref[h, :, QK_NOPE:] = q_pe[:, h * QK_ROPE:(h + 1) * QK_ROPE].astype(bf16)
        base = h * (QK_NOPE + V_HEAD)
        kh_ref[h, :, :QK_NOPE] = kv[:, base:base + QK_NOPE].astype(bf16)
        kh_ref[h, :, QK_NOPE:] = k_pe_b
        vh_ref[h] = kv[:, base + QK_NOPE:base + QK_NOPE + V_HEAD].astype(bf16)


def _mla_proj(proj, q_a_norm, kv_a_norm, wq_p, wkv):
    col_q = (4 * HGRN_WIDTH) // Q_RANK
    col_kv = (4 * HGRN_WIDTH + Q_RANK) // KV_RANK
    col_kr = (4 * HGRN_WIDTH + Q_RANK + KV_RANK) // LANES
    head_spec = lambda d: pl.BlockSpec((MLA_HEADS, TM, d), lambda i: (0, i, 0))
    return pl.pallas_call(
        _mla_proj_kernel,
        grid=(R_ALL // TM,),
        in_specs=[
            pl.BlockSpec((TM, Q_RANK), lambda i: (i, col_q)),
            pl.BlockSpec((TM, KV_RANK), lambda i: (i, col_kv)),
            pl.BlockSpec((TM, LANES), lambda i: (i, col_kr)),
            pl.BlockSpec((1, Q_RANK), lambda i: (0, 0)),
            pl.BlockSpec((1, KV_RANK), lambda i: (0, 0)),
            pl.BlockSpec(wq_p.shape, lambda i: (0, 0)),
            pl.BlockSpec(wkv.shape, lambda i: (0, 0)),
        ],
        out_specs=[
            head_spec(QK_DIM), head_spec(QK_DIM), head_spec(V_HEAD),
            pl.BlockSpec((TM, KV_RANK), lambda i: (i, 0)),
            pl.BlockSpec((TM, QK_ROPE), lambda i: (i, 0)),
        ],
        out_shape=[
            jax.ShapeDtypeStruct((MLA_HEADS, R_ALL, QK_DIM), bf16),
            jax.ShapeDtypeStruct((MLA_HEADS, R_ALL, QK_DIM), bf16),
            jax.ShapeDtypeStruct((MLA_HEADS, R_ALL, V_HEAD), bf16),
            jax.ShapeDtypeStruct((R_ALL, KV_RANK), f32),
            jax.ShapeDtypeStruct((R_ALL, QK_ROPE), f32),
        ],
        compiler_params=_params(("arbitrary",)),
        name="mla_proj",
    )(proj, proj, proj, q_a_norm, kv_a_norm, wq_p, wkv)


def _attn_kernel(q_ref, k_ref, v_ref, o_ref, m_ref, l_ref, acc_ref, ta_ref, tb_ref):
    i = pl.program_id(1)
    q = q_ref[0]
    m_ref[...] = jnp.full(m_ref.shape, NEG_BIG, f32)
    l_ref[...] = jnp.zeros(l_ref.shape, f32)
    acc_ref[...] = jnp.zeros(acc_ref.shape, f32)
    reps = TQ // LANES
    groups = range(0, TQ, TQ_SUB)

    def logits(t_ref, j):
        t_ref[...] = _dot_nt(q, k_ref[0, pl.ds(pl.multiple_of(j * TQ, TQ), TQ), :])

    def block(t_ref, j, masked):
        v = v_ref[0, pl.ds(pl.multiple_of(j * TQ, TQ), TQ), :]
        for r0 in groups:
            rows = slice(r0, r0 + TQ_SUB)
            t = t_ref[rows] * (SOFTMAX_SCALE * LOG2_E)
            if masked:
                q_row = i * TQ + r0 + lax.broadcasted_iota(jnp.int32, (TQ_SUB, TQ), 0)
                k_row = j * TQ + lax.broadcasted_iota(jnp.int32, (TQ_SUB, TQ), 1)
                t = jnp.where(k_row <= q_row, jnp.where(k_row >= PAD, t, NEG_BIG), NEG_BIG)
            m_prev = m_ref[rows]
            m_next = jnp.maximum(m_prev, jnp.max(t, axis=1, keepdims=True))
            p = jnp.exp2(t - jnp.concatenate([m_next] * reps, axis=1))
            alpha = jnp.exp2(m_prev - m_next)
            l_ref[rows] = alpha * l_ref[rows] + jnp.sum(p, axis=1, keepdims=True)
            acc_ref[rows] = alpha * acc_ref[rows] + _dot(p.astype(bf16), v)
            m_ref[rows] = m_next

    logits(ta_ref, 0)

    @pl.when(i == 0)
    def _():
        block(ta_ref, 0, True)

    @pl.when(i > 0)
    def _():
        logits(tb_ref, 1)
        block(ta_ref, 0, True)
        n_inner = i - 1

        def pair(p, carry):
            j = 1 + 2 * p
            logits(ta_ref, j + 1)
            block(tb_ref, j, False)
            logits(tb_ref, j + 2)
            block(ta_ref, j + 1, False)
            return carry

        lax.fori_loop(0, n_inner // 2, pair, 0)
        j = 1 + 2 * (n_inner // 2)

        @pl.when(n_inner % 2 == 1)
        def _():
            logits(ta_ref, j + 1)
            block(tb_ref, j, False)
            block(ta_ref, j + 1, True)

        @pl.when(n_inner % 2 == 0)
        def _():
            block(tb_ref, j, True)

    o_ref[...] = (acc_ref[...] / l_ref[...]).astype(bf16)


def _attention(qh, kh, vh):
    return pl.pallas_call(
        _attn_kernel,
        grid=(MLA_HEADS, P_ROWS // TQ),
        in_specs=[
            pl.BlockSpec((1, TQ, QK_DIM), lambda h, i: (h, i, 0)),
            pl.BlockSpec((1, R_ALL, QK_DIM), lambda h, i: (h, 0, 0)),
            pl.BlockSpec((1, R_ALL, V_HEAD), lambda h, i: (h, 0, 0)),
        ],
        out_specs=pl.BlockSpec((TQ, V_HEAD), lambda h, i: (i, h)),
        out_shape=jax.ShapeDtypeStruct((P_ROWS, MLA_WIDTH), bf16),
        scratch_shapes=[pltpu.VMEM((TQ, LANES), f32), pltpu.VMEM((TQ, LANES), f32),
                        pltpu.VMEM((TQ, V_HEAD), f32),
                        pltpu.VMEM((TQ, TQ), f32), pltpu.VMEM((TQ, TQ), f32)],
        compiler_params=_params(("arbitrary", "arbitrary")),
        name="prompt_attn",
    )(qh, kh, vh)


N_LEVELS = 7


def _hgrn_constants():
    t = np.arange(CHUNK)
    sums = (t[None, :] <= t[:, None]).astype(np.float32)
    x = t[:, None] ^ t[None, :]
    lev = np.where(x > 0, np.floor(np.log2(np.maximum(x, 1))).astype(np.int32), N_LEVELS)
    lev = np.where(t[None, :] > t[:, None], N_LEVELS + 1, lev).astype(np.int32)
    return sums, lev


def _split3(x):
    a = x.astype(bf16)
    r = x - a.astype(f32)
    b = r.astype(bf16)
    c = (r - b.astype(f32)).astype(bf16)
    return a, b, c


def _lower_bound(lb_raw_ref):
    a0 = lb_raw_ref[0:1, :]
    a1 = lb_raw_ref[1:2, :]
    m = jnp.maximum(a0, a1)
    e0 = jnp.exp(a0 - m)
    return e0 / (e0 + jnp.exp(a1 - m))


def _pair_block_reference(b, lvl):
    m = 1 << lvl
    n, w = b.shape
    if m >= SUBLANES:
        g = m // SUBLANES
        b4 = b.reshape(n // (2 * m), 2 * g, SUBLANES, w)
        return jnp.broadcast_to(b4[:, g - 1:g, SUBLANES - 1:, :], b4.shape).reshape(n, w)
    b3 = b.reshape(n // SUBLANES, SUBLANES, w)
    sub = lax.broadcasted_iota(jnp.int32, (1, SUBLANES, 1), 1)
    ref = b3[:, m - 1:m, :]
    for start in range(2 * m, SUBLANES, 2 * m):
        ref = jnp.where(sub >= start, b3[:, start + m - 1:start + m, :], ref)
    return jnp.broadcast_to(ref, b3.shape).reshape(n, w)


def _hgrn_prompt_kernel(hq_ref, hf_ref, hi_ref, hg_ref, lb_ref, on_ref, sums_ref, lev_ref,
                        o_ref, s_ref, st_ref):
    c = pl.program_id(0)

    @pl.when(c == 0)
    def _():
        st_ref[...] = jnp.zeros(st_ref.shape, f32)

    @pl.when(c == N_CHUNKS)
    def _():
        o_ref[...] = jnp.zeros(o_ref.shape, bf16)

    @pl.when(c < N_CHUNKS)
    def _():
        lb = _lower_bound(lb_ref)
        f_all = lb + (1.0 - lb) * _sigmoid(hf_ref[...])
        l1, l2, l3 = _split3(jnp.log(f_all))
        sums = sums_ref[...]
        b_all = _dot(sums, l1) + _dot(sums, l2) + _dot(sums, l3)

        row = lax.broadcasted_iota(jnp.int32, (CHUNK, 1), 0)
        lev = lev_ref[...]
        refs = [_pair_block_reference(b_all, lvl) for lvl in range(N_LEVELS)]
        is_q = [((row >> lvl) & 1) == 1 for lvl in range(N_LEVELS)]
        sign = [jnp.where(m, 1.0, -1.0) for m in is_q]
        for h in range(HGRN_HEADS):
            sl = slice(h * HGRN_K, (h + 1) * HGRN_K)
            q = _silu(hq_ref[:, sl])
            k = 1.0 - f_all[:, sl]
            v = hi_ref[:, sl].astype(bf16)
            b = b_all[:, sl]
            a = jnp.where(lev == N_LEVELS, _dot_nt(q.astype(bf16), k.astype(bf16)), 0.0)
            for lvl in range(N_LEVELS):
                w = (jnp.where(is_q[lvl], q, k) * jnp.exp((b - refs[lvl][:, sl]) * sign[lvl])).astype(bf16)
                a = jnp.where(lev == lvl, _dot_nt(w, w), a)

            b_suf = b[CHUNK - 1:CHUNK, :] - b
            st = st_ref[h]
            o = _dot(a.astype(bf16), v) + _dot_nt((q * jnp.exp(b)).astype(bf16), st.astype(bf16))
            st_new = st * jnp.exp(b[CHUNK - 1:CHUNK, :]) + _dot_tn(v, (k * jnp.exp(b_suf)).astype(bf16))
            st_ref[h] = st_new
            o_ref[:, sl] = (_rms(o, on_ref[...]) * _silu(hg_ref[:, sl])).astype(bf16)

    @pl.when(c == N_CHUNKS - 1)
    def _():
        for h in range(HGRN_HEADS):
            s_ref[h] = st_ref[h].T


def _hgrn_prompt(proj, lb_raw, o_norm):
    sums, lev = _hgrn_constants()
    blk = lambda part: pl.BlockSpec((CHUNK, HGRN_WIDTH), lambda c: (c, part))
    return pl.pallas_call(
        _hgrn_prompt_kernel,
        grid=(N_CHUNKS + DEC_BATCH // CHUNK,),
        in_specs=[
            blk(0), blk(1), blk(2), blk(3),
            pl.BlockSpec((2, HGRN_WIDTH), lambda c: (0, 0)),
            pl.BlockSpec((1, HGRN_V), lambda c: (0, 0)),
            pl.BlockSpec(sums.shape, lambda c: (0, 0)),
            pl.BlockSpec(lev.shape, lambda c: (0, 0)),
        ],
        out_specs=[
            pl.BlockSpec((CHUNK, HGRN_WIDTH), lambda c: (c, 0)),
            pl.BlockSpec((HGRN_HEADS, HGRN_K, HGRN_V), lambda c: (0, 0, 0)),
        ],
        out_shape=[
            jax.ShapeDtypeStruct((R_ALL, HGRN_WIDTH), bf16),
            jax.ShapeDtypeStruct((HGRN_HEADS, HGRN_K, HGRN_V), f32),
        ],
        scratch_shapes=[pltpu.VMEM((HGRN_HEADS, HGRN_V, HGRN_K), f32)],
        compiler_params=_params(("arbitrary",)),
        name="hgrn_prompt",
    )(proj, proj, proj, proj, lb_raw, o_norm, jnp.asarray(sums, bf16), jnp.asarray(lev))


def _hgrn_sample_kernel(hq_ref, hf_ref, hi_ref, hg_ref, lb_ref, on_ref, s0_ref, oh_hbm, o_ref, s_ref):
    del oh_hbm
    lb = _lower_bound(lb_ref)
    f = lb + (1.0 - lb) * _sigmoid(hf_ref[...])
    q = _silu(hq_ref[...])
    k = (1.0 - f).astype(bf16).astype(f32)
    v = hi_ref[...].astype(bf16).astype(f32)
    gate = _silu(hg_ref[...])
    f1 = f.astype(bf16).astype(f32)
    f2 = (f - f1).astype(bf16).astype(f32)
    f3 = ((f - f1) - f2).astype(bf16).astype(f32)

    n_rows = 16
    rid = lax.broadcasted_iota(jnp.int32, (n_rows, HGRN_K), 0)
    rid2 = lax.broadcasted_iota(jnp.int32, (n_rows, 2 * HGRN_V), 0)
    left = lax.broadcasted_iota(jnp.int32, (n_rows, 2 * HGRN_V), 1) < HGRN_V
    zero_v = jnp.zeros((1, HGRN_V), f32)

    at = lambda x, s, h: x[s:s + 1, h * HGRN_K:(h + 1) * HGRN_K]

    def mix(s, h):
        lhs = jnp.where(rid == 0, at(f1, s, h), jnp.where(rid == 1, at(f2, s, h), jnp.where(
            rid == 2, at(f3, s, h), jnp.where(rid == 3, at(k, s, h), 0.0))))
        rhs = jnp.where(rid2 < 3, jnp.where(left, 1.0, 0.0),
                        jnp.where(rid2 == 3, jnp.concatenate([zero_v, at(v, s, h)], axis=1), 0.0))
        return _dot_tn(lhs.astype(bf16), rhs.astype(bf16))

    outs = []
    for s0 in range(0, HGRN_SEQS, HGRN_STAGE_SEQS):
        pairs = [(s, h) for s in range(s0, s0 + HGRN_STAGE_SEQS) for h in range(HGRN_HEADS)]
        mixes = [mix(s, h) for s, h in pairs]
        for (s, h), mx in zip(pairs, mixes):
            s_ref[s, h] = mx[:, :HGRN_V] * s0_ref[s, h] + mx[:, HGRN_V:]
        o_raw = [_dot(jnp.where(rid == 0, at(q, s, h), 0.0).astype(bf16), s_ref[s, h].astype(bf16))[0:1]
                 for s, h in pairs]
        outs += [_rms(o, on_ref[...]) * at(gate, s, h) for (s, h), o in zip(pairs, o_raw)]
    o_ref[...] = jnp.concatenate(
        [jnp.concatenate(outs[s * HGRN_HEADS:(s + 1) * HGRN_HEADS], axis=1) for s in range(HGRN_SEQS)],
        axis=0).astype(bf16)


def _hgrn_sample(proj, lb_raw, o_norm, state, o_h):
    row_blk = P_ROWS // HGRN_SEQS
    blk = lambda part: pl.BlockSpec((HGRN_SEQS, HGRN_WIDTH), lambda b: (row_blk + b, part))
    state_spec = pl.BlockSpec((HGRN_SEQS, HGRN_HEADS, HGRN_K, HGRN_V), lambda b: (b, 0, 0, 0))
    return pl.pallas_call(
        _hgrn_sample_kernel,
        grid=(DEC_BATCH // HGRN_SEQS,),
        in_specs=[
            blk(0), blk(1), blk(2), blk(3),
            pl.BlockSpec((2, HGRN_WIDTH), lambda b: (0, 0)),
            pl.BlockSpec((1, HGRN_V), lambda b: (0, 0)),
            state_spec,
            pl.BlockSpec(memory_space=pl.ANY),
        ],
        out_specs=[blk(0), state_spec],
        out_shape=[
            jax.ShapeDtypeStruct(o_h.shape, o_h.dtype),
            jax.ShapeDtypeStruct(state.shape, f32),
        ],
        input_output_aliases={7: 0},
        compiler_params=_params(("arbitrary",)),
        name="hgrn_sample",
    )(proj, proj, proj, proj, lb_raw, o_norm, state, o_h)


def _q_latent_kernel(q_ref, wuk_ref, o_ref):
    o_ref[0] = _dot_nt(q_ref[0][:, :QK_NOPE], wuk_ref[...]).astype(bf16)


def _q_latent(qh, wkv):
    row_blk = P_ROWS // DEC_BATCH
    return pl.pallas_call(
        _q_latent_kernel,
        grid=(MLA_HEADS,),
        in_specs=[
            pl.BlockSpec((1, DEC_BATCH, QK_DIM), lambda h: (h, row_blk, 0)),
            pl.BlockSpec((KV_RANK, QK_NOPE), lambda h: (0, 2 * h)),
        ],
        out_specs=pl.BlockSpec((1, DEC_BATCH, KV_RANK), lambda h: (h, 0, 0)),
        out_shape=jax.ShapeDtypeStruct((MLA_HEADS, DEC_BATCH, KV_RANK), bf16),
        compiler_params=_params(("arbitrary",)),
        name="q_latent",
    )(qh, wkv)


def _page_copies(pt_ref, ckv_hbm, kpe_hbm, ckv_buf, kpe_buf, sems, seq, c):
    copies = []
    for p in range(PAGES_PER_STEP):
        page = pt_ref[seq, c * PAGES_PER_STEP + p]
        copies.append(pltpu.make_async_copy(ckv_hbm.at[0, page], ckv_buf.at[c, p], sems.at[0, c]))
        copies.append(pltpu.make_async_copy(kpe_hbm.at[0, page], kpe_buf.at[c, p], sems.at[1, c]))
    return copies


def _paged_attn_kernel(pt_ref, ql_ref, qp_ref, cn_ref, kn_ref, ckv_hbm, kpe_hbm, o_ref,
                       ckv_buf, kpe_buf, sems, ckb_ref, kpb_ref):
    b = pl.program_id(0)
    copies = functools.partial(_page_copies, pt_ref, ckv_hbm, kpe_hbm, ckv_buf, kpe_buf, sems)
    to_log2 = SOFTMAX_SCALE * LOG2_E

    @pl.when(b == 0)
    def _():
        for c in range(N_PAGE_STEPS):
            for cp in copies(b, c):
                cp.start()

    ql = ql_ref[0]
    qp = qp_ref[0]

    def scores(c):
        for cp in copies(b, c):
            cp.wait()
        half = c % 2
        ckb_ref[half] = ckv_buf[c].reshape(KEYS_PER_STEP, KV_RANK).astype(bf16)
        kpb_ref[half] = jnp.concatenate([kpe_buf[c, p] for p in range(PAGES_PER_STEP)], axis=1).astype(bf16)
        return (_dot_nt(ql, ckb_ref[half]) + _dot(qp, kpb_ref[half])) * to_log2

    t_new = (jnp.sum(ql.astype(f32) * cn_ref[0], axis=1, keepdims=True)
             + jnp.sum(qp.astype(f32) * kn_ref[0], axis=1, keepdims=True)) * to_log2
    m = jnp.broadcast_to(t_new, (MLA_HEADS, LANES))
    l = jnp.ones((MLA_HEADS, LANES), f32)
    acc = jnp.broadcast_to(cn_ref[0], (MLA_HEADS, KV_RANK))

    t = scores(0)
    for c in range(N_PAGE_STEPS):
        t_next = scores(c + 1) if c + 1 < N_PAGE_STEPS else None
        m_next = jnp.maximum(m, jnp.max(t, axis=1, keepdims=True))
        p = jnp.exp2(t - jnp.concatenate([m_next] * (KEYS_PER_STEP // LANES), axis=1))
        alpha = jnp.exp2(m - m_next)
        l = alpha * l + jnp.sum(p, axis=1, keepdims=True)
        acc = jnp.concatenate([alpha] * (KV_RANK // LANES), axis=1) * acc + _dot(p.astype(bf16), ckb_ref[c % 2])
        m = m_next

        @pl.when(b + 1 < pl.num_programs(0))
        def _():
            for cp in copies(b + 1, c):
                cp.start()

        t = t_next

    o_ref[0] = acc / jnp.concatenate([l] * (KV_RANK // LANES), axis=1)


def _paged_attention(page_table, q_lat, q_pe, ckv_new, kpe_new, cache_ckv, cache_krope_t):
    seq_spec = lambda r, d: pl.BlockSpec((1, r, d), lambda b, pt: (b, 0, 0))
    grid_spec = pltpu.PrefetchScalarGridSpec(
        num_scalar_prefetch=1,
        grid=(DEC_BATCH,),
        in_specs=[
            seq_spec(MLA_HEADS, KV_RANK), seq_spec(MLA_HEADS, QK_ROPE),
            seq_spec(1, KV_RANK), seq_spec(1, QK_ROPE),
            pl.BlockSpec(memory_space=pl.ANY), pl.BlockSpec(memory_space=pl.ANY),
        ],
        out_specs=seq_spec(MLA_HEADS, KV_RANK),
        scratch_shapes=[
            pltpu.VMEM((N_PAGE_STEPS, PAGES_PER_STEP, PAGE_SIZE, KV_RANK), f32),
            pltpu.VMEM((N_PAGE_STEPS, PAGES_PER_STEP, QK_ROPE, PAGE_SIZE), f32),
            pltpu.SemaphoreType.DMA((2, N_PAGE_STEPS)),
            pltpu.VMEM((2, KEYS_PER_STEP, KV_RANK), bf16),
            pltpu.VMEM((2, QK_ROPE, KEYS_PER_STEP), bf16),
        ],
    )
    return pl.pallas_call(
        _paged_attn_kernel,
        grid_spec=grid_spec,
        out_shape=jax.ShapeDtypeStruct((DEC_BATCH, MLA_HEADS, KV_RANK), f32),
        compiler_params=_params(("arbitrary",)),
        name="paged_attn",
    )(page_table, q_lat, q_pe, ckv_new, kpe_new, cache_ckv, cache_krope_t)


def _v_up_kernel(o_ref, wuv_ref, out_ref):
    out_ref[...] = _dot(o_ref[0].astype(bf16), wuv_ref[...]).astype(bf16)


def _v_up(o_lat_h, wkv):
    return pl.pallas_call(
        _v_up_kernel,
        grid=(MLA_HEADS,),
        in_specs=[
            pl.BlockSpec((1, DEC_BATCH, KV_RANK), lambda h: (h, 0, 0)),
            pl.BlockSpec((KV_RANK, V_HEAD), lambda h: (0, 2 * h + 1)),
        ],
        out_specs=pl.BlockSpec((DEC_BATCH, V_HEAD), lambda h: (0, h)),
        out_shape=jax.ShapeDtypeStruct((DEC_BATCH, MLA_WIDTH), bf16),
        compiler_params=_params(("arbitrary",)),
        name="v_up",
    )(o_lat_h, wkv)


def _out_proj_kernel(oh_ref, oa_ref, head_hbm, xp_hbm, xs_hbm, wh_ref, wa_ref, g_ref, o_ref, x_buf, sems):
    i = pl.program_id(0)
    _x_tile_ready(functools.partial(_x_tile_copies, head_hbm, xp_hbm, xs_hbm, x_buf, sems), i)
    mix = _dot(oh_ref[...], wh_ref[...]) + _dot(oa_ref[...], wa_ref[...])
    o_ref[...] = x_buf[i % 2] + _rms(mix, g_ref[...])


def _out_proj(o_h, o_a, x_head, x_prompt, x_sample, w_out_b, g):
    hbm = pl.BlockSpec(memory_space=pl.ANY)
    return pl.pallas_call(
        _out_proj_kernel,
        grid=(N_TILES,),
        in_specs=[
            pl.BlockSpec((TM, HGRN_WIDTH), lambda i: (i, 0)),
            pl.BlockSpec((TM, MLA_WIDTH), lambda i: (i, 0)),
            hbm, hbm, hbm,
            pl.BlockSpec((HGRN_WIDTH, D_MODEL), lambda i: (0, 0)),
            pl.BlockSpec((MLA_WIDTH, D_MODEL), lambda i: (1, 0)),
            pl.BlockSpec((1, D_MODEL), lambda i: (0, 0)),
        ],
        out_specs=pl.BlockSpec((TM, D_MODEL), lambda i: (i, 0)),
        out_shape=jax.ShapeDtypeStruct((R_ALL, D_MODEL), f32),
        scratch_shapes=[pltpu.VMEM((2, TM, D_MODEL), f32), pltpu.SemaphoreType.DMA((2,))],
        compiler_params=_params(("arbitrary",)),
        name="out_proj",
    )(o_h, o_a, x_head, x_prompt, x_sample, w_out_b, w_out_b, g)


def _ffn_out_copies(acc_ref, sems, yp_hbm, ys_hbm, tile, slot, kind):
    if kind == "first":
        return [pltpu.make_async_copy(acc_ref.at[slot, pl.ds(P_START, TM - P_START)],
                                      yp_hbm.at[pl.ds(0, TM - P_START)], sems.at[slot])]
    if kind == "middle":
        return [pltpu.make_async_copy(acc_ref.at[slot],
                                      yp_hbm.at[pl.ds(pl.multiple_of(tile * TM - P_START, CHUNK), TM)],
                                      sems.at[slot])]
    return [pltpu.make_async_copy(acc_ref.at[slot, pl.ds(0, SAMPLE_LOCAL)],
                                  yp_hbm.at[pl.ds(SEQ - SAMPLE_LOCAL, SAMPLE_LOCAL)], sems.at[slot]),
            pltpu.make_async_copy(acc_ref.at[slot, pl.ds(SAMPLE_LOCAL, DEC_BATCH)], ys_hbm, sems.at[slot])]


def _ffn_kernel(x_ref, gpre_ref, gpost_ref, wg_ref, wv_ref, cw_ref, cb_ref, wd_ref, buf0_ref, buf1_ref,
                yp_hbm, ys_hbm, tail_ref, h_ref, act_ref, carry_ref, acc_ref, sems):
    i = pl.program_id(0)
    j = pl.program_id(1)
    last_i = pl.num_programs(0) - 1
    slot = i % 2
    col = pl.ds(pl.multiple_of(j * FF_TN, FF_TN), FF_TN)
    out_copies = functools.partial(_ffn_out_copies, acc_ref, sems, yp_hbm, ys_hbm)

    @pl.when(j == 0)
    def _():
        @pl.when(i == 2)
        def _():
            for cp in out_copies(0, slot, "first"):
                cp.wait()

        @pl.when(i > 2)
        def _():
            for cp in out_copies(i - 2, slot, "middle"):
                cp.wait()

        h_ref[...] = _rms(x_ref[...], gpre_ref[...]).astype(bf16)
        acc_ref[slot] = jnp.zeros((TM, D_MODEL), f32)

    @pl.when(i == 0)
    def _():
        carry_ref[:, col] = jnp.zeros((8, FF_TN), f32)

    h = h_ref[...]
    row = lax.broadcasted_iota(jnp.int32, (TM, 1), 0)
    g = jnp.where(row + i * TM >= PAD, _dot(h, wg_ref[...]), 0.0)
    val = _dot(h, wv_ref[...])
    prev = carry_ref[:, col]
    g1 = jnp.where(row == 0, prev[7:8], pltpu.roll(g, 1, 0))
    g2 = jnp.where(row == 0, prev[6:7], jnp.where(row == 1, prev[7:8], pltpu.roll(g, 2, 0)))
    w0, w1, w2 = cw_ref[0:1, :], cw_ref[1:2, :], cw_ref[2:3, :]
    y = cb_ref[...] + w0 * g2 + w1 * g1 + w2 * g
    act_ref[...] = (_silu(y) * val).astype(bf16)
    carry_ref[:, col] = g[TM - 8:, :]
    tail_ref[0] = g[TM - TAIL_ROWS:, :]

    @pl.when(i == last_i)
    def _():
        ys = cb_ref[...] + w0 * buf0_ref[...] + w1 * buf1_ref[...] + w2 * g[SAMPLE_LOCAL:, :]
        act_ref[SAMPLE_LOCAL:, :] = (_silu(ys) * val[SAMPLE_LOCAL:, :]).astype(bf16)

    acc_ref[slot] += _dot(act_ref[...], wd_ref[...])

    @pl.when(j == pl.num_programs(1) - 1)
    def _():
        acc_ref[slot] = x_ref[...] + _rms(acc_ref[slot], gpost_ref[...])

        @pl.when(i == 0)
        def _():
            for cp in out_copies(i, slot, "first"):
                cp.start()

        @pl.when((i > 0) & (i < last_i))
        def _():
            for cp in out_copies(i, slot, "middle"):
                cp.start()

        @pl.when(i == last_i)
        def _():
            for cp in out_copies(i, slot, "last"):
                cp.start()
            for cp in out_copies(i - 1, 1 - slot, "middle"):
                cp.wait()
            for cp in out_copies(i, slot, "last"):
                cp.wait()


def _ffn(x1, g_pre, g_post, w_up_b, conv_w, conv_b, w_down_b, buf0, buf1):
    n_ff = D_FF // FF_TN
    n_tiles = R_ALL // TM
    sample_buf = pl.BlockSpec((DEC_BATCH, FF_TN), lambda i, j: (0, jnp.where(i == n_tiles - 1, j, 0)))
    return pl.pallas_call(
        _ffn_kernel,
        grid=(R_ALL // TM, n_ff),
        in_specs=[
            pl.BlockSpec((TM, D_MODEL), lambda i, j: (i, 0)),
            pl.BlockSpec((1, D_MODEL), lambda i, j: (0, 0)),
            pl.BlockSpec((1, D_MODEL), lambda i, j: (0, 0)),
            pl.BlockSpec((D_MODEL, FF_TN), lambda i, j: (0, j)),
            pl.BlockSpec((D_MODEL, FF_TN), lambda i, j: (0, j + n_ff)),
            pl.BlockSpec((3, FF_TN), lambda i, j: (0, j)),
            pl.BlockSpec((1, FF_TN), lambda i, j: (0, j)),
            pl.BlockSpec((FF_TN, D_MODEL), lambda i, j: (j, 0)),
            sample_buf, sample_buf,
        ],
        out_specs=[
            pl.BlockSpec(memory_space=pl.ANY),
            pl.BlockSpec(memory_space=pl.ANY),
            pl.BlockSpec((1, TAIL_ROWS, FF_TN), lambda i, j: (i, 0, j)),
        ],
        out_shape=[
            jax.ShapeDtypeStruct((SEQ, D_MODEL), f32),
            jax.ShapeDtypeStruct((DEC_BATCH, D_MODEL), f32),
            jax.ShapeDtypeStruct((R_ALL // TM, TAIL_ROWS, D_FF), f32),
        ],
        scratch_shapes=[
            pltpu.VMEM((TM, D_MODEL), bf16),
            pltpu.VMEM((TM, FF_TN), bf16),
            pltpu.VMEM((8, D_FF), f32),
            pltpu.VMEM((2, TM, D_MODEL), f32),
            pltpu.SemaphoreType.DMA((2,)),
        ],
        compiler_params=_params(("arbitrary", "arbitrary")),
        name="conv_ffn",
    )(x1, g_pre, g_post, w_up_b, w_up_b, conv_w, conv_b, w_down_b, buf0, buf1)


def kernel(x_prompt, x_sample, cache_ckv, cache_krope, state_hgrn, state_conv, page_table, meta_tokens,
           lb_raw, g_mix_pre, g_mix_post, g_ffn_pre, g_ffn_post, w_in, hgrn_o_norm, q_a_norm, kv_a_norm,
           w_q_up, w_kv_up, w_out, w_ffn_up, conv_w, conv_b, w_ffn_down):
    x_head = jnp.concatenate([jnp.zeros((PAD, D_MODEL), f32), meta_tokens.astype(f32)], axis=0)
    x_p = x_prompt[0]
    x_s = x_sample[:, 0]

    w_in_t = w_in[0].T.astype(bf16)
    w_in_tail = jnp.pad(w_in_t[IN_FULL_TILES * IN_TN:], ((0, IN_PAD - IN_TOTAL), (0, 0)))
    wq = w_q_up[0].reshape(Q_RANK, MLA_HEADS, QK_DIM)
    wq_p = jnp.concatenate([wq[:, :, :QK_NOPE].reshape(Q_RANK, -1), wq[:, :, QK_NOPE:].reshape(Q_RANK, -1)],
                           axis=1).astype(bf16)
    wkv = w_kv_up[0].astype(bf16)
    w_out_b = w_out[0].astype(bf16)
    w_up_b = w_ffn_up[0].astype(bf16)
    w_down_b = w_ffn_down[0].astype(bf16)

    proj = _in_proj(x_head, x_p, x_s, g_mix_pre, w_in_t, w_in_tail)
    qh, kh, vh, ckv, kpe = _mla_proj(proj, q_a_norm, kv_a_norm, wq_p, wkv)

    o_a = _attention(qh, kh, vh)
    o_h, s_p = _hgrn_prompt(proj, lb_raw, hgrn_o_norm)

    o_h, s_s = _hgrn_sample(proj, lb_raw, hgrn_o_norm, state_hgrn[0], o_h)
    q_lat = _q_latent(qh, wkv).transpose(1, 0, 2)
    q_pe_s = qh[:, P_ROWS:, QK_NOPE:].transpose(1, 0, 2)
    ckv_s = ckv[P_ROWS:]
    kpe_s = kpe[P_ROWS:]
    o_lat = _paged_attention(page_table, q_lat, q_pe_s, ckv_s[:, None, :], kpe_s[:, None, :],
                             cache_ckv, cache_krope.transpose(0, 1, 3, 2))
    o_a = jnp.concatenate([o_a, _v_up(o_lat.transpose(1, 0, 2), wkv)], axis=0)

    x1 = _out_proj(o_h, o_a, x_head, x_p, x_s, w_out_b, g_mix_post)
    y_p, y_s, tails = _ffn(x1, g_ffn_pre, g_ffn_post, w_up_b, conv_w[0], conv_b, w_down_b,
                           state_conv[0, :, 0], state_conv[0, :, 1])
    tail = tails[-1]

    n_tail = TAIL_ROWS - DEC_BATCH
    conv_p = tail[n_tail - 2:n_tail][None, None]
    conv_s = jnp.stack([state_conv[0, :, 1], tail[n_tail:]], axis=1)[None]
    return (y_p[None], y_s[:, None],
            ckv[PAD:P_ROWS][None, None], kpe[PAD:P_ROWS][None, None],
            ckv_s[None, :, None], kpe_s[None, :, None],
            s_p[None, None], s_s[None], conv_p, conv_s)
```

```python
import functools

import numpy as np
import jax
import jax.numpy as jnp
from jax import lax
from jax.experimental import pallas as pl
from jax.experimental.pallas import tpu as pltpu

D_MODEL = 2048
SEQ = 8192
N_META = 16
DEC_BATCH = 128
PAST_LEN = 16384
PAGE_SIZE = 128
N_PAGES = PAST_LEN // PAGE_SIZE
HGRN_HEADS = 8
HGRN_K = 128
HGRN_V = 128
HGRN_WIDTH = HGRN_HEADS * HGRN_V
MLA_HEADS = 8
QK_NOPE = 128
QK_ROPE = 64
QK_DIM = QK_NOPE + QK_ROPE
V_HEAD = 128
Q_RANK = 512
KV_RANK = 256
MLA_WIDTH = MLA_HEADS * V_HEAD
ROPE_THETA = 10000.0
SOFTMAX_SCALE = QK_DIM ** -0.5
LOG2_E = 1.4426950408889634
D_FF = 5632
EPS = 1e-6
IN_TOTAL = 4 * HGRN_WIDTH + Q_RANK + KV_RANK + QK_ROPE

CHUNK = 128
PAD = CHUNK - N_META
P_START = PAD + N_META
P_ROWS = P_START + SEQ
R_ALL = P_ROWS + DEC_BATCH
N_CHUNKS = P_ROWS // CHUNK

LANES = 128
SUBLANES = 8
IN_PAD = 5120
IN_TN = 1280
IN_FULL_TILES = IN_TOTAL // IN_TN
TM = 768
TQ = 640
TQ_SUB = 640
FF_TN = 512
HGRN_SEQS = 16
HGRN_STAGE_SEQS = 8
SAMPLE_LOCAL = P_ROWS - (R_ALL // TM - 1) * TM
TAIL_ROWS = 136
PAGES_PER_STEP = 16
N_PAGE_STEPS = N_PAGES // PAGES_PER_STEP
KEYS_PER_STEP = PAGES_PER_STEP * PAGE_SIZE
NEG_BIG = -1e30
VMEM_LIMIT = 56 * 1024 * 1024

NT_DIMS = (((1,), (1,)), ((), ()))
TN_DIMS = (((0,), (0,)), ((), ()))

f32 = jnp.float32
bf16 = jnp.bfloat16


def _params(sem, vmem=VMEM_LIMIT):
    return pltpu.CompilerParams(dimension_semantics=sem, vmem_limit_bytes=vmem)


def _rms(x, g):
    return x * lax.rsqrt(jnp.mean(x * x, axis=-1, keepdims=True) + EPS) * g


def _sigmoid(x):
    return 1.0 / (1.0 + jnp.exp(-x))


def _silu(x):
    return x * _sigmoid(x)


def _dot(a, b):
    return jnp.dot(a, b, preferred_element_type=f32)


def _dot_nt(a, b):
    return lax.dot_general(a, b, NT_DIMS, preferred_element_type=f32)


def _dot_tn(a, b):
    return lax.dot_general(a, b, TN_DIMS, preferred_element_type=f32)


N_TILES = R_ALL // TM


def _x_tile_copies(head_hbm, xp_hbm, xs_hbm, buf, sems, tile, kind):
    slot = tile % 2
    body = TM - P_START
    if kind == "first":
        return [pltpu.make_async_copy(head_hbm, buf.at[slot, pl.ds(0, P_START)], sems.at[slot]),
                pltpu.make_async_copy(xp_hbm.at[pl.ds(0, body)], buf.at[slot, pl.ds(P_START, body)], sems.at[slot])]
    if kind == "middle":
        return [pltpu.make_async_copy(xp_hbm.at[pl.ds(pl.multiple_of(tile * TM - P_START, CHUNK), TM)],
                                      buf.at[slot], sems.at[slot])]
    return [pltpu.make_async_copy(xp_hbm.at[pl.ds(SEQ - body, body)], buf.at[slot, pl.ds(0, body)], sems.at[slot]),
            pltpu.make_async_copy(xs_hbm, buf.at[slot, pl.ds(body, DEC_BATCH)], sems.at[slot])]


def _x_tile_ready(copies, i):
    last = N_TILES - 1

    @pl.when(i == 0)
    def _():
        for cp in copies(i, "first"):
            cp.start()
        for cp in copies(i, "first"):
            cp.wait()
        for cp in copies(i + 1, "middle"):
            cp.start()

    @pl.when((i > 0) & (i < last - 1))
    def _():
        for cp in copies(i, "middle"):
            cp.wait()
        for cp in copies(i + 1, "middle"):
            cp.start()

    @pl.when(i == last - 1)
    def _():
        for cp in copies(i, "middle"):
            cp.wait()
        for cp in copies(i + 1, "last"):
            cp.start()

    @pl.when(i == last)
    def _():
        for cp in copies(i, "last"):
            cp.wait()


def _in_proj_kernel(head_hbm, xp_hbm, xs_hbm, g_ref, w_ref, wtail_ref, o_ref, h_ref, x_buf, sems):
    i = pl.program_id(0)
    j = pl.program_id(1)

    @pl.when(j == 0)
    def _():
        _x_tile_ready(functools.partial(_x_tile_copies, head_hbm, xp_hbm, xs_hbm, x_buf, sems), i)
        h_ref[...] = _rms(x_buf[i % 2], g_ref[...]).astype(bf16)

    @pl.when(j < IN_FULL_TILES)
    def _():
        o_ref[...] = _dot_nt(h_ref[...], w_ref[...])

    @pl.when(j >= IN_FULL_TILES)
    def _():
        o_ref[...] = _dot_nt(h_ref[...], wtail_ref[...])


def _in_proj(x_head, x_prompt, x_sample, g, w_in_t, w_in_tail):
    hbm = pl.BlockSpec(memory_space=pl.ANY)
    return pl.pallas_call(
        _in_proj_kernel,
        grid=(N_TILES, IN_PAD // IN_TN),
        in_specs=[
            hbm, hbm, hbm,
            pl.BlockSpec((1, D_MODEL), lambda i, j: (0, 0)),
            pl.BlockSpec((IN_TN, D_MODEL), lambda i, j: (jnp.minimum(j, IN_FULL_TILES - 1), 0)),
            pl.BlockSpec((IN_TN, D_MODEL), lambda i, j: (0, 0)),
        ],
        out_specs=pl.BlockSpec((TM, IN_TN), lambda i, j: (i, j)),
        out_shape=jax.ShapeDtypeStruct((R_ALL, IN_PAD), f32),
        scratch_shapes=[pltpu.VMEM((TM, D_MODEL), bf16), pltpu.VMEM((2, TM, D_MODEL), f32),
                        pltpu.SemaphoreType.DMA((2,))],
        compiler_params=_params(("arbitrary", "arbitrary")),
        name="in_proj",
    )(x_head, x_prompt, x_sample, g, w_in_t, w_in_tail)


def _rope_tables(row0, n_rows, cos_tab, sin_tab):
    lane = lax.broadcasted_iota(jnp.int32, (1, LANES), 1)
    half = QK_ROPE // 2
    fidx = (lane & (half - 1)).astype(f32)
    inv = jnp.power(jnp.float32(ROPE_THETA), -fidx / half)

    @pl.when(row0 == 0)
    def _():
        r = lax.broadcasted_iota(jnp.int32, (n_rows, 1), 0).astype(f32)
        cos_tab[...] = jnp.cos(r * inv)
        sin_tab[...] = jnp.sin(r * inv)

    a0 = (row0 - PAD).astype(f32) * inv
    a_s = jnp.float32(PAST_LEN) * inv
    c0, s0 = jnp.cos(a0), jnp.sin(a0)
    in_prompt = row0 + lax.broadcasted_iota(jnp.int32, (n_rows, 1), 0) < P_ROWS
    cos = jnp.where(in_prompt, c0 * cos_tab[...] - s0 * sin_tab[...], jnp.cos(a_s))
    sin = jnp.where(in_prompt, s0 * cos_tab[...] + c0 * sin_tab[...], jnp.sin(a_s))
    return cos, jnp.where((lane & half) == 0, -sin, sin)


def _rotate_half(x, cos, sin_signed):
    n = x.shape[1]
    half = QK_ROPE // 2
    first_half = (lax.broadcasted_iota(jnp.int32, (1, n), 1) & half) == 0
    partner = jnp.where(first_half, pltpu.roll(x, n - half, 1), pltpu.roll(x, half, 1))
    return x * cos + partner * sin_signed


def _mla_proj_kernel(qd_ref, kvd_ref, kr_ref, qn_ref, kvn_ref, wq_ref, wkv_ref,
                     qh_ref, kh_ref, vh_ref, ckv_ref, kpe_ref, cos_tab, sin_tab):
    tm = qd_ref.shape[0]
    cos, sin_s = _rope_tables(pl.program_id(0) * tm, tm, cos_tab, sin_tab)

    q = _dot(_rms(qd_ref[...], qn_ref[...]).astype(bf16), wq_ref[...])
    reps = MLA_HEADS * QK_ROPE // LANES
    q_pe = _rotate_half(q[:, MLA_HEADS * QK_NOPE:],
                        jnp.concatenate([cos] * reps, axis=1),
                        jnp.concatenate([sin_s] * reps, axis=1))

    k_pe = _rotate_half(kr_ref[...], cos, sin_s)[:, :QK_ROPE]
    kpe_ref[...] = k_pe

    c_kv = _rms(kvd_ref[...], kvn_ref[...])
    ckv_ref[...] = c_kv
    kv = _dot(c_kv.astype(bf16), wkv_ref[...])

    k_pe_b = k_pe.astype(bf16)
    for h in range(MLA_HEADS):
        qh_ref[h, :, :QK_NOPE] = q[:, h * QK_NOPE:(h + 1) * QK_NOPE].astype(bf16)
        qh_ref[h, :, QK_NOPE:] = q_pe[:, h * QK_ROPE:(h + 1) * QK_ROPE].astype(bf16)
        base = h * (QK_NOPE + V_HEAD)
        kh_ref[h, :, :QK_NOPE] = kv[:, base:base + QK_NOPE].astype(bf16)
        kh_ref[h, :, QK_NOPE:] = k_pe_b
        vh_ref[h] = kv[:, base + QK_NOPE:base + QK_NOPE + V_HEAD].astype(bf16)


def _mla_proj(proj, q_a_norm, kv_a_norm, wq_p, wkv):
    col_q = (4 * HGRN_WIDTH) // Q_RANK
    col_kv = (4 * HGRN_WIDTH + Q_RANK) // KV_RANK
    col_kr = (4 * HGRN_WIDTH + Q_RANK + KV_RANK) // LANES
    head_spec = lambda d: pl.BlockSpec((MLA_HEADS, TM, d), lambda i: (0, i, 0))
    return pl.pallas_call(
        _mla_proj_kernel,
        grid=(R_ALL // TM,),
        in_specs=[
            pl.BlockSpec((TM, Q_RANK), lambda i: (i, col_q)),
            pl.BlockSpec((TM, KV_RANK), lambda i: (i, col_kv)),
            pl.BlockSpec((TM, LANES), lambda i: (i, col_kr)),
            pl.BlockSpec((1, Q_RANK), lambda i: (0, 0)),
            pl.BlockSpec((1, KV_RANK), lambda i: (0, 0)),
            pl.BlockSpec(wq_p.shape, lambda i: (0, 0)),
            pl.BlockSpec(wkv.shape, lambda i: (0, 0)),
        ],
        out_specs=[
            head_spec(QK_DIM), head_spec(QK_DIM), head_spec(V_HEAD),
            pl.BlockSpec((TM, KV_RANK), lambda i: (i, 0)),
            pl.BlockSpec((TM, QK_ROPE), lambda i: (i, 0)),
        ],
        out_shape=[
            jax.ShapeDtypeStruct((MLA_HEADS, R_ALL, QK_DIM), bf16),
            jax.ShapeDtypeStruct((MLA_HEADS, R_ALL, QK_DIM), bf16),
            jax.ShapeDtypeStruct((MLA_HEADS, R_ALL, V_HEAD), bf16),
            jax.ShapeDtypeStruct((R_ALL, KV_RANK), f32),
            jax.ShapeDtypeStruct((R_ALL, QK_ROPE), f32),
        ],
        scratch_shapes=[pltpu.VMEM((TM, LANES), f32), pltpu.VMEM((TM, LANES), f32)],
        compiler_params=_params(("arbitrary",)),
        name="mla_proj",
    )(proj, proj, proj, q_a_norm, kv_a_norm, wq_p, wkv)


def _attn_kernel(q_ref, k_ref, v_ref, o_ref, m_ref, l_ref, acc_ref, ta_ref, tb_ref):
    i = pl.program_id(1)
    q = q_ref[0]
    m_ref[...] = jnp.full(m_ref.shape, NEG_BIG, f32)
    l_ref[...] = jnp.zeros(l_ref.shape, f32)
    acc_ref[...] = jnp.zeros(acc_ref.shape, f32)
    reps = TQ // LANES
    groups = range(0, TQ, TQ_SUB)

    def logits(t_ref, j):
        t_ref[...] = _dot_nt(q, k_ref[0, pl.ds(pl.multiple_of(j * TQ, TQ), TQ), :])

    def block(t_ref, j, masked):
        v = v_ref[0, pl.ds(pl.multiple_of(j * TQ, TQ), TQ), :]
        for r0 in groups:
            rows = slice(r0, r0 + TQ_SUB)
            t = t_ref[rows] * (SOFTMAX_SCALE * LOG2_E)
            if masked:
                q_row = i * TQ + r0 + lax.broadcasted_iota(jnp.int32, (TQ_SUB, TQ), 0)
                k_row = j * TQ + lax.broadcasted_iota(jnp.int32, (TQ_SUB, TQ), 1)
                t = jnp.where(k_row <= q_row, jnp.where(k_row >= PAD, t, NEG_BIG), NEG_BIG)
            m_prev = m_ref[rows]
            m_next = jnp.maximum(m_prev, jnp.max(t, axis=1, keepdims=True))
            p = jnp.exp2(t - jnp.concatenate([m_next] * reps, axis=1))
            alpha = jnp.exp2(m_prev - m_next)
            l_ref[rows] = alpha * l_ref[rows] + jnp.sum(p, axis=1, keepdims=True)
            acc_ref[rows] = alpha * acc_ref[rows] + _dot(p.astype(bf16), v)
            m_ref[rows] = m_next

    logits(ta_ref, 0)

    @pl.when(i == 0)
    def _():
        block(ta_ref, 0, True)

    @pl.when(i > 0)
    def _():
        logits(tb_ref, 1)
        block(ta_ref, 0, True)
        n_inner = i - 1

        def pair(p, carry):
            j = 1 + 2 * p
            logits(ta_ref, j + 1)
            block(tb_ref, j, False)
            logits(tb_ref, j + 2)
            block(ta_ref, j + 1, False)
            return carry

        lax.fori_loop(0, n_inner // 2, pair, 0)
        j = 1 + 2 * (n_inner // 2)

        @pl.when(n_inner % 2 == 1)
        def _():
            logits(ta_ref, j + 1)
            block(tb_ref, j, False)
            block(ta_ref, j + 1, True)

        @pl.when(n_inner % 2 == 0)
        def _():
            block(tb_ref, j, True)

    o_ref[...] = (acc_ref[...] / l_ref[...]).astype(bf16)


def _attention(qh, kh, vh):
    return pl.pallas_call(
        _attn_kernel,
        grid=(MLA_HEADS, P_ROWS // TQ),
        in_specs=[
            pl.BlockSpec((1, TQ, QK_DIM), lambda h, i: (h, i, 0)),
            pl.BlockSpec((1, R_ALL, QK_DIM), lambda h, i: (h, 0, 0)),
            pl.BlockSpec((1, R_ALL, V_HEAD), lambda h, i: (h, 0, 0)),
        ],
        out_specs=pl.BlockSpec((TQ, V_HEAD), lambda h, i: (i, h)),
        out_shape=jax.ShapeDtypeStruct((P_ROWS, MLA_WIDTH), bf16),
        scratch_shapes=[pltpu.VMEM((TQ, LANES), f32), pltpu.VMEM((TQ, LANES), f32),
                        pltpu.VMEM((TQ, V_HEAD), f32),
                        pltpu.VMEM((TQ, TQ), f32), pltpu.VMEM((TQ, TQ), f32)],
        compiler_params=_params(("arbitrary", "arbitrary")),
        name="prompt_attn",
    )(qh, kh, vh)


N_LEVELS = 7


def _hgrn_constants():
    t = np.arange(CHUNK)
    sums = (t[None, :] <= t[:, None]).astype(np.float32)
    x = t[:, None] ^ t[None, :]
    lev = np.where(x > 0, np.floor(np.log2(np.maximum(x, 1))).astype(np.int32), N_LEVELS)
    lev = np.where(t[None, :] > t[:, None], N_LEVELS + 1, lev).astype(np.int32)
    return sums, lev


def _split3(x):
    a = x.astype(bf16)
    r = x - a.astype(f32)
    b = r.astype(bf16)
    c = (r - b.astype(f32)).astype(bf16)
    return a, b, c


def _lower_bound(lb_raw_ref):
    a0 = lb_raw_ref[0:1, :]
    a1 = lb_raw_ref[1:2, :]
    m = jnp.maximum(a0, a1)
    e0 = jnp.exp(a0 - m)
    return e0 / (e0 + jnp.exp(a1 - m))


def _pair_block_reference(b, lvl):
    m = 1 << lvl
    n, w = b.shape
    if m >= SUBLANES:
        g = m // SUBLANES
        b4 = b.reshape(n // (2 * m), 2 * g, SUBLANES, w)
        return jnp.broadcast_to(b4[:, g - 1:g, SUBLANES - 1:, :], b4.shape).reshape(n, w)
    b3 = b.reshape(n // SUBLANES, SUBLANES, w)
    sub = lax.broadcasted_iota(jnp.int32, (1, SUBLANES, 1), 1)
    ref = b3[:, m - 1:m, :]
    for start in range(2 * m, SUBLANES, 2 * m):
        ref = jnp.where(sub >= start, b3[:, start + m - 1:start + m, :], ref)
    return jnp.broadcast_to(ref, b3.shape).reshape(n, w)


def _hgrn_prompt_kernel(hq_ref, hf_ref, hi_ref, hg_ref, lb_ref, on_ref, sums_ref, lev_ref,
                        o_ref, s_ref, st_ref):
    c = pl.program_id(0)

    @pl.when(c == 0)
    def _():
        st_ref[...] = jnp.zeros(st_ref.shape, f32)

    @pl.when(c == N_CHUNKS)
    def _():
        o_ref[...] = jnp.zeros(o_ref.shape, bf16)

    @pl.when(c < N_CHUNKS)
    def _():
        lb = _lower_bound(lb_ref)
        f_all = lb + (1.0 - lb) * _sigmoid(hf_ref[...])
        l1, l2, l3 = _split3(jnp.log(f_all))
        sums = sums_ref[...]
        b_all = _dot(sums, l1) + _dot(sums, l2) + _dot(sums, l3)

        row = lax.broadcasted_iota(jnp.int32, (CHUNK, 1), 0)
        lev = lev_ref[...]
        refs = [_pair_block_reference(b_all, lvl) for lvl in range(N_LEVELS)]
        is_q = [((row >> lvl) & 1) == 1 for lvl in range(N_LEVELS)]
        sign = [jnp.where(m, 1.0, -1.0) for m in is_q]
        for h in range(HGRN_HEADS):
            sl = slice(h * HGRN_K, (h + 1) * HGRN_K)
            q = _silu(hq_ref[:, sl])
            k = 1.0 - f_all[:, sl]
            v = hi_ref[:, sl].astype(bf16)
            b = b_all[:, sl]
            a = jnp.where(lev == N_LEVELS, _dot_nt(q.astype(bf16), k.astype(bf16)), 0.0)
            for lvl in range(N_LEVELS):
                w = (jnp.where(is_q[lvl], q, k) * jnp.exp((b - refs[lvl][:, sl]) * sign[lvl])).astype(bf16)
                a = jnp.where(lev == lvl, _dot_nt(w, w), a)

            b_suf = b[CHUNK - 1:CHUNK, :] - b
            st = st_ref[h]
            o = _dot(a.astype(bf16), v) + _dot_nt((q * jnp.exp(b)).astype(bf16), st.astype(bf16))
            st_new = st * jnp.exp(b[CHUNK - 1:CHUNK, :]) + _dot_tn(v, (k * jnp.exp(b_suf)).astype(bf16))
            st_ref[h] = st_new
            o_ref[:, sl] = (_rms(o, on_ref[...]) * _silu(hg_ref[:, sl])).astype(bf16)

    @pl.when(c == N_CHUNKS - 1)
    def _():
        for h in range(HGRN_HEADS):
            s_ref[h] = st_ref[h].T


def _hgrn_prompt(proj, lb_raw, o_norm):
    sums, lev = _hgrn_constants()
    blk = lambda part: pl.BlockSpec((CHUNK, HGRN_WIDTH), lambda c: (c, part))
    return pl.pallas_call(
        _hgrn_prompt_kernel,
        grid=(N_CHUNKS + DEC_BATCH // CHUNK,),
        in_specs=[
            blk(0), blk(1), blk(2), blk(3),
            pl.BlockSpec((2, HGRN_WIDTH), lambda c: (0, 0)),
            pl.BlockSpec((1, HGRN_V), lambda c: (0, 0)),
            pl.BlockSpec(sums.shape, lambda c: (0, 0)),
            pl.BlockSpec(lev.shape, lambda c: (0, 0)),
        ],
        out_specs=[
            pl.BlockSpec((CHUNK, HGRN_WIDTH), lambda c: (c, 0)),
            pl.BlockSpec((HGRN_HEADS, HGRN_K, HGRN_V), lambda c: (0, 0, 0)),
        ],
        out_shape=[
            jax.ShapeDtypeStruct((R_ALL, HGRN_WIDTH), bf16),
            jax.ShapeDtypeStruct((HGRN_HEADS, HGRN_K, HGRN_V), f32),
        ],
        scratch_shapes=[pltpu.VMEM((HGRN_HEADS, HGRN_V, HGRN_K), f32)],
        compiler_params=_params(("arbitrary",)),
        name="hgrn_prompt",
    )(proj, proj, proj, proj, lb_raw, o_norm, jnp.asarray(sums, bf16), jnp.asarray(lev))


def _hgrn_sample_kernel(hq_ref, hf_ref, hi_ref, hg_ref, lb_ref, on_ref, s0_ref, oh_hbm, o_ref, s_ref):
    del oh_hbm
    lb = _lower_bound(lb_ref)
    f = lb + (1.0 - lb) * _sigmoid(hf_ref[...])
    q = _silu(hq_ref[...])
    k = (1.0 - f).astype(bf16).astype(f32)
    v = hi_ref[...].astype(bf16).astype(f32)
    gate = _silu(hg_ref[...])
    f1 = f.astype(bf16).astype(f32)
    f2 = (f - f1).astype(bf16).astype(f32)
    f3 = ((f - f1) - f2).astype(bf16).astype(f32)

    n_rows = 16
    rid = lax.broadcasted_iota(jnp.int32, (n_rows, HGRN_K), 0)
    rid2 = lax.broadcasted_iota(jnp.int32, (n_rows, 2 * HGRN_V), 0)
    left = lax.broadcasted_iota(jnp.int32, (n_rows, 2 * HGRN_V), 1) < HGRN_V
    zero_v = jnp.zeros((1, HGRN_V), f32)

    at = lambda x, s, h: x[s:s + 1, h * HGRN_K:(h + 1) * HGRN_K]

    def mix(s, h):
        lhs = jnp.where(rid == 0, at(f1, s, h), jnp.where(rid == 1, at(f2, s, h), jnp.where(
            rid == 2, at(f3, s, h), jnp.where(rid == 3, at(k, s, h), 0.0))))
        rhs = jnp.where(rid2 < 3, jnp.where(left, 1.0, 0.0),
                        jnp.where(rid2 == 3, jnp.concatenate([zero_v, at(v, s, h)], axis=1), 0.0))
        return _dot_tn(lhs.astype(bf16), rhs.astype(bf16))

    outs = []
    for s0 in range(0, HGRN_SEQS, HGRN_STAGE_SEQS):
        pairs = [(s, h) for s in range(s0, s0 + HGRN_STAGE_SEQS) for h in range(HGRN_HEADS)]
        mixes = [mix(s, h) for s, h in pairs]
        for (s, h), mx in zip(pairs, mixes):
            s_ref[s, h] = mx[:, :HGRN_V] * s0_ref[s, h] + mx[:, HGRN_V:]
        o_raw = [_dot(jnp.where(rid == 0, at(q, s, h), 0.0).astype(bf16), s_ref[s, h].astype(bf16))[0:1]
                 for s, h in pairs]
        outs += [_rms(o, on_ref[...]) * at(gate, s, h) for (s, h), o in zip(pairs, o_raw)]
    o_ref[...] = jnp.concatenate(
        [jnp.concatenate(outs[s * HGRN_HEADS:(s + 1) * HGRN_HEADS], axis=1) for s in range(HGRN_SEQS)],
        axis=0).astype(bf16)


def _hgrn_sample(proj, lb_raw, o_norm, state, o_h):
    row_blk = P_ROWS // HGRN_SEQS
    blk = lambda part: pl.BlockSpec((HGRN_SEQS, HGRN_WIDTH), lambda b: (row_blk + b, part))
    state_spec = pl.BlockSpec((HGRN_SEQS, HGRN_HEADS, HGRN_K, HGRN_V), lambda b: (b, 0, 0, 0))
    return pl.pallas_call(
        _hgrn_sample_kernel,
        grid=(DEC_BATCH // HGRN_SEQS,),
        in_specs=[
            blk(0), blk(1), blk(2), blk(3),
            pl.BlockSpec((2, HGRN_WIDTH), lambda b: (0, 0)),
            pl.BlockSpec((1, HGRN_V), lambda b: (0, 0)),
            state_spec,
            pl.BlockSpec(memory_space=pl.ANY),
        ],
        out_specs=[blk(0), state_spec],
        out_shape=[
            jax.ShapeDtypeStruct(o_h.shape, o_h.dtype),
            jax.ShapeDtypeStruct(state.shape, f32),
        ],
        input_output_aliases={7: 0},
        compiler_params=_params(("arbitrary",)),
        name="hgrn_sample",
    )(proj, proj, proj, proj, lb_raw, o_norm, state, o_h)


def _q_latent_kernel(q_ref, wuk_ref, o_ref):
    o_ref[0] = _dot_nt(q_ref[0][:, :QK_NOPE], wuk_ref[...]).astype(bf16)


def _q_latent(qh, wkv):
    row_blk = P_ROWS // DEC_BATCH
    return pl.pallas_call(
        _q_latent_kernel,
        grid=(MLA_HEADS,),
        in_specs=[
            pl.BlockSpec((1, DEC_BATCH, QK_DIM), lambda h: (h, row_blk, 0)),
            pl.BlockSpec((KV_RANK, QK_NOPE), lambda h: (0, 2 * h)),
        ],
        out_specs=pl.BlockSpec((1, DEC_BATCH, KV_RANK), lambda h: (h, 0, 0)),
        out_shape=jax.ShapeDtypeStruct((MLA_HEADS, DEC_BATCH, KV_RANK), bf16),
        compiler_params=_params(("arbitrary",)),
        name="q_latent",
    )(qh, wkv)


def _page_copies(pt_ref, ckv_hbm, kpe_hbm, ckv_buf, kpe_buf, sems, seq, c):
    copies = []
    for p in range(PAGES_PER_STEP):
        page = pt_ref[seq, c * PAGES_PER_STEP + p]
        copies.append(pltpu.make_async_copy(ckv_hbm.at[0, page], ckv_buf.at[c, p], sems.at[0, c]))
        copies.append(pltpu.make_async_copy(kpe_hbm.at[0, page], kpe_buf.at[c, p], sems.at[1, c]))
    return copies


def _paged_attn_kernel(pt_ref, ql_ref, qp_ref, cn_ref, kn_ref, ckv_hbm, kpe_hbm, o_ref,
                       ckv_buf, kpe_buf, sems, ckb_ref, kpb_ref):
    b = pl.program_id(0)
    copies = functools.partial(_page_copies, pt_ref, ckv_hbm, kpe_hbm, ckv_buf, kpe_buf, sems)
    to_log2 = SOFTMAX_SCALE * LOG2_E

    @pl.when(b == 0)
    def _():
        for c in range(N_PAGE_STEPS):
            for cp in copies(b, c):
                cp.start()

    ql = ql_ref[0]
    qp = qp_ref[0]

    def scores(c):
        for cp in copies(b, c):
            cp.wait()
        half = c % 2
        ckb_ref[half] = ckv_buf[c].reshape(KEYS_PER_STEP, KV_RANK).astype(bf16)
        kpb_ref[half] = jnp.concatenate([kpe_buf[c, p] for p in range(PAGES_PER_STEP)], axis=1).astype(bf16)
        return (_dot_nt(ql, ckb_ref[half]) + _dot(qp, kpb_ref[half])) * to_log2

    t_new = (jnp.sum(ql.astype(f32) * cn_ref[0], axis=1, keepdims=True)
             + jnp.sum(qp.astype(f32) * kn_ref[0], axis=1, keepdims=True)) * to_log2
    m = jnp.broadcast_to(t_new, (MLA_HEADS, LANES))
    l = jnp.ones((MLA_HEADS, LANES), f32)
    acc = jnp.broadcast_to(cn_ref[0], (MLA_HEADS, KV_RANK))

    t = scores(0)
    for c in range(N_PAGE_STEPS):
        t_next = scores(c + 1) if c + 1 < N_PAGE_STEPS else None
        m_next = jnp.maximum(m, jnp.max(t, axis=1, keepdims=True))
        p = jnp.exp2(t - jnp.concatenate([m_next] * (KEYS_PER_STEP // LANES), axis=1))
        alpha = jnp.exp2(m - m_next)
        l = alpha * l + jnp.sum(p, axis=1, keepdims=True)
        acc = jnp.concatenate([alpha] * (KV_RANK // LANES), axis=1) * acc + _dot(p.astype(bf16), ckb_ref[c % 2])
        m = m_next

        @pl.when(b + 1 < pl.num_programs(0))
        def _():
            for cp in copies(b + 1, c):
                cp.start()

        t = t_next

    o_ref[0] = acc / jnp.concatenate([l] * (KV_RANK // LANES), axis=1)


def _paged_attention(page_table, q_lat, q_pe, ckv_new, kpe_new, cache_ckv, cache_krope_t):
    seq_spec = lambda r, d: pl.BlockSpec((1, r, d), lambda b, pt: (b, 0, 0))
    grid_spec = pltpu.PrefetchScalarGridSpec(
        num_scalar_prefetch=1,
        grid=(DEC_BATCH,),
        in_specs=[
            seq_spec(MLA_HEADS, KV_RANK), seq_spec(MLA_HEADS, QK_ROPE),
            seq_spec(1, KV_RANK), seq_spec(1, QK_ROPE),
            pl.BlockSpec(memory_space=pl.ANY), pl.BlockSpec(memory_space=pl.ANY),
        ],
        out_specs=seq_spec(MLA_HEADS, KV_RANK),
        scratch_shapes=[
            pltpu.VMEM((N_PAGE_STEPS, PAGES_PER_STEP, PAGE_SIZE, KV_RANK), f32),
            pltpu.VMEM((N_PAGE_STEPS, PAGES_PER_STEP, QK_ROPE, PAGE_SIZE), f32),
            pltpu.SemaphoreType.DMA((2, N_PAGE_STEPS)),
            pltpu.VMEM((2, KEYS_PER_STEP, KV_RANK), bf16),
            pltpu.VMEM((2, QK_ROPE, KEYS_PER_STEP), bf16),
        ],
    )
    return pl.pallas_call(
        _paged_attn_kernel,
        grid_spec=grid_spec,
        out_shape=jax.ShapeDtypeStruct((DEC_BATCH, MLA_HEADS, KV_RANK), f32),
        compiler_params=_params(("arbitrary",)),
        name="paged_attn",
    )(page_table, q_lat, q_pe, ckv_new, kpe_new, cache_ckv, cache_krope_t)


def _v_up_kernel(o_ref, wuv_ref, out_ref):
    out_ref[...] = _dot(o_ref[0].astype(bf16), wuv_ref[...]).astype(bf16)


def _v_up(o_lat_h, wkv):
    return pl.pallas_call(
        _v_up_kernel,
        grid=(MLA_HEADS,),
        in_specs=[
            pl.BlockSpec((1, DEC_BATCH, KV_RANK), lambda h: (h, 0, 0)),
            pl.BlockSpec((KV_RANK, V_HEAD), lambda h: (0, 2 * h + 1)),
        ],
        out_specs=pl.BlockSpec((DEC_BATCH, V_HEAD), lambda h: (0, h)),
        out_shape=jax.ShapeDtypeStruct((DEC_BATCH, MLA_WIDTH), bf16),
        compiler_params=_params(("arbitrary",)),
        name="v_up",
    )(o_lat_h, wkv)


def _out_proj_kernel(oh_ref, oa_ref, head_hbm, xp_hbm, xs_hbm, wh_ref, wa_ref, g_ref, o_ref, x_buf, sems):
    i = pl.program_id(0)
    _x_tile_ready(functools.partial(_x_tile_copies, head_hbm, xp_hbm, xs_hbm, x_buf, sems), i)
    mix = _dot(oh_ref[...], wh_ref[...]) + _dot(oa_ref[...], wa_ref[...])
    o_ref[...] = x_buf[i % 2] + _rms(mix, g_ref[...])


def _out_proj(o_h, o_a, x_head, x_prompt, x_sample, w_out_b, g):
    hbm = pl.BlockSpec(memory_space=pl.ANY)
    return pl.pallas_call(
        _out_proj_kernel,
        grid=(N_TILES,),
        in_specs=[
            pl.BlockSpec((TM, HGRN_WIDTH), lambda i: (i, 0)),
            pl.BlockSpec((TM, MLA_WIDTH), lambda i: (i, 0)),
            hbm, hbm, hbm,
            pl.BlockSpec((HGRN_WIDTH, D_MODEL), lambda i: (0, 0)),
            pl.BlockSpec((MLA_WIDTH, D_MODEL), lambda i: (1, 0)),
            pl.BlockSpec((1, D_MODEL), lambda i: (0, 0)),
        ],
        out_specs=pl.BlockSpec((TM, D_MODEL), lambda i: (i, 0)),
        out_shape=jax.ShapeDtypeStruct((R_ALL, D_MODEL), f32),
        scratch_shapes=[pltpu.VMEM((2, TM, D_MODEL), f32), pltpu.SemaphoreType.DMA((2,))],
        compiler_params=_params(("arbitrary",)),
        name="out_proj",
    )(o_h, o_a, x_head, x_prompt, x_sample, w_out_b, w_out_b, g)


def _ffn_out_copies(acc_ref, sems, yp_hbm, ys_hbm, tile, slot, kind):
    if kind == "first":
        return [pltpu.make_async_copy(acc_ref.at[slot, pl.ds(P_START, TM - P_START)],
                                      yp_hbm.at[pl.ds(0, TM - P_START)], sems.at[slot])]
    if kind == "middle":
        return [pltpu.make_async_copy(acc_ref.at[slot],
                                      yp_hbm.at[pl.ds(pl.multiple_of(tile * TM - P_START, CHUNK), TM)],
                                      sems.at[slot])]
    return [pltpu.make_async_copy(acc_ref.at[slot, pl.ds(0, SAMPLE_LOCAL)],
                                  yp_hbm.at[pl.ds(SEQ - SAMPLE_LOCAL, SAMPLE_LOCAL)], sems.at[slot]),
            pltpu.make_async_copy(acc_ref.at[slot, pl.ds(SAMPLE_LOCAL, DEC_BATCH)], ys_hbm, sems.at[slot])]


def _ffn_kernel(x_ref, gpre_ref, gpost_ref, wg_ref, wv_ref, cw_ref, cb_ref, wd_ref, buf0_ref, buf1_ref,
                yp_hbm, ys_hbm, tail_ref, h_ref, act_ref, carry_ref, acc_ref, sems):
    i = pl.program_id(0)
    j = pl.program_id(1)
    last_i = pl.num_programs(0) - 1
    slot = i % 2
    col = pl.ds(pl.multiple_of(j * FF_TN, FF_TN), FF_TN)
    out_copies = functools.partial(_ffn_out_copies, acc_ref, sems, yp_hbm, ys_hbm)

    @pl.when(j == 0)
    def _():
        @pl.when(i == 2)
        def _():
            for cp in out_copies(0, slot, "first"):
                cp.wait()

        @pl.when(i > 2)
        def _():
            for cp in out_copies(i - 2, slot, "middle"):
                cp.wait()

        h_ref[...] = _rms(x_ref[...], gpre_ref[...]).astype(bf16)
        acc_ref[slot] = jnp.zeros((TM, D_MODEL), f32)

    @pl.when(i == 0)
    def _():
        carry_ref[:, col] = jnp.zeros((8, FF_TN), f32)

    h = h_ref[...]
    row = lax.broadcasted_iota(jnp.int32, (TM, 1), 0)
    g = jnp.where(row + i * TM >= PAD, _dot(h, wg_ref[...]), 0.0)
    val = _dot(h, wv_ref[...])
    prev = carry_ref[:, col]
    g1 = jnp.where(row == 0, prev[7:8], pltpu.roll(g, 1, 0))
    g2 = jnp.where(row == 0, prev[6:7], jnp.where(row == 1, prev[7:8], pltpu.roll(g, 2, 0)))
    w0, w1, w2 = cw_ref[0:1, :], cw_ref[1:2, :], cw_ref[2:3, :]
    y = cb_ref[...] + w0 * g2 + w1 * g1 + w2 * g
    act_ref[...] = (_silu(y) * val).astype(bf16)
    carry_ref[:, col] = g[TM - 8:, :]
    tail_ref[0] = g[TM - TAIL_ROWS:, :]

    @pl.when(i == last_i)
    def _():
        ys = cb_ref[...] + w0 * buf0_ref[...] + w1 * buf1_ref[...] + w2 * g[SAMPLE_LOCAL:, :]
        act_ref[SAMPLE_LOCAL:, :] = (_silu(ys) * val[SAMPLE_LOCAL:, :]).astype(bf16)

    acc_ref[slot] += _dot(act_ref[...], wd_ref[...])

    @pl.when(j == pl.num_programs(1) - 1)
    def _():
        acc_ref[slot] = x_ref[...] + _rms(acc_ref[slot], gpost_ref[...])

        @pl.when(i == 0)
        def _():
            for cp in out_copies(i, slot, "first"):
                cp.start()

        @pl.when((i > 0) & (i < last_i))
        def _():
            for cp in out_copies(i, slot, "middle"):
                cp.start()

        @pl.when(i == last_i)
        def _():
            for cp in out_copies(i, slot, "last"):
                cp.start()
            for cp in out_copies(i - 1, 1 - slot, "middle"):
                cp.wait()
            for cp in out_copies(i, slot, "last"):
                cp.wait()


def _ffn(x1, g_pre, g_post, w_up_b, conv_w, conv_b, w_down_b, buf0, buf1):
    n_ff = D_FF // FF_TN
    n_tiles = R_ALL // TM
    sample_buf = pl.BlockSpec((DEC_BATCH, FF_TN), lambda i, j: (0, jnp.where(i == n_tiles - 1, j, 0)))
    return pl.pallas_call(
        _ffn_kernel,
        grid=(R_ALL // TM, n_ff),
        in_specs=[
            pl.BlockSpec((TM, D_MODEL), lambda i, j: (i, 0)),
            pl.BlockSpec((1, D_MODEL), lambda i, j: (0, 0)),
            pl.BlockSpec((1, D_MODEL), lambda i, j: (0, 0)),
            pl.BlockSpec((D_MODEL, FF_TN), lambda i, j: (0, j)),
            pl.BlockSpec((D_MODEL, FF_TN), lambda i, j: (0, j + n_ff)),
            pl.BlockSpec((3, FF_TN), lambda i, j: (0, j)),
            pl.BlockSpec((1, FF_TN), lambda i, j: (0, j)),
            pl.BlockSpec((FF_TN, D_MODEL), lambda i, j: (j, 0)),
            sample_buf, sample_buf,
        ],
        out_specs=[
            pl.BlockSpec(memory_space=pl.ANY),
            pl.BlockSpec(memory_space=pl.ANY),
            pl.BlockSpec((1, TAIL_ROWS, FF_TN), lambda i, j: (i, 0, j)),
        ],
        out_shape=[
            jax.ShapeDtypeStruct((SEQ, D_MODEL), f32),
            jax.ShapeDtypeStruct((DEC_BATCH, D_MODEL), f32),
            jax.ShapeDtypeStruct((R_ALL // TM, TAIL_ROWS, D_FF), f32),
        ],
        scratch_shapes=[
            pltpu.VMEM((TM, D_MODEL), bf16),
            pltpu.VMEM((TM, FF_TN), bf16),
            pltpu.VMEM((8, D_FF), f32),
            pltpu.VMEM((2, TM, D_MODEL), f32),
            pltpu.SemaphoreType.DMA((2,)),
        ],
        compiler_params=_params(("arbitrary", "arbitrary")),
        name="conv_ffn",
    )(x1, g_pre, g_post, w_up_b, w_up_b, conv_w, conv_b, w_down_b, buf0, buf1)


def kernel(x_prompt, x_sample, cache_ckv, cache_krope, state_hgrn, state_conv, page_table, meta_tokens,
           lb_raw, g_mix_pre, g_mix_post, g_ffn_pre, g_ffn_post, w_in, hgrn_o_norm, q_a_norm, kv_a_norm,
           w_q_up, w_kv_up, w_out, w_ffn_up, conv_w, conv_b, w_ffn_down):
    x_head = jnp.concatenate([jnp.zeros((PAD, D_MODEL), f32), meta_tokens.astype(f32)], axis=0)
    x_p = x_prompt[0]
    x_s = x_sample[:, 0]

    w_in_t = w_in[0].T.astype(bf16)
    w_in_tail = jnp.pad(w_in_t[IN_FULL_TILES * IN_TN:], ((0, IN_PAD - IN_TOTAL), (0, 0)))
    wq = w_q_up[0].reshape(Q_RANK, MLA_HEADS, QK_DIM)
    wq_p = jnp.concatenate([wq[:, :, :QK_NOPE].reshape(Q_RANK, -1), wq[:, :, QK_NOPE:].reshape(Q_RANK, -1)],
                           axis=1).astype(bf16)
    wkv = w_kv_up[0].astype(bf16)
    w_out_b = w_out[0].astype(bf16)
    w_up_b = w_ffn_up[0].astype(bf16)
    w_down_b = w_ffn_down[0].astype(bf16)

    proj = _in_proj(x_head, x_p, x_s, g_mix_pre, w_in_t, w_in_tail)
    qh, kh, vh, ckv, kpe = _mla_proj(proj, q_a_norm, kv_a_norm, wq_p, wkv)

    o_a = _attention(qh, kh, vh)
    o_h, s_p = _hgrn_prompt(proj, lb_raw, hgrn_o_norm)

    o_h, s_s = _hgrn_sample(proj, lb_raw, hgrn_o_norm, state_hgrn[0], o_h)
    q_lat = _q_latent(qh, wkv).transpose(1, 0, 2)
    q_pe_s = qh[:, P_ROWS:, QK_NOPE:].transpose(1, 0, 2)
    ckv_s = ckv[P_ROWS:]
    kpe_s = kpe[P_ROWS:]
    o_lat = _paged_attention(page_table, q_lat, q_pe_s, ckv_s[:, None, :], kpe_s[:, None, :],
                             cache_ckv, cache_krope.transpose(0, 1, 3, 2))
    o_a = jnp.concatenate([o_a, _v_up(o_lat.transpose(1, 0, 2), wkv)], axis=0)

    x1 = _out_proj(o_h, o_a, x_head, x_p, x_s, w_out_b, g_mix_post)
    y_p, y_s, tails = _ffn(x1, g_ffn_pre, g_ffn_post, w_up_b, conv_w[0], conv_b, w_down_b,
                           state_conv[0, :, 0], state_conv[0, :, 1])
    tail = tails[-1]

    n_tail = TAIL_ROWS - DEC_BATCH
    conv_p = tail[n_tail - 2:n_tail][None, None]
    conv_s = jnp.stack([state_conv[0, :, 1], tail[n_tail:]], axis=1)[None]
    return (y_p[None], y_s[:, None],
            ckv[PAD:P_ROWS][None, None], kpe[PAD:P_ROWS][None, None],
            ckv_s[None, :, None], kpe_s[None, :, None],
            s_p[None, None], s_s[None], conv_p, conv_s)
```

```python
import functools

import numpy as np
import jax
import jax.numpy as jnp
from jax import lax
from jax.experimental import pallas as pl
from jax.experimental.pallas import tpu as pltpu

D_MODEL = 2048
SEQ = 8192
N_META = 16
DEC_BATCH = 128
PAST_LEN = 16384
PAGE_SIZE = 128
N_PAGES = PAST_LEN // PAGE_SIZE
HGRN_HEADS = 8
HGRN_K = 128
HGRN_V = 128
HGRN_WIDTH = HGRN_HEADS * HGRN_V
MLA_HEADS = 8
QK_NOPE = 128
QK_ROPE = 64
QK_DIM = QK_NOPE + QK_ROPE
V_HEAD = 128
Q_RANK = 512
KV_RANK = 256
MLA_WIDTH = MLA_HEADS * V_HEAD
ROPE_THETA = 10000.0
SOFTMAX_SCALE = QK_DIM ** -0.5
LOG2_E = 1.4426950408889634
D_FF = 5632
EPS = 1e-6
IN_TOTAL = 4 * HGRN_WIDTH + Q_RANK + KV_RANK + QK_ROPE

CHUNK = 128
PAD = CHUNK - N_META
P_START = PAD + N_META
P_ROWS = P_START + SEQ
R_ALL = P_ROWS + DEC_BATCH
N_CHUNKS = P_ROWS // CHUNK

LANES = 128
SUBLANES = 8
IN_PAD = 5120
IN_TN = 1280
IN_FULL_TILES = IN_TOTAL // IN_TN
TM = 768
TQ = 640
TQ_SUB = 640
FF_TN = 512
HGRN_SEQS = 16
HGRN_STAGE_SEQS = 8
SAMPLE_LOCAL = P_ROWS - (R_ALL // TM - 1) * TM
TAIL_ROWS = 136
PAGES_PER_STEP = 16
N_PAGE_STEPS = N_PAGES // PAGES_PER_STEP
KEYS_PER_STEP = PAGES_PER_STEP * PAGE_SIZE
NEG_BIG = -1e30
VMEM_LIMIT = 56 * 1024 * 1024

NT_DIMS = (((1,), (1,)), ((), ()))
TN_DIMS = (((0,), (0,)), ((), ()))

f32 = jnp.float32
bf16 = jnp.bfloat16


def _params(sem, vmem=VMEM_LIMIT):
    return pltpu.CompilerParams(dimension_semantics=sem, vmem_limit_bytes=vmem)


def _rms(x, g):
    return x * lax.rsqrt(jnp.mean(x * x, axis=-1, keepdims=True) + EPS) * g


def _sigmoid(x):
    return 1.0 / (1.0 + jnp.exp(-x))


def _silu(x):
    return x * _sigmoid(x)


def _dot(a, b):
    return jnp.dot(a, b, preferred_element_type=f32)


def _dot_nt(a, b):
    return lax.dot_general(a, b, NT_DIMS, preferred_element_type=f32)


def _dot_tn(a, b):
    return lax.dot_general(a, b, TN_DIMS, preferred_element_type=f32)


N_TILES = R_ALL // TM


def _x_tile_copies(head_hbm, xp_hbm, xs_hbm, buf, sems, tile, kind):
    slot = tile % 2
    body = TM - P_START
    if kind == "first":
        return [pltpu.make_async_copy(head_hbm, buf.at[slot, pl.ds(0, P_START)], sems.at[slot]),
                pltpu.make_async_copy(xp_hbm.at[pl.ds(0, body)], buf.at[slot, pl.ds(P_START, body)], sems.at[slot])]
    if kind == "middle":
        return [pltpu.make_async_copy(xp_hbm.at[pl.ds(pl.multiple_of(tile * TM - P_START, CHUNK), TM)],
                                      buf.at[slot], sems.at[slot])]
    return [pltpu.make_async_copy(xp_hbm.at[pl.ds(SEQ - body, body)], buf.at[slot, pl.ds(0, body)], sems.at[slot]),
            pltpu.make_async_copy(xs_hbm, buf.at[slot, pl.ds(body, DEC_BATCH)], sems.at[slot])]


def _x_tile_ready(copies, i):
    last = N_TILES - 1

    @pl.when(i == 0)
    def _():
        for cp in copies(i, "first"):
            cp.start()
        for cp in copies(i, "first"):
            cp.wait()
        for cp in copies(i + 1, "middle"):
            cp.start()

    @pl.when((i > 0) & (i < last - 1))
    def _():
        for cp in copies(i, "middle"):
            cp.wait()
        for cp in copies(i + 1, "middle"):
            cp.start()

    @pl.when(i == last - 1)
    def _():
        for cp in copies(i, "middle"):
            cp.wait()
        for cp in copies(i + 1, "last"):
            cp.start()

    @pl.when(i == last)
    def _():
        for cp in copies(i, "last"):
            cp.wait()


def _in_proj_kernel(head_hbm, xp_hbm, xs_hbm, g_ref, w_ref, wtail_ref, o_ref, h_ref, x_buf, sems):
    i = pl.program_id(0)
    j = pl.program_id(1)

    @pl.when(j == 0)
    def _():
        _x_tile_ready(functools.partial(_x_tile_copies, head_hbm, xp_hbm, xs_hbm, x_buf, sems), i)
        h_ref[...] = _rms(x_buf[i % 2], g_ref[...]).astype(bf16)

    @pl.when(j < IN_FULL_TILES)
    def _():
        o_ref[...] = _dot_nt(h_ref[...], w_ref[...])

    @pl.when(j >= IN_FULL_TILES)
    def _():
        o_ref[...] = _dot_nt(h_ref[...], wtail_ref[...])


def _in_proj(x_head, x_prompt, x_sample, g, w_in_t, w_in_tail):
    hbm = pl.BlockSpec(memory_space=pl.ANY)
    return pl.pallas_call(
        _in_proj_kernel,
        grid=(N_TILES, IN_PAD // IN_TN),
        in_specs=[
            hbm, hbm, hbm,
            pl.BlockSpec((1, D_MODEL), lambda i, j: (0, 0)),
            pl.BlockSpec((IN_TN, D_MODEL), lambda i, j: (jnp.minimum(j, IN_FULL_TILES - 1), 0)),
            pl.BlockSpec((IN_TN, D_MODEL), lambda i, j: (0, 0)),
        ],
        out_specs=pl.BlockSpec((TM, IN_TN), lambda i, j: (i, j)),
        out_shape=jax.ShapeDtypeStruct((R_ALL, IN_PAD), f32),
        scratch_shapes=[pltpu.VMEM((TM, D_MODEL), bf16), pltpu.VMEM((2, TM, D_MODEL), f32),
                        pltpu.SemaphoreType.DMA((2,))],
        compiler_params=_params(("arbitrary", "arbitrary")),
        name="in_proj",
    )(x_head, x_prompt, x_sample, g, w_in_t, w_in_tail)


def _rope_tables(row0, n_rows, cos_tab, sin_tab):
    lane = lax.broadcasted_iota(jnp.int32, (1, LANES), 1)
    half = QK_ROPE // 2
    fidx = (lane & (half - 1)).astype(f32)
    inv = jnp.power(jnp.float32(ROPE_THETA), -fidx / half)

    @pl.when(row0 == 0)
    def _():
        r = lax.broadcasted_iota(jnp.int32, (n_rows, 1), 0).astype(f32)
        cos_tab[...] = jnp.cos(r * inv)
        sin_tab[...] = jnp.sin(r * inv)

    a0 = (row0 - PAD).astype(f32) * inv
    a_s = jnp.float32(PAST_LEN) * inv
    c0, s0 = jnp.cos(a0), jnp.sin(a0)
    in_prompt = row0 + lax.broadcasted_iota(jnp.int32, (n_rows, 1), 0) < P_ROWS
    cos = jnp.where(in_prompt, c0 * cos_tab[...] - s0 * sin_tab[...], jnp.cos(a_s))
    sin = jnp.where(in_prompt, s0 * cos_tab[...] + c0 * sin_tab[...], jnp.sin(a_s))
    return cos, jnp.where((lane & half) == 0, -sin, sin)


def _rotate_half(x, cos, sin_signed):
    n = x.shape[1]
    half = QK_ROPE // 2
    first_half = (lax.broadcasted_iota(jnp.int32, (1, n), 1) & half) == 0
    partner = jnp.where(first_half, pltpu.roll(x, n - half, 1), pltpu.roll(x, half, 1))
    return x * cos + partner * sin_signed


def _mla_proj_kernel(qd_ref, kvd_ref, kr_ref, qn_ref, kvn_ref, wq_ref, wkv_ref,
                     qh_ref, kh_ref, vh_ref, ckv_ref, kpe_ref, cos_tab, sin_tab):
    tm = qd_ref.shape[0]
    cos, sin_s = _rope_tables(pl.program_id(0) * tm, tm, cos_tab, sin_tab)

    q = _dot(_rms(qd_ref[...], qn_ref[...]).astype(bf16), wq_ref[...])
    reps = MLA_HEADS * QK_ROPE // LANES
    q_pe = _rotate_half(q[:, MLA_HEADS * QK_NOPE:],
                        jnp.concatenate([cos] * reps, axis=1),
                        jnp.concatenate([sin_s] * reps, axis=1))

    k_pe = _rotate_half(kr_ref[...], cos, sin_s)[:, :QK_ROPE]
    kpe_ref[...] = k_pe

    c_kv = _rms(kvd_ref[...], kvn_ref[...])
    ckv_ref[...] = c_kv
    kv = _dot(c_kv.astype(bf16), wkv_ref[...])

    k_pe_b = k_pe.astype(bf16)
    for h in range(MLA_HEADS):
        qh_ref[h, :, :QK_NOPE] = q[:, h * QK_NOPE:(h + 1) * QK_NOPE].astype(bf16)
        qh_ref[h, :, QK_NOPE:] = q_pe[:, h * QK_ROPE:(h + 1) * QK_ROPE].astype(bf16)
        base = h * (QK_NOPE + V_HEAD)
        kh_ref[h, :, :QK_NOPE] = kv[:, base:base + QK_NOPE].astype(bf16)
        kh_ref[h, :, QK_NOPE:] = k_pe_b
        vh_ref[h] = kv[:, base + QK_NOPE:base + QK_NOPE + V_HEAD].astype(bf16)


def _mla_proj(proj, q_a_norm, kv_a_norm, wq_p, wkv):
    col_q = (4 * HGRN_WIDTH) // Q_RANK
    col_kv = (4 * HGRN_WIDTH + Q_RANK) // KV_RANK
    col_kr = (4 * HGRN_WIDTH + Q_RANK + KV_RANK) // LANES
    head_spec = lambda d: pl.BlockSpec((MLA_HEADS, TM, d), lambda i: (0, i, 0))
    return pl.pallas_call(
        _mla_proj_kernel,
        grid=(R_ALL // TM,),
        in_specs=[
            pl.BlockSpec((TM, Q_RANK), lambda i: (i, col_q)),
            pl.BlockSpec((TM, KV_RANK), lambda i: (i, col_kv)),
            pl.BlockSpec((TM, LANES), lambda i: (i, col_kr)),
            pl.BlockSpec((1, Q_RANK), lambda i: (0, 0)),
            pl.BlockSpec((1, KV_RANK), lambda i: (0, 0)),
            pl.BlockSpec(wq_p.shape, lambda i: (0, 0)),
            pl.BlockSpec(wkv.shape, lambda i: (0, 0)),
        ],
        out_specs=[
            head_spec(QK_DIM), head_spec(QK_DIM), head_spec(V_HEAD),
            pl.BlockSpec((TM, KV_RANK), lambda i: (i, 0)),
            pl.BlockSpec((TM, QK_ROPE), lambda i: (i, 0)),
        ],
        out_shape=[
            jax.ShapeDtypeStruct((MLA_HEADS, R_ALL, QK_DIM), bf16),
            jax.ShapeDtypeStruct((MLA_HEADS, R_ALL, QK_DIM), bf16),
            jax.ShapeDtypeStruct((MLA_HEADS, R_ALL, V_HEAD), bf16),
            jax.ShapeDtypeStruct((R_ALL, KV_RANK), f32),
            jax.ShapeDtypeStruct((R_ALL, QK_ROPE), f32),
        ],
        scratch_shapes=[pltpu.VMEM((TM, LANES), f32), pltpu.VMEM((TM, LANES), f32)],
        compiler_params=_params(("arbitrary",)),
        name="mla_proj",
    )(proj, proj, proj, q_a_norm, kv_a_norm, wq_p, wkv)


def _attn_kernel(q_ref, k_ref, v_ref, o_ref, m_ref, l_ref, acc_ref, ta_ref, tb_ref):
    i = pl.program_id(1)
    q = q_ref[0]
    m_ref[...] = jnp.full(m_ref.shape, NEG_BIG, f32)
    l_ref[...] = jnp.zeros(l_ref.shape, f32)
    acc_ref[...] = jnp.zeros(acc_ref.shape, f32)
    reps = TQ // LANES
    groups = range(0, TQ, TQ_SUB)

    def logits(t_ref, j):
        t_ref[...] = _dot_nt(q, k_ref[0, pl.ds(pl.multiple_of(j * TQ, TQ), TQ), :])

    def block(t_ref, j, masked):
        v = v_ref[0, pl.ds(pl.multiple_of(j * TQ, TQ), TQ), :]
        for r0 in groups:
            rows = slice(r0, r0 + TQ_SUB)
            t = t_ref[rows] * (SOFTMAX_SCALE * LOG2_E)
            if masked:
                q_row = i * TQ + r0 + lax.broadcasted_iota(jnp.int32, (TQ_SUB, TQ), 0)
                k_row = j * TQ + lax.broadcasted_iota(jnp.int32, (TQ_SUB, TQ), 1)
                t = jnp.where(k_row <= q_row, jnp.where(k_row >= PAD, t, NEG_BIG), NEG_BIG)
            m_prev = m_ref[rows]
            m_next = jnp.maximum(m_prev, jnp.max(t, axis=1, keepdims=True))
            p = jnp.exp2(t - jnp.concatenate([m_next] * reps, axis=1))
            alpha = jnp.exp2(m_prev - m_next)
            l_ref[rows] = alpha * l_ref[rows] + jnp.sum(p, axis=1, keepdims=True)
            acc_ref[rows] = alpha * acc_ref[rows] + _dot(p.astype(bf16), v)
            m_ref[rows] = m_next

    logits(ta_ref, 0)

    @pl.when(i == 0)
    def _():
        block(ta_ref, 0, True)

    @pl.when(i > 0)
    def _():
        logits(tb_ref, 1)
        block(ta_ref, 0, True)
        n_inner = i - 1

        def pair(p, carry):
            j = 1 + 2 * p
            logits(ta_ref, j + 1)
            block(tb_ref, j, False)
            logits(tb_ref, j + 2)
            block(ta_ref, j + 1, False)
            return carry

        lax.fori_loop(0, n_inner // 2, pair, 0)
        j = 1 + 2 * (n_inner // 2)

        @pl.when(n_inner % 2 == 1)
        def _():
            logits(ta_ref, j + 1)
            block(tb_ref, j, False)
            block(ta_ref, j + 1, True)

        @pl.when(n_inner % 2 == 0)
        def _():
            block(tb_ref, j, True)

    o_ref[...] = (acc_ref[...] / l_ref[...]).astype(bf16)


def _attention(qh, kh, vh):
    return pl.pallas_call(
        _attn_kernel,
        grid=(MLA_HEADS, P_ROWS // TQ),
        in_specs=[
            pl.BlockSpec((1, TQ, QK_DIM), lambda h, i: (h, i, 0)),
            pl.BlockSpec((1, R_ALL, QK_DIM), lambda h, i: (h, 0, 0)),
            pl.BlockSpec((1, R_ALL, V_HEAD), lambda h, i: (h, 0, 0)),
        ],
        out_specs=pl.BlockSpec((TQ, V_HEAD), lambda h, i: (i, h)),
        out_shape=jax.ShapeDtypeStruct((P_ROWS, MLA_WIDTH), bf16),
        scratch_shapes=[pltpu.VMEM((TQ, LANES), f32), pltpu.VMEM((TQ, LANES), f32),
                        pltpu.VMEM((TQ, V_HEAD), f32),
                        pltpu.VMEM((TQ, TQ), f32), pltpu.VMEM((TQ, TQ), f32)],
        compiler_params=_params(("arbitrary", "arbitrary")),
        name="prompt_attn",
    )(qh, kh, vh)


N_LEVELS = 7


def _hgrn_constants():
    t = np.arange(CHUNK)
    sums = (t[None, :] <= t[:, None]).astype(np.float32)
    x = t[:, None] ^ t[None, :]
    lev = np.where(x > 0, np.floor(np.log2(np.maximum(x, 1))).astype(np.int32), N_LEVELS)
    lev = np.where(t[None, :] > t[:, None], N_LEVELS + 1, lev).astype(np.int32)
    return sums, lev


def _split3(x):
    a = x.astype(bf16)
    r = x - a.astype(f32)
    b = r.astype(bf16)
    c = (r - b.astype(f32)).astype(bf16)
    return a, b, c


def _lower_bound(lb_raw_ref):
    a0 = lb_raw_ref[0:1, :]
    a1 = lb_raw_ref[1:2, :]
    m = jnp.maximum(a0, a1)
    e0 = jnp.exp(a0 - m)
    return e0 / (e0 + jnp.exp(a1 - m))


def _pair_block_reference(b, lvl):
    m = 1 << lvl
    n, w = b.shape
    if m >= SUBLANES:
        g = m // SUBLANES
        b4 = b.reshape(n // (2 * m), 2 * g, SUBLANES, w)
        return jnp.broadcast_to(b4[:, g - 1:g, SUBLANES - 1:, :], b4.shape).reshape(n, w)
    b3 = b.reshape(n // SUBLANES, SUBLANES, w)
    sub = lax.broadcasted_iota(jnp.int32, (1, SUBLANES, 1), 1)
    ref = b3[:, m - 1:m, :]
    for start in range(2 * m, SUBLANES, 2 * m):
        ref = jnp.where(sub >= start, b3[:, start + m - 1:start + m, :], ref)
    return jnp.broadcast_to(ref, b3.shape).reshape(n, w)


def _hgrn_prompt_kernel(hq_ref, hf_ref, hi_ref, hg_ref, lb_ref, on_ref, sums_ref, lev_ref,
                        o_ref, s_ref, st_ref):
    c = pl.program_id(0)

    @pl.when(c == 0)
    def _():
        st_ref[...] = jnp.zeros(st_ref.shape, f32)

    @pl.when(c == N_CHUNKS)
    def _():
        o_ref[...] = jnp.zeros(o_ref.shape, bf16)

    @pl.when(c < N_CHUNKS)
    def _():
        lb = _lower_bound(lb_ref)
        f_all = lb + (1.0 - lb) * _sigmoid(hf_ref[...])
        l1, l2, l3 = _split3(jnp.log(f_all))
        sums = sums_ref[...]
        b_all = _dot(sums, l1) + _dot(sums, l2) + _dot(sums, l3)

        row = lax.broadcasted_iota(jnp.int32, (CHUNK, 1), 0)
        lev = lev_ref[...]
        refs = [_pair_block_reference(b_all, lvl) for lvl in range(N_LEVELS)]
        is_q = [((row >> lvl) & 1) == 1 for lvl in range(N_LEVELS)]
        sign = [jnp.where(m, 1.0, -1.0) for m in is_q]
        for h in range(HGRN_HEADS):
            sl = slice(h * HGRN_K, (h + 1) * HGRN_K)
            q = _silu(hq_ref[:, sl])
            k = 1.0 - f_all[:, sl]
            v = hi_ref[:, sl].astype(bf16)
            b = b_all[:, sl]
            a = jnp.where(lev == N_LEVELS, _dot_nt(q.astype(bf16), k.astype(bf16)), 0.0)
            for lvl in range(N_LEVELS):
                w = (jnp.where(is_q[lvl], q, k) * jnp.exp((b - refs[lvl][:, sl]) * sign[lvl])).astype(bf16)
                a = jnp.where(lev == lvl, _dot_nt(w, w), a)

            b_suf = b[CHUNK - 1:CHUNK, :] - b
            st = st_ref[h]
            o = _dot(a.astype(bf16), v) + _dot_nt((q * jnp.exp(b)).astype(bf16), st.astype(bf16))
            st_new = st * jnp.exp(b[CHUNK - 1:CHUNK, :]) + _dot_tn(v, (k * jnp.exp(b_suf)).astype(bf16))
            st_ref[h] = st_new
            o_ref[:, sl] = (_rms(o, on_ref[...]) * _silu(hg_ref[:, sl])).astype(bf16)

    @pl.when(c == N_CHUNKS - 1)
    def _():
        for h in range(HGRN_HEADS):
            s_ref[h] = st_ref[h].T


def _hgrn_prompt(proj, lb_raw, o_norm):
    sums, lev = _hgrn_constants()
    blk = lambda part: pl.BlockSpec((CHUNK, HGRN_WIDTH), lambda c: (c, part))
    return pl.pallas_call(
        _hgrn_prompt_kernel,
        grid=(N_CHUNKS + DEC_BATCH // CHUNK,),
        in_specs=[
            blk(0), blk(1), blk(2), blk(3),
            pl.BlockSpec((2, HGRN_WIDTH), lambda c: (0, 0)),
            pl.BlockSpec((1, HGRN_V), lambda c: (0, 0)),
            pl.BlockSpec(sums.shape, lambda c: (0, 0)),
            pl.BlockSpec(lev.shape, lambda c: (0, 0)),
        ],
        out_specs=[
            pl.BlockSpec((CHUNK, HGRN_WIDTH), lambda c: (c, 0)),
            pl.BlockSpec((HGRN_HEADS, HGRN_K, HGRN_V), lambda c: (0, 0, 0)),
        ],
        out_shape=[
            jax.ShapeDtypeStruct((R_ALL, HGRN_WIDTH), bf16),
            jax.ShapeDtypeStruct((HGRN_HEADS, HGRN_K, HGRN_V), f32),
        ],
        scratch_shapes=[pltpu.VMEM((HGRN_HEADS, HGRN_V, HGRN_K), f32)],
        compiler_params=_params(("arbitrary",)),
        name="hgrn_prompt",
    )(proj, proj, proj, proj, lb_raw, o_norm, jnp.asarray(sums, bf16), jnp.asarray(lev))


def _hgrn_sample_kernel(hq_ref, hf_ref, hi_ref, hg_ref, lb_ref, on_ref, s0_ref, oh_hbm, o_ref, s_ref):
    del oh_hbm
    lb = _lower_bound(lb_ref)
    f = lb + (1.0 - lb) * _sigmoid(hf_ref[...])
    q = _silu(hq_ref[...])
    k = (1.0 - f).astype(bf16).astype(f32)
    v = hi_ref[...].astype(bf16).astype(f32)
    gate = _silu(hg_ref[...])
    f1 = f.astype(bf16).astype(f32)
    f2 = (f - f1).astype(bf16).astype(f32)
    f3 = ((f - f1) - f2).astype(bf16).astype(f32)

    n_rows = 16
    rid = lax.broadcasted_iota(jnp.int32, (n_rows, HGRN_K), 0)
    rid2 = lax.broadcasted_iota(jnp.int32, (n_rows, 2 * HGRN_V), 0)
    left = lax.broadcasted_iota(jnp.int32, (n_rows, 2 * HGRN_V), 1) < HGRN_V
    zero_v = jnp.zeros((1, HGRN_V), f32)

    at = lambda x, s, h: x[s:s + 1, h * HGRN_K:(h + 1) * HGRN_K]

    def mix(s, h):
        lhs = jnp.where(rid == 0, at(f1, s, h), jnp.where(rid == 1, at(f2, s, h), jnp.where(
            rid == 2, at(f3, s, h), jnp.where(rid == 3, at(k, s, h), 0.0))))
        rhs = jnp.where(rid2 < 3, jnp.where(left, 1.0, 0.0),
                        jnp.where(rid2 == 3, jnp.concatenate([zero_v, at(v, s, h)], axis=1), 0.0))
        return _dot_tn(lhs.astype(bf16), rhs.astype(bf16))

    outs = []
    for s0 in range(0, HGRN_SEQS, HGRN_STAGE_SEQS):
        pairs = [(s, h) for s in range(s0, s0 + HGRN_STAGE_SEQS) for h in range(HGRN_HEADS)]
        mixes = [mix(s, h) for s, h in pairs]
        for (s, h), mx in zip(pairs, mixes):
            s_ref[s, h] = mx[:, :HGRN_V] * s0_ref[s, h] + mx[:, HGRN_V:]
        o_raw = [_dot(jnp.where(rid == 0, at(q, s, h), 0.0).astype(bf16), s_ref[s, h].astype(bf16))[0:1]
                 for s, h in pairs]
        outs += [_rms(o, on_ref[...]) * at(gate, s, h) for (s, h), o in zip(pairs, o_raw)]
    o_ref[...] = jnp.concatenate(
        [jnp.concatenate(outs[s * HGRN_HEADS:(s + 1) * HGRN_HEADS], axis=1) for s in range(HGRN_SEQS)],
        axis=0).astype(bf16)


def _hgrn_sample(proj, lb_raw, o_norm, state, o_h):
    row_blk = P_ROWS // HGRN_SEQS
    blk = lambda part: pl.BlockSpec((HGRN_SEQS, HGRN_WIDTH), lambda b: (row_blk + b, part))
    state_spec = pl.BlockSpec((HGRN_SEQS, HGRN_HEADS, HGRN_K, HGRN_V), lambda b: (b, 0, 0, 0))
    return pl.pallas_call(
        _hgrn_sample_kernel,
        grid=(DEC_BATCH // HGRN_SEQS,),
        in_specs=[
            blk(0), blk(1), blk(2), blk(3),
            pl.BlockSpec((2, HGRN_WIDTH), lambda b: (0, 0)),
            pl.BlockSpec((1, HGRN_V), lambda b: (0, 0)),
            state_spec,
            pl.BlockSpec(memory_space=pl.ANY),
        ],
        out_specs=[blk(0), state_spec],
        out_shape=[
            jax.ShapeDtypeStruct(o_h.shape, o_h.dtype),
            jax.ShapeDtypeStruct(state.shape, f32),
        ],
        input_output_aliases={7: 0},
        compiler_params=_params(("arbitrary",)),
        name="hgrn_sample",
    )(proj, proj, proj, proj, lb_raw, o_norm, state, o_h)


def _q_latent_kernel(q_ref, wuk_ref, o_ref):
    o_ref[0] = _dot_nt(q_ref[0][:, :QK_NOPE], wuk_ref[...]).astype(bf16)


def _q_latent(qh, wkv):
    row_blk = P_ROWS // DEC_BATCH
    return pl.pallas_call(
        _q_latent_kernel,
        grid=(MLA_HEADS,),
        in_specs=[
            pl.BlockSpec((1, DEC_BATCH, QK_DIM), lambda h: (h, row_blk, 0)),
            pl.BlockSpec((KV_RANK, QK_NOPE), lambda h: (0, 2 * h)),
        ],
        out_specs=pl.BlockSpec((1, DEC_BATCH, KV_RANK), lambda h: (h, 0, 0)),
        out_shape=jax.ShapeDtypeStruct((MLA_HEADS, DEC_BATCH, KV_RANK), bf16),
        compiler_params=_params(("arbitrary",)),
        name="q_latent",
    )(qh, wkv)


def _page_copies(pt_ref, ckv_hbm, kpe_hbm, ckv_buf, kpe_buf, sems, seq, c):
    copies = []
    for p in range(PAGES_PER_STEP):
        page = pt_ref[seq, c * PAGES_PER_STEP + p]
        copies.append(pltpu.make_async_copy(ckv_hbm.at[0, page], ckv_buf.at[c, p], sems.at[0, c]))
        copies.append(pltpu.make_async_copy(kpe_hbm.at[0, page], kpe_buf.at[c, p], sems.at[1, c]))
    return copies


def _paged_attn_kernel(pt_ref, ql_ref, qp_ref, cn_ref, kn_ref, ckv_hbm, kpe_hbm, o_ref,
                       ckv_buf, kpe_buf, sems, ckb_ref, kpb_ref):
    b = pl.program_id(0)
    copies = functools.partial(_page_copies, pt_ref, ckv_hbm, kpe_hbm, ckv_buf, kpe_buf, sems)
    to_log2 = SOFTMAX_SCALE * LOG2_E

    def start_pages(cps):
        for n, cp in enumerate(cps):
            cp.start(priority=(n // 2) % 2)

    @pl.when(b == 0)
    def _():
        for c in range(N_PAGE_STEPS):
            start_pages(copies(b, c))

    ql = ql_ref[0]
    qp = qp_ref[0]

    def scores(c):
        for cp in copies(b, c):
            cp.wait()
        half = c % 2
        ckb_ref[half] = ckv_buf[c].reshape(KEYS_PER_STEP, KV_RANK).astype(bf16)
        kpb_ref[half] = jnp.concatenate([kpe_buf[c, p] for p in range(PAGES_PER_STEP)], axis=1).astype(bf16)
        return (_dot_nt(ql, ckb_ref[half]) + _dot(qp, kpb_ref[half])) * to_log2

    t_new = (jnp.sum(ql.astype(f32) * cn_ref[0], axis=1, keepdims=True)
             + jnp.sum(qp.astype(f32) * kn_ref[0], axis=1, keepdims=True)) * to_log2
    m = jnp.broadcast_to(t_new, (MLA_HEADS, LANES))
    l = jnp.ones((MLA_HEADS, LANES), f32)
    acc = jnp.broadcast_to(cn_ref[0], (MLA_HEADS, KV_RANK))

    t = scores(0)
    for c in range(N_PAGE_STEPS):
        t_next = scores(c + 1) if c + 1 < N_PAGE_STEPS else None
        m_next = jnp.maximum(m, jnp.max(t, axis=1, keepdims=True))
        p = jnp.exp2(t - jnp.concatenate([m_next] * (KEYS_PER_STEP // LANES), axis=1))
        alpha = jnp.exp2(m - m_next)
        l = alpha * l + jnp.sum(p, axis=1, keepdims=True)
        acc = jnp.concatenate([alpha] * (KV_RANK // LANES), axis=1) * acc + _dot(p.astype(bf16), ckb_ref[c % 2])
        m = m_next

        @pl.when(b + 1 < pl.num_programs(0))
        def _():
            start_pages(copies(b + 1, c))

        t = t_next

    o_ref[0] = acc / jnp.concatenate([l] * (KV_RANK // LANES), axis=1)


def _paged_attention(page_table, q_lat, q_pe, ckv_new, kpe_new, cache_ckv, cache_krope_t):
    seq_spec = lambda r, d: pl.BlockSpec((1, r, d), lambda b, pt: (b, 0, 0))
    grid_spec = pltpu.PrefetchScalarGridSpec(
        num_scalar_prefetch=1,
        grid=(DEC_BATCH,),
        in_specs=[
            seq_spec(MLA_HEADS, KV_RANK), seq_spec(MLA_HEADS, QK_ROPE),
            seq_spec(1, KV_RANK), seq_spec(1, QK_ROPE),
            pl.BlockSpec(memory_space=pl.ANY), pl.BlockSpec(memory_space=pl.ANY),
        ],
        out_specs=seq_spec(MLA_HEADS, KV_RANK),
        scratch_shapes=[
            pltpu.VMEM((N_PAGE_STEPS, PAGES_PER_STEP, PAGE_SIZE, KV_RANK), f32),
            pltpu.VMEM((N_PAGE_STEPS, PAGES_PER_STEP, QK_ROPE, PAGE_SIZE), f32),
            pltpu.SemaphoreType.DMA((2, N_PAGE_STEPS)),
            pltpu.VMEM((2, KEYS_PER_STEP, KV_RANK), bf16),
            pltpu.VMEM((2, QK_ROPE, KEYS_PER_STEP), bf16),
        ],
    )
    return pl.pallas_call(
        _paged_attn_kernel,
        grid_spec=grid_spec,
        out_shape=jax.ShapeDtypeStruct((DEC_BATCH, MLA_HEADS, KV_RANK), f32),
        compiler_params=_params(("arbitrary",)),
        name="paged_attn",
    )(page_table, q_lat, q_pe, ckv_new, kpe_new, cache_ckv, cache_krope_t)


def _v_up_kernel(o_ref, wuv_ref, out_ref):
    out_ref[...] = _dot(o_ref[0].astype(bf16), wuv_ref[...]).astype(bf16)


def _v_up(o_lat_h, wkv):
    return pl.pallas_call(
        _v_up_kernel,
        grid=(MLA_HEADS,),
        in_specs=[
            pl.BlockSpec((1, DEC_BATCH, KV_RANK), lambda h: (h, 0, 0)),
            pl.BlockSpec((KV_RANK, V_HEAD), lambda h: (0, 2 * h + 1)),
        ],
        out_specs=pl.BlockSpec((DEC_BATCH, V_HEAD), lambda h: (0, h)),
        out_shape=jax.ShapeDtypeStruct((DEC_BATCH, MLA_WIDTH), bf16),
        compiler_params=_params(("arbitrary",)),
        name="v_up",
    )(o_lat_h, wkv)


def _out_proj_kernel(oh_ref, oa_ref, head_hbm, xp_hbm, xs_hbm, wh_ref, wa_ref, g_ref, o_ref, x_buf, sems):
    i = pl.program_id(0)
    _x_tile_ready(functools.partial(_x_tile_copies, head_hbm, xp_hbm, xs_hbm, x_buf, sems), i)
    mix = _dot(oh_ref[...], wh_ref[...]) + _dot(oa_ref[...], wa_ref[...])
    o_ref[...] = x_buf[i % 2] + _rms(mix, g_ref[...])


def _out_proj(o_h, o_a, x_head, x_prompt, x_sample, w_out_b, g):
    hbm = pl.BlockSpec(memory_space=pl.ANY)
    return pl.pallas_call(
        _out_proj_kernel,
        grid=(N_TILES,),
        in_specs=[
            pl.BlockSpec((TM, HGRN_WIDTH), lambda i: (i, 0)),
            pl.BlockSpec((TM, MLA_WIDTH), lambda i: (i, 0)),
            hbm, hbm, hbm,
            pl.BlockSpec((HGRN_WIDTH, D_MODEL), lambda i: (0, 0)),
            pl.BlockSpec((MLA_WIDTH, D_MODEL), lambda i: (1, 0)),
            pl.BlockSpec((1, D_MODEL), lambda i: (0, 0)),
        ],
        out_specs=pl.BlockSpec((TM, D_MODEL), lambda i: (i, 0)),
        out_shape=jax.ShapeDtypeStruct((R_ALL, D_MODEL), f32),
        scratch_shapes=[pltpu.VMEM((2, TM, D_MODEL), f32), pltpu.SemaphoreType.DMA((2,))],
        compiler_params=_params(("arbitrary",)),
        name="out_proj",
    )(o_h, o_a, x_head, x_prompt, x_sample, w_out_b, w_out_b, g)


def _ffn_out_copies(acc_ref, sems, yp_hbm, ys_hbm, tile, slot, kind):
    if kind == "first":
        return [pltpu.make_async_copy(acc_ref.at[slot, pl.ds(P_START, TM - P_START)],
                                      yp_hbm.at[pl.ds(0, TM - P_START)], sems.at[slot])]
    if kind == "middle":
        return [pltpu.make_async_copy(acc_ref.at[slot],
                                      yp_hbm.at[pl.ds(pl.multiple_of(tile * TM - P_START, CHUNK), TM)],
                                      sems.at[slot])]
    return [pltpu.make_async_copy(acc_ref.at[slot, pl.ds(0, SAMPLE_LOCAL)],
                                  yp_hbm.at[pl.ds(SEQ - SAMPLE_LOCAL, SAMPLE_LOCAL)], sems.at[slot]),
            pltpu.make_async_copy(acc_ref.at[slot, pl.ds(SAMPLE_LOCAL, DEC_BATCH)], ys_hbm, sems.at[slot])]


def _ffn_kernel(x_ref, gpre_ref, gpost_ref, wg_ref, wv_ref, cw_ref, cb_ref, wd_ref, buf0_ref, buf1_ref,
                yp_hbm, ys_hbm, tail_ref, h_ref, act_ref, carry_ref, acc_ref, sems):
    i = pl.program_id(0)
    j = pl.program_id(1)
    last_i = pl.num_programs(0) - 1
    slot = i % 2
    col = pl.ds(pl.multiple_of(j * FF_TN, FF_TN), FF_TN)
    out_copies = functools.partial(_ffn_out_copies, acc_ref, sems, yp_hbm, ys_hbm)

    @pl.when(j == 0)
    def _():
        @pl.when(i == 2)
        def _():
            for cp in out_copies(0, slot, "first"):
                cp.wait()

        @pl.when(i > 2)
        def _():
            for cp in out_copies(i - 2, slot, "middle"):
                cp.wait()

        h_ref[...] = _rms(x_ref[...], gpre_ref[...]).astype(bf16)
        acc_ref[slot] = jnp.zeros((TM, D_MODEL), f32)

    @pl.when(i == 0)
    def _():
        carry_ref[:, col] = jnp.zeros((8, FF_TN), f32)

    h = h_ref[...]
    row = lax.broadcasted_iota(jnp.int32, (TM, 1), 0)
    g = jnp.where(row + i * TM >= PAD, _dot(h, wg_ref[...]), 0.0)
    val = _dot(h, wv_ref[...])
    prev = carry_ref[:, col]
    g1 = jnp.where(row == 0, prev[7:8], pltpu.roll(g, 1, 0))
    g2 = jnp.where(row == 0, prev[6:7], jnp.where(row == 1, prev[7:8], pltpu.roll(g, 2, 0)))
    w0, w1, w2 = cw_ref[0:1, :], cw_ref[1:2, :], cw_ref[2:3, :]
    y = cb_ref[...] + w0 * g2 + w1 * g1 + w2 * g
    act_ref[...] = (_silu(y) * val).astype(bf16)
    carry_ref[:, col] = g[TM - 8:, :]
    tail_ref[0] = g[TM - TAIL_ROWS:, :]

    @pl.when(i == last_i)
    def _():
        ys = cb_ref[...] + w0 * buf0_ref[...] + w1 * buf1_ref[...] + w2 * g[SAMPLE_LOCAL:, :]
        act_ref[SAMPLE_LOCAL:, :] = (_silu(ys) * val[SAMPLE_LOCAL:, :]).astype(bf16)

    acc_ref[slot] += _dot(act_ref[...], wd_ref[...])

    @pl.when(j == pl.num_programs(1) - 1)
    def _():
        acc_ref[slot] = x_ref[...] + _rms(acc_ref[slot], gpost_ref[...])

        @pl.when(i == 0)
        def _():
            for cp in out_copies(i, slot, "first"):
                cp.start()

        @pl.when((i > 0) & (i < last_i))
        def _():
            for cp in out_copies(i, slot, "middle"):
                cp.start()

        @pl.when(i == last_i)
        def _():
            for cp in out_copies(i, slot, "last"):
                cp.start()
            for cp in out_copies(i - 1, 1 - slot, "middle"):
                cp.wait()
            for cp in out_copies(i, slot, "last"):
                cp.wait()


def _ffn(x1, g_pre, g_post, w_up_b, conv_w, conv_b, w_down_b, buf0, buf1):
    n_ff = D_FF // FF_TN
    n_tiles = R_ALL // TM
    sample_buf = pl.BlockSpec((DEC_BATCH, FF_TN), lambda i, j: (0, jnp.where(i == n_tiles - 1, j, 0)))
    return pl.pallas_call(
        _ffn_kernel,
        grid=(R_ALL // TM, n_ff),
        in_specs=[
            pl.BlockSpec((TM, D_MODEL), lambda i, j: (i, 0)),
            pl.BlockSpec((1, D_MODEL), lambda i, j: (0, 0)),
            pl.BlockSpec((1, D_MODEL), lambda i, j: (0, 0)),
            pl.BlockSpec((D_MODEL, FF_TN), lambda i, j: (0, j)),
            pl.BlockSpec((D_MODEL, FF_TN), lambda i, j: (0, j + n_ff)),
            pl.BlockSpec((3, FF_TN), lambda i, j: (0, j)),
            pl.BlockSpec((1, FF_TN), lambda i, j: (0, j)),
            pl.BlockSpec((FF_TN, D_MODEL), lambda i, j: (j, 0)),
            sample_buf, sample_buf,
        ],
        out_specs=[
            pl.BlockSpec(memory_space=pl.ANY),
            pl.BlockSpec(memory_space=pl.ANY),
            pl.BlockSpec((1, TAIL_ROWS, FF_TN), lambda i, j: (i, 0, j)),
        ],
        out_shape=[
            jax.ShapeDtypeStruct((SEQ, D_MODEL), f32),
            jax.ShapeDtypeStruct((DEC_BATCH, D_MODEL), f32),
            jax.ShapeDtypeStruct((R_ALL // TM, TAIL_ROWS, D_FF), f32),
        ],
        scratch_shapes=[
            pltpu.VMEM((TM, D_MODEL), bf16),
            pltpu.VMEM((TM, FF_TN), bf16),
            pltpu.VMEM((8, D_FF), f32),
            pltpu.VMEM((2, TM, D_MODEL), f32),
            pltpu.SemaphoreType.DMA((2,)),
        ],
        compiler_params=_params(("arbitrary", "arbitrary")),
        name="conv_ffn",
    )(x1, g_pre, g_post, w_up_b, w_up_b, conv_w, conv_b, w_down_b, buf0, buf1)


def kernel(x_prompt, x_sample, cache_ckv, cache_krope, state_hgrn, state_conv, page_table, meta_tokens,
           lb_raw, g_mix_pre, g_mix_post, g_ffn_pre, g_ffn_post, w_in, hgrn_o_norm, q_a_norm, kv_a_norm,
           w_q_up, w_kv_up, w_out, w_ffn_up, conv_w, conv_b, w_ffn_down):
    x_head = jnp.concatenate([jnp.zeros((PAD, D_MODEL), f32), meta_tokens.astype(f32)], axis=0)
    x_p = x_prompt[0]
    x_s = x_sample[:, 0]

    w_in_t = w_in[0].T.astype(bf16)
    w_in_tail = jnp.pad(w_in_t[IN_FULL_TILES * IN_TN:], ((0, IN_PAD - IN_TOTAL), (0, 0)))
    wq = w_q_up[0].reshape(Q_RANK, MLA_HEADS, QK_DIM)
    wq_p = jnp.concatenate([wq[:, :, :QK_NOPE].reshape(Q_RANK, -1), wq[:, :, QK_NOPE:].reshape(Q_RANK, -1)],
                           axis=1).astype(bf16)
    wkv = w_kv_up[0].astype(bf16)
    w_out_b = w_out[0].astype(bf16)
    w_up_b = w_ffn_up[0].astype(bf16)
    w_down_b = w_ffn_down[0].astype(bf16)

    proj = _in_proj(x_head, x_p, x_s, g_mix_pre, w_in_t, w_in_tail)
    qh, kh, vh, ckv, kpe = _mla_proj(proj, q_a_norm, kv_a_norm, wq_p, wkv)

    o_a = _attention(qh, kh, vh)
    o_h, s_p = _hgrn_prompt(proj, lb_raw, hgrn_o_norm)

    o_h, s_s = _hgrn_sample(proj, lb_raw, hgrn_o_norm, state_hgrn[0], o_h)
    q_lat = _q_latent(qh, wkv).transpose(1, 0, 2)
    q_pe_s = qh[:, P_ROWS:, QK_NOPE:].transpose(1, 0, 2)
    ckv_s = ckv[P_ROWS:]
    kpe_s = kpe[P_ROWS:]
    o_lat = _paged_attention(page_table, q_lat, q_pe_s, ckv_s[:, None, :], kpe_s[:, None, :],
                             cache_ckv, cache_krope.transpose(0, 1, 3, 2))
    o_a = jnp.concatenate([o_a, _v_up(o_lat.transpose(1, 0, 2), wkv)], axis=0)

    x1 = _out_proj(o_h, o_a, x_head, x_p, x_s, w_out_b, g_mix_post)
    y_p, y_s, tails = _ffn(x1, g_ffn_pre, g_ffn_post, w_up_b, conv_w[0], conv_b, w_down_b,
                           state_conv[0, :, 0], state_conv[0, :, 1])
    tail = tails[-1]

    n_tail = TAIL_ROWS - DEC_BATCH
    conv_p = tail[n_tail - 2:n_tail][None, None]
    conv_s = jnp.stack([state_conv[0, :, 1], tail[n_tail:]], axis=1)[None]
    return (y_p[None], y_s[:, None],
            ckv[PAD:P_ROWS][None, None], kpe[PAD:P_ROWS][None, None],
            ckv_s[None, :, None], kpe_s[None, :, None],
            s_p[None, None], s_s[None], conv_p, conv_s)
```
